```python
import jax, jax.numpy as jnp
from jax import lax
import numpy as np

D_MODEL = 1024
BATCH = 32
SEQ = 2048
DEPTH = 2
DEC_BATCH = 32
DEC_SEQ = 16
PAST_LEN = 1024

CHUNK = 64
HEAD_DIM = 64
A_HEADS = (3 * D_MODEL // 8) // HEAD_DIM
B_HEADS = (3 * D_MODEL // 8) // HEAD_DIM
A_WIDTH = A_HEADS * HEAD_DIM
B_WIDTH = B_HEADS * HEAD_DIM
C_WIDTH = D_MODEL // 4
MIX_WIDTH = A_WIDTH + B_WIDTH + C_WIDTH
A_LEFT_CHUNKS = 8
A_WINDOW = A_LEFT_CHUNKS * CHUNK
REL_CLIP = 128
IDX_HEADS = 8
IDX_DIM = 64
TOPK_MAX = 256
POOL_WINDOWS = (2, 4, 8, 16)
POOL_GROUPS = len(POOL_WINDOWS)
POOL_GROUP = C_WIDTH // POOL_GROUPS
POOL_MAX = 16
D_FF = 4 * D_MODEL
RMS_EPS = 1e-6
PROJ_SIZES = (A_WIDTH, A_WIDTH, A_WIDTH, B_WIDTH, B_WIDTH, B_WIDTH,
              IDX_HEADS * IDX_DIM, IDX_DIM, IDX_HEADS, C_WIDTH)
PROJ_WIDTH = sum(PROJ_SIZES)

kernel_name = 'hybrid_stream_encoder_step'


def rms_norm(x, g):
    x32 = x.astype(jnp.float32)
    y = x32 * lax.rsqrt(jnp.mean(x32 * x32, axis=-1, keepdims=True) + RMS_EPS)
    return (y * g.astype(jnp.float32)).astype(x.dtype)


def project(h, w_in):
    B, T, _ = h.shape
    z = h @ w_in
    parts = []
    off = 0
    for size in PROJ_SIZES:
        parts.append(z[..., off:off + size])
        off += size
    qa, ka, va, qb, kb, vb, qi, ki, wi, u = parts
    hd = lambda t, n: t.reshape(B, T, n, -1)
    return (hd(qa, A_HEADS), hd(ka, A_HEADS), hd(va, A_HEADS),
            hd(qb, B_HEADS), hd(kb, B_HEADS), hd(vb, B_HEADS),
            hd(qi, IDX_HEADS), ki, wi, u)


def band_attention(q, k, v, q_pos, k_pos, rel_bias):
    s = jnp.einsum('bqhd,bkhd->bhqk', q, k, preferred_element_type=jnp.float32) * (HEAD_DIM ** -0.5)
    rel = jnp.clip(q_pos[:, None] - k_pos[None, :], -REL_CLIP, REL_CLIP) + REL_CLIP
    bias = rel_bias.astype(jnp.float32)[:, rel]
    qc = q_pos // CHUNK
    kc = k_pos // CHUNK
    valid = ((kc[None, :] <= qc[:, None]) & (kc[None, :] >= qc[:, None] - A_LEFT_CHUNKS)
             & (k_pos[None, :] >= 0))
    s = jnp.where(valid, s + bias, -jnp.inf)
    p = jax.nn.softmax(s, axis=-1).astype(v.dtype)
    return jnp.einsum('bhqk,bkhd->bqhd', p, v)


def band_attention_prompt(q, k, v, rel_bias):
    B, S, H, D = q.shape
    nc = S // CHUNK
    band = A_WINDOW + CHUNK
    kp = jnp.pad(k, ((0, 0), (A_WINDOW, 0), (0, 0), (0, 0)))
    vp = jnp.pad(v, ((0, 0), (A_WINDOW, 0), (0, 0), (0, 0)))
    q_blocks = jnp.moveaxis(q.reshape(B, nc, CHUNK, H, D), 1, 0)

    def one_chunk(args):
        c, qb = args
        start = c * CHUNK
        kb = lax.dynamic_slice_in_dim(kp, start, band, axis=1)
        vb = lax.dynamic_slice_in_dim(vp, start, band, axis=1)
        q_pos = start + jnp.arange(CHUNK)
        k_pos = start - A_WINDOW + jnp.arange(band)
        return band_attention(qb, kb, vb, q_pos, k_pos, rel_bias)

    out = lax.map(one_chunk, (jnp.arange(nc), q_blocks))
    return jnp.moveaxis(out, 0, 1).reshape(B, S, H * D)


def sparse_attention(q, q_idx, w, k, v, k_idx, q_pos, k_pos, n_sel):
    logits = jax.nn.relu(jnp.einsum('bqhd,bsd->bqhs', q_idx, k_idx, preferred_element_type=jnp.float32))
    score = jnp.einsum('bqhs,bqh->bqs', logits, w.astype(jnp.float32)) * (IDX_HEADS ** -0.5 * IDX_DIM ** -0.5)
    qc = q_pos // CHUNK
    kc = k_pos // CHUNK
    score = jnp.where(kc[None, None, :] <= qc[None, :, None], score, -jnp.inf)
    _, sel = lax.top_k(score, n_sel)
    k_sel = jax.vmap(lambda kk, ii: kk[ii])(k, sel)
    v_sel = jax.vmap(lambda vv, ii: vv[ii])(v, sel)
    ok = kc[sel] <= qc[None, :, None]
    s = jnp.einsum('bqhd,bqnhd->bhqn', q, k_sel, preferred_element_type=jnp.float32) * (HEAD_DIM ** -0.5)
    s = jnp.where(ok[:, None], s, -jnp.inf)
    p = jax.nn.softmax(s, axis=-1).astype(v.dtype)
    return jnp.einsum('bhqn,bqnhd->bqhd', p, v_sel)


def sparse_attention_prompt(q, k, v, q_idx, k_idx, w):
    B, S, H, D = q.shape
    nc = S // CHUNK
    n_sel = min(TOPK_MAX, S // 4)
    k_pos = jnp.arange(S)
    to_blocks = lambda t: jnp.moveaxis(t.reshape((B, nc, CHUNK) + t.shape[2:]), 1, 0)

    def one_block(args):
        c, qb, qib, wb = args
        q_pos = c * CHUNK + jnp.arange(CHUNK)
        return sparse_attention(qb, qib, wb, k, v, k_idx, q_pos, k_pos, n_sel)

    out = lax.map(one_block, (jnp.arange(nc), to_blocks(q), to_blocks(q_idx), to_blocks(w)))
    return jnp.moveaxis(out, 0, 1).reshape(B, S, H * D)


def pool_mixer(u, prev, pos0, w_pool, pool_scale):
    B, T, C = u.shape
    L = POOL_MAX - 1
    full = jnp.concatenate([prev, u], axis=1).astype(jnp.float32)
    cs = jnp.pad(jnp.cumsum(full, axis=1), ((0, 0), (1, 0), (0, 0)))
    pos = pos0 + jnp.arange(T)
    means = []
    for g, win in enumerate(POOL_WINDOWS):
        sl = slice(g * POOL_GROUP, (g + 1) * POOL_GROUP)
        total = cs[:, L + 1:L + 1 + T, sl] - cs[:, L + 1 - win:L + 1 - win + T, sl]
        cnt = jnp.minimum(pos + 1, win).astype(jnp.float32)
        means.append(total / cnt[None, :, None])
    pooled = (jnp.concatenate(means, axis=-1) - u.astype(jnp.float32)).astype(u.dtype)
    mixed = jnp.einsum('btgc,gcd->btgd', pooled.reshape(B, T, POOL_GROUPS, POOL_GROUP), w_pool)
    return mixed.reshape(B, T, C) * pool_scale


def channel_block(x, g_pre, w1, w2, g_post):
    h = rms_norm(x, g_pre)
    m = jnp.square(jax.nn.relu(h @ w1)) @ w2
    return x + rms_norm(m, g_post)


def layer_prompt(x, g_pre_mix, w_in, rel_bias, w_pool, pool_scale, w_out, g_post_mix,
                 g_pre_mlp, w_ff1, w_ff2, g_post_mlp):
    B, S, _ = x.shape
    h = rms_norm(x, g_pre_mix)
    qa, ka, va, qb, kb, vb, qi, ki, wi, u = project(h, w_in)
    ya = band_attention_prompt(qa, ka, va, rel_bias)
    yb = sparse_attention_prompt(qb, kb, vb, qi, ki, wi)
    prev = jnp.zeros((B, POOL_MAX - 1, C_WIDTH), u.dtype)
    yc = pool_mixer(u, prev, 0, w_pool, pool_scale)
    y = jnp.concatenate([ya, yb, yc], axis=-1) @ w_out
    x = x + rms_norm(y, g_post_mix)
    x = channel_block(x, g_pre_mlp, w_ff1, w_ff2, g_post_mlp)
    n_a = min(A_WINDOW, S)
    return x, (ka[:, S - n_a:], va[:, S - n_a:], kb, vb, ki, u[:, S - (POOL_MAX - 1):])


def layer_sample(x, c_a_k, c_a_v, c_b_k, c_b_v, c_b_kidx, s_pool, g_pre_mix, w_in, rel_bias,
                 w_pool, pool_scale, w_out, g_post_mix, g_pre_mlp, w_ff1, w_ff2, g_post_mlp):
    B, T, _ = x.shape
    P = c_b_k.shape[1]
    n_a = c_a_k.shape[1]
    h = rms_norm(x, g_pre_mix)
    qa, ka, va, qb, kb, vb, qi, ki, wi, u = project(h, w_in)
    q_pos = P + jnp.arange(T)
    ya = band_attention(qa, jnp.concatenate([c_a_k, ka], axis=1), jnp.concatenate([c_a_v, va], axis=1),
                        q_pos, P - n_a + jnp.arange(n_a + T), rel_bias).reshape(B, T, A_WIDTH)
    yb = sparse_attention(qb, qi, wi, jnp.concatenate([c_b_k, kb], axis=1),
                          jnp.concatenate([c_b_v, vb], axis=1), jnp.concatenate([c_b_kidx, ki], axis=1),
                          q_pos, jnp.arange(P + T), min(TOPK_MAX, (P + T) // 4)).reshape(B, T, B_WIDTH)
    yc = pool_mixer(u, s_pool, P, w_pool, pool_scale)
    y = jnp.concatenate([ya, yb, yc], axis=-1) @ w_out
    x = x + rms_norm(y, g_post_mix)
    x = channel_block(x, g_pre_mlp, w_ff1, w_ff2, g_post_mlp)
    new_pool = jnp.concatenate([s_pool, u], axis=1)[:, T:]
    return x, (ka, va, kb, vb, ki, new_pool)


def setup_inputs(seed: int = 0) -> dict:
    key = jax.random.key(seed)
    ks = jax.random.split(key, 24)
    nrm = lambda k, shape, scale: jax.random.normal(k, shape, jnp.float32) * scale
    n_a = min(A_WINDOW, PAST_LEN)
    return {
        'x_prompt': nrm(ks[0], (BATCH, SEQ, D_MODEL), 1.0),
        'x_sample': nrm(ks[1], (DEC_BATCH, DEC_SEQ, D_MODEL), 1.0),
        'cache_a_k': nrm(ks[2], (DEPTH, DEC_BATCH, n_a, A_HEADS, HEAD_DIM), 1.0),
        'cache_a_v': nrm(ks[3], (DEPTH, DEC_BATCH, n_a, A_HEADS, HEAD_DIM), 1.0),
        'cache_b_k': nrm(ks[4], (DEPTH, DEC_BATCH, PAST_LEN, B_HEADS, HEAD_DIM), 1.0),
        'cache_b_v': nrm(ks[5], (DEPTH, DEC_BATCH, PAST_LEN, B_HEADS, HEAD_DIM), 1.0),
        'cache_b_kidx': nrm(ks[6], (DEPTH, DEC_BATCH, PAST_LEN, IDX_DIM), 1.0),
        'state_pool': nrm(ks[7], (DEPTH, DEC_BATCH, POOL_MAX - 1, C_WIDTH), 1.0),
        'g_pre_mix': 1.0 + nrm(ks[8], (DEPTH, D_MODEL), 0.05),
        'w_in': nrm(ks[9], (DEPTH, D_MODEL, PROJ_WIDTH), D_MODEL ** -0.5),
        'rel_bias': nrm(ks[10], (DEPTH, A_HEADS, 2 * REL_CLIP + 1), 0.5),
        'w_pool': nrm(ks[11], (DEPTH, POOL_GROUPS, POOL_GROUP, POOL_GROUP), POOL_GROUP ** -0.5),
        'pool_scale': 1.0 + nrm(ks[12], (DEPTH, C_WIDTH), 0.1),
        'w_out': nrm(ks[13], (DEPTH, MIX_WIDTH, D_MODEL), MIX_WIDTH ** -0.5),
        'g_post_mix': 1.0 + nrm(ks[14], (DEPTH, D_MODEL), 0.05),
        'g_pre_mlp': 1.0 + nrm(ks[15], (DEPTH, D_MODEL), 0.05),
        'w_ff1': nrm(ks[16], (DEPTH, D_MODEL, D_FF), D_MODEL ** -0.5),
        'w_ff2': nrm(ks[17], (DEPTH, D_FF, D_MODEL), D_FF ** -0.5),
        'g_post_mlp': 1.0 + nrm(ks[18], (DEPTH, D_MODEL), 0.05),
    }


def reference(x_prompt, x_sample, cache_a_k, cache_a_v, cache_b_k, cache_b_v, cache_b_kidx, state_pool,
              g_pre_mix, w_in, rel_bias, w_pool, pool_scale, w_out, g_post_mix,
              g_pre_mlp, w_ff1, w_ff2, g_post_mlp):
    xp = x_prompt
    xs = x_sample
    p_states = []
    s_states = []
    for l in range(DEPTH):
        weights = (g_pre_mix[l], w_in[l], rel_bias[l], w_pool[l], pool_scale[l], w_out[l],
                   g_post_mix[l], g_pre_mlp[l], w_ff1[l], w_ff2[l], g_post_mlp[l])
        xp, sp = layer_prompt(xp, *weights)
        xs, ss = layer_sample(xs, cache_a_k[l], cache_a_v[l], cache_b_k[l], cache_b_v[l],
                              cache_b_kidx[l], state_pool[l], *weights)
        p_states.append(sp)
        s_states.append(ss)
    stk = lambda states, i: jnp.stack([st[i] for st in states], axis=0)
    return (xp, xs,
            stk(p_states, 0), stk(p_states, 1), stk(p_states, 2), stk(p_states, 3), stk(p_states, 4), stk(p_states, 5),
            stk(s_states, 0), stk(s_states, 1), stk(s_states, 2), stk(s_states, 3), stk(s_states, 4), stk(s_states, 5))
```

```python
import functools

import jax
import jax.numpy as jnp
from jax import lax
from jax.experimental import pallas as pl
from jax.experimental.pallas import tpu as pltpu

D_MODEL = 1024
CHUNK = 64
CHUNK_SHIFT = 6
HEAD_DIM = 64
N_HEADS = 6
ATT_WIDTH = N_HEADS * HEAD_DIM
N_PAIRS = N_HEADS // 2
C_WIDTH = 256
A_LEFT_CHUNKS = 8
A_WINDOW = A_LEFT_CHUNKS * CHUNK
REL_CLIP = 128
IDX_HEADS = 8
IDX_DIM = 64
TOPK = 256
POOL_WINDOWS = (2, 4, 8, 16)
POOL_GROUP = 64
POOL_MAX = 16
POOL_PAD = 32
D_FF = 4 * D_MODEL
RMS_EPS = 1e-6
IDX_SCALE = IDX_HEADS ** -0.5 * IDX_DIM ** -0.5
ATT_SCALE = HEAD_DIM ** -0.5

LANES = 128
KEY_TILE = 256
ROW_TILE = 512
NEG = -1e30
BISECT_STEPS = 32
VMEM_LIMIT = 48 * 1024 * 1024

_OFF_QA, _OFF_KA, _OFF_QB, _OFF_KB, _OFF_QI, _OFF_KI, _OFF_WI, _OFF_U, _OFF_END = (
    0, 384, 1152, 1536, 2304, 2816, 2880, 2888, 3144)
PROJ_COLS = 3328

F32 = jnp.float32
BF16 = jnp.bfloat16


def _const_spec(shape):
    zeros = (0,) * len(shape)
    return pl.BlockSpec(shape, lambda *_: zeros, pipeline_mode=pl.Buffered(1))


def _params(semantics):
    return pltpu.CompilerParams(dimension_semantics=semantics, vmem_limit_bytes=VMEM_LIMIT)


def _rms(x, g):
    ms = jnp.mean(x * x, axis=-1, keepdims=True)
    return x * lax.rsqrt(ms + RMS_EPS) * g


def _dot(a, b):
    return jnp.dot(a, b, preferred_element_type=F32)


def _dot_t(a, b):
    return lax.dot_general(a, b, (((1,), (1,)), ((), ())), preferred_element_type=F32)


def _softmax_step(s, vt, m, l, acc):
    m_new = jnp.maximum(m, jnp.max(s, axis=-1, keepdims=True))
    alpha = jnp.exp(m - m_new)
    p = jnp.exp(s - m_new)
    l = alpha * l + jnp.sum(p, axis=-1, keepdims=True)
    acc = alpha * acc + _dot(p.astype(BF16), vt)
    return m_new, l, acc


def _half_masks(rows):
    lane = lax.broadcasted_iota(jnp.int32, (rows, LANES), 1)
    first = lane < HEAD_DIM
    m0 = jnp.where(first, 1.0, 0.0).astype(BF16)
    m1 = jnp.where(first, 0.0, 1.0).astype(BF16)
    return first, m0, m1


def _proj_kernel(x_ref, g_ref, w_ref, qa_ref, ka_ref, va_ref, qb_ref, kb_ref, vb_ref,
                 qi_ref, ki2_ref, wi_ref, u_ref):
    h = _rms(x_ref[...], g_ref[...]).astype(BF16)
    z = _dot(h, w_ref[:, 0:768])
    qa_ref[...] = z[:, 0:384].astype(BF16)
    ka_ref[...] = z[:, 384:768]
    z = _dot(h, w_ref[:, 768:1536])
    va_ref[...] = z[:, 0:384]
    qb_ref[...] = z[:, 384:768].astype(BF16)
    z = _dot(h, w_ref[:, 1536:2304])
    kb_ref[...] = z[:, 0:384]
    vb_ref[...] = z[:, 384:768]
    qi_ref[...] = _dot(h, w_ref[:, 2304:2816]).astype(BF16)
    z = _dot(h, w_ref[:, 2816:3328])
    ki2_ref[...] = z[:, 0:128]
    wi_ref[...] = z[:, 128:256]
    u_ref[...] = z[:, 256:512]


def _proj(x, g, w):
    n = x.shape[0]
    assert n % ROW_TILE == 0
    widths = (ATT_WIDTH,) * 6 + (IDX_HEADS * IDX_DIM, LANES, LANES, C_WIDTH)
    dtypes = (BF16, F32, F32, BF16, F32, F32, BF16, F32, F32, F32)
    row = lambda i: (i, 0)
    return pl.pallas_call(
        _proj_kernel,
        out_shape=tuple(jax.ShapeDtypeStruct((n, wd), dt) for wd, dt in zip(widths, dtypes)),
        grid=(n // ROW_TILE,),
        in_specs=[pl.BlockSpec((ROW_TILE, D_MODEL), row),
                  _const_spec((1, D_MODEL)),
                  _const_spec((D_MODEL, PROJ_COLS))],
        out_specs=tuple(pl.BlockSpec((ROW_TILE, wd), row) for wd in widths),
        compiler_params=_params(("parallel",)),
        name="proj",
    )(x, g, w)


def _proj_weight(w_in):
    qa = w_in[:, _OFF_QA:_OFF_KA] * ATT_SCALE
    kava = w_in[:, _OFF_KA:_OFF_QB]
    qb = w_in[:, _OFF_QB:_OFF_KB] * ATT_SCALE
    kbvb = w_in[:, _OFF_KB:_OFF_QI]
    qi = w_in[:, _OFF_QI:_OFF_KI]
    ki = w_in[:, _OFF_KI:_OFF_WI]
    wi = w_in[:, _OFF_WI:_OFF_U]
    u = w_in[:, _OFF_U:_OFF_END]
    pad = jnp.zeros((D_MODEL, LANES - IDX_HEADS), w_in.dtype)
    w = jnp.concatenate([qa, kava, qb, kbvb, qi, ki, ki, wi, pad, u], axis=1)
    assert w.shape[1] == PROJ_COLS
    return w.astype(BF16)


def _band_kernel(q_ref, k_ref, v_ref, tab_ref, o_ref, *, tq, shift):
    j = pl.program_id(1)
    first, m0, m1 = _half_masks(tq)
    for pair in range(N_PAIRS):
        cols = slice(LANES * pair, LANES * (pair + 1))
        qp = q_ref[0, :, cols]
        outs = []
        for hh in range(2):
            qm = qp * (m0 if hh == 0 else m1)
            m = jnp.full((tq, 1), NEG, F32)
            l = jnp.zeros((tq, 1), F32)
            acc = jnp.zeros((tq, LANES), F32)
            for t in range(3):
                blk = j + (t - shift)
                start = pl.multiple_of(jnp.maximum(blk, 0) * KEY_TILE, KEY_TILE)
                kt = k_ref[0, pl.ds(start, KEY_TILE), cols].astype(BF16)
                vt = v_ref[0, pl.ds(start, KEY_TILE), cols].astype(BF16)
                s = _dot_t(qm, kt) + tab_ref[2 * pair + hh, t]
                s = jnp.where(blk >= 0, s, NEG)
                m, l, acc = _softmax_step(s, vt, m, l, acc)
            outs.append(acc / l)
        o_ref[0, :, cols] = jnp.where(first, outs[0], outs[1]).astype(BF16)


def _band(q, k, v, tab, *, tq, shift):
    bn, tq_total, _ = q.shape
    tk_total = k.shape[1]
    kern = functools.partial(_band_kernel, tq=tq, shift=shift)
    qmap = lambda b, j: (b, j, 0)
    kmap = lambda b, j: (b, 0, 0)
    return pl.pallas_call(
        kern,
        out_shape=jax.ShapeDtypeStruct((bn, tq_total, ATT_WIDTH), BF16),
        grid=(bn, tq_total // tq),
        in_specs=[pl.BlockSpec((1, tq, ATT_WIDTH), qmap),
                  pl.BlockSpec((1, tk_total, ATT_WIDTH), kmap),
                  pl.BlockSpec((1, tk_total, ATT_WIDTH), kmap),
                  _const_spec(tab.shape)],
        out_specs=pl.BlockSpec((1, tq, ATT_WIDTH), qmap),
        compiler_params=_params(("parallel", "arbitrary")),
        name="band",
    )(q, k, v, tab)


def _band_table(rel_bias, q_pos, k_pos, k_real):
    rel = jnp.clip(q_pos[:, None] - k_pos[None, :], -REL_CLIP, REL_CLIP) + REL_CLIP
    bias = rel_bias.astype(F32)[:, rel]
    qc = q_pos // CHUNK
    kc = k_pos // CHUNK
    valid = ((kc[None, :] <= qc[:, None]) & (kc[None, :] >= qc[:, None] - A_LEFT_CHUNKS)
             & k_real[None, :])
    tab = jnp.where(valid[None], bias, NEG)
    h, tq, tk = tab.shape
    return tab.reshape(h, tq, tk // KEY_TILE, KEY_TILE).transpose(0, 2, 1, 3)


def _sparse_kernel(qi_ref, wi_ref, ki2_ref, qb_ref, kb_ref, vb_ref, tri_ref, o_ref,
                   sc_ref, mb_ref, *, tq, q_pos0, n_keys):
    j = pl.program_id(1)
    q_first = q_pos0 + j * tq
    last_chunk = lax.shift_right_logical(q_first + (tq - 1), CHUNK_SHIFT)
    k_end = jnp.minimum((last_chunk + 1) * CHUNK, n_keys)
    n_tiles = lax.shift_right_logical(k_end + (KEY_TILE - 1), 8)

    first, m0, m1 = _half_masks(tq)
    q_pos = q_first + lax.broadcasted_iota(jnp.int32, (tq, 1), 0)
    k_lim = jnp.minimum((lax.shift_right_logical(q_pos, CHUNK_SHIFT) + 1) * CHUNK, n_keys)
    col = lax.broadcasted_iota(jnp.int32, (tq, KEY_TILE), 1)

    w = wi_ref[0]
    w_cols = [w[:, h:h + 1] for h in range(IDX_HEADS)]
    q_idx = [qi_ref[0, :, LANES * (h // 2):LANES * (h // 2 + 1)] * (m0 if h % 2 == 0 else m1)
             for h in range(IDX_HEADS)]

    def score_tile(t, carry):
        rmin, rmax = carry
        start = pl.multiple_of(t * KEY_TILE, KEY_TILE)
        kt = ki2_ref[0, pl.ds(start, KEY_TILE), :].astype(BF16)
        acc = jnp.zeros((tq, KEY_TILE), F32)
        for h in range(IDX_HEADS):
            acc = acc + jnp.maximum(_dot_t(q_idx[h], kt), 0.0) * w_cols[h]
        sc = acc * IDX_SCALE
        adm = (start + col) < k_lim
        sc_ref[t] = jnp.where(adm, sc, -jnp.inf)
        rmax = jnp.maximum(rmax, jnp.max(jnp.where(adm, sc, -jnp.inf), axis=-1, keepdims=True))
        rmin = jnp.minimum(rmin, jnp.min(jnp.where(adm, sc, jnp.inf), axis=-1, keepdims=True))
        return rmin, rmax

    rmin, rmax = lax.fori_loop(
        0, n_tiles, score_tile,
        (jnp.full((tq, 1), jnp.inf, F32), jnp.full((tq, 1), -jnp.inf, F32)))

    def count_ge(thr):
        def body(t, c):
            return c + jnp.where(sc_ref[t] >= thr, 1.0, 0.0)
        c = lax.fori_loop(0, n_tiles, body, jnp.zeros((tq, KEY_TILE), F32))
        return jnp.sum(c, axis=-1, keepdims=True)

    def bisect(_, carry):
        lo, hi = carry
        mid = lo + (hi - lo) * 0.5
        enough = count_ge(mid) >= TOPK
        return jnp.where(enough, mid, lo), jnp.where(enough, hi, mid)

    hi0 = rmax + jnp.maximum(jnp.abs(rmax), 1e-30) * (2.0 ** -10)
    lo, hi = lax.fori_loop(0, BISECT_STEPS, bisect, (rmin, hi0))

    need = TOPK - count_ge(hi)

    def mask_tile(t, run):
        x = sc_ref[t]
        inr = jnp.where(x >= lo, jnp.where(x < hi, 1.0, 0.0), 0.0)
        rank = _dot(inr.astype(BF16), tri_ref[...]) + run
        mb_ref[t] = jnp.where(
            x >= hi, 0.0, jnp.where(inr > 0.0, jnp.where(rank <= need, 0.0, NEG), NEG))
        return run + jnp.sum(inr, axis=-1, keepdims=True)

    lax.fori_loop(0, n_tiles, mask_tile, jnp.zeros((tq, 1), F32))

    for pair in range(N_PAIRS):
        cols = slice(LANES * pair, LANES * (pair + 1))
        qp = qb_ref[0, :, cols]
        outs = []
        for hh in range(2):
            qm = qp * (m0 if hh == 0 else m1)

            def attend(t, carry, qm=qm, cols=cols):
                start = pl.multiple_of(t * KEY_TILE, KEY_TILE)
                kt = kb_ref[0, pl.ds(start, KEY_TILE), cols].astype(BF16)
                vt = vb_ref[0, pl.ds(start, KEY_TILE), cols].astype(BF16)
                s = _dot_t(qm, kt) + mb_ref[t]
                return _softmax_step(s, vt, *carry)

            _, l, acc = lax.fori_loop(
                0, n_tiles, attend,
                (jnp.full((tq, 1), NEG, F32), jnp.zeros((tq, 1), F32),
                 jnp.zeros((tq, LANES), F32)))
            outs.append(acc / l)
        o_ref[0, :, cols] = jnp.where(first, outs[0], outs[1]).astype(BF16)


def _sparse(qi, wi, ki2, qb, kb, vb, tri, *, tq, q_pos0, n_keys):
    bn, tq_total, _ = qb.shape
    tk_total = kb.shape[1]
    assert tk_total % KEY_TILE == 0 and n_keys <= tk_total
    max_tiles = tk_total // KEY_TILE
    kern = functools.partial(_sparse_kernel, tq=tq, q_pos0=q_pos0, n_keys=n_keys)
    qmap = lambda b, j: (b, j, 0)
    kmap = lambda b, j: (b, 0, 0)
    return pl.pallas_call(
        kern,
        out_shape=jax.ShapeDtypeStruct((bn, tq_total, ATT_WIDTH), BF16),
        grid=(bn, tq_total // tq),
        in_specs=[pl.BlockSpec((1, tq, IDX_HEADS * IDX_DIM), qmap),
                  pl.BlockSpec((1, tq, LANES), qmap),
                  pl.BlockSpec((1, tk_total, LANES), kmap),
                  pl.BlockSpec((1, tq, ATT_WIDTH), qmap),
                  pl.BlockSpec((1, tk_total, ATT_WIDTH), kmap),
                  pl.BlockSpec((1, tk_total, ATT_WIDTH), kmap),
                  _const_spec((KEY_TILE, KEY_TILE))],
        out_specs=pl.BlockSpec((1, tq, ATT_WIDTH), qmap),
        scratch_shapes=[pltpu.VMEM((max_tiles, tq, KEY_TILE), F32),
                        pltpu.VMEM((max_tiles, tq, KEY_TILE), F32)],
        compiler_params=_params(("parallel", "arbitrary")),
        name="sparse",
    )(qi, wi, ki2, qb, kb, vb, tri)


def _pool_kernel(u_ref, prev_ref, w_ref, scale_ref, o_ref, f_ref, a_ref, b_ref, *, t, pos0):
    n = t + POOL_PAD
    u = u_ref[0]
    f_ref[0:16, :] = jnp.zeros((16, C_WIDTH), F32)
    f_ref[16:32, :] = prev_ref[0]
    f_ref[pl.ds(POOL_PAD, t), :] = u
    s2 = f_ref[pl.ds(8, n - 8), :] + f_ref[pl.ds(7, n - 8), :]
    a_ref[pl.ds(8, n - 8), :] = s2
    s4 = a_ref[pl.ds(16, n - 16), :] + a_ref[pl.ds(14, n - 16), :]
    b_ref[pl.ds(16, n - 16), :] = s4
    s8 = b_ref[pl.ds(24, n - 24), :] + b_ref[pl.ds(20, n - 24), :]
    a_ref[pl.ds(24, n - 24), :] = s8
    s16 = a_ref[pl.ds(32, t), :] + a_ref[pl.ds(24, t), :]
    lane = lax.broadcasted_iota(jnp.int32, (t, C_WIDTH), 1)
    g0, g1, g2 = lane < POOL_GROUP, lane < 2 * POOL_GROUP, lane < 3 * POOL_GROUP
    total = jnp.where(g0, s2[24:], jnp.where(g1, s4[16:], jnp.where(g2, s8[8:], s16)))
    win = jnp.where(g0, POOL_WINDOWS[0],
                    jnp.where(g1, POOL_WINDOWS[1],
                              jnp.where(g2, POOL_WINDOWS[2], POOL_WINDOWS[3])))
    pos = pos0 + lax.broadcasted_iota(jnp.int32, (t, C_WIDTH), 0)
    cnt = jnp.minimum(pos + 1, win).astype(F32)
    pooled = total / cnt - u
    o_ref[0] = (_dot(pooled.astype(BF16), w_ref[...]) * scale_ref[...]).astype(BF16)


def _pool(u, prev16, w_blk, scale, *, pos0):
    bn, t, _ = u.shape
    kern = functools.partial(_pool_kernel, t=t, pos0=pos0)
    bmap = lambda b: (b, 0, 0)
    n = t + POOL_PAD
    return pl.pallas_call(
        kern,
        out_shape=jax.ShapeDtypeStruct((bn, t, C_WIDTH), BF16),
        grid=(bn,),
        in_specs=[pl.BlockSpec((1, t, C_WIDTH), bmap),
                  pl.BlockSpec((1, POOL_MAX, C_WIDTH), bmap),
                  _const_spec((C_WIDTH, C_WIDTH)),
                  _const_spec((1, C_WIDTH))],
        out_specs=pl.BlockSpec((1, t, C_WIDTH), bmap),
        scratch_shapes=[pltpu.VMEM((n, C_WIDTH), F32)] * 3,
        compiler_params=_params(("parallel",)),
        name="pool",
    )(u, prev16, w_blk, scale)


def _pool_weight(w_pool):
    w = jnp.zeros((C_WIDTH, C_WIDTH), w_pool.dtype)
    for g in range(len(POOL_WINDOWS)):
        sl = slice(g * POOL_GROUP, (g + 1) * POOL_GROUP)
        w = w.at[sl, sl].set(w_pool[g])
    return w.astype(BF16)


def _outmlp_kernel(ya_ref, yb_ref, yc_ref, x_ref, woa_ref, wob_ref, woc_ref,
                   g1_ref, g2_ref, g3_ref, w1_ref, w2_ref, o_ref):
    y = _dot(ya_ref[...], woa_ref[...]) + _dot(yb_ref[...], wob_ref[...])
    y = y + _dot(yc_ref[...], woc_ref[...])
    x1 = x_ref[...] + _rms(y, g1_ref[...])
    h = _rms(x1, g2_ref[...]).astype(BF16)
    m = jnp.zeros_like(x1)
    for f in range(D_FF // D_MODEL):
        sl = slice(f * D_MODEL, (f + 1) * D_MODEL)
        a = jnp.maximum(_dot(h, w1_ref[:, sl]), 0.0)
        m = m + _dot((a * a).astype(BF16), w2_ref[sl, :])
    o_ref[...] = x1 + _rms(m, g3_ref[...])


def _outmlp(ya, yb, yc, x, woa, wob, woc, g1, g2, g3, w1, w2):
    n = x.shape[0]
    assert n % ROW_TILE == 0
    row = lambda i: (i, 0)
    return pl.pallas_call(
        _outmlp_kernel,
        out_shape=jax.ShapeDtypeStruct((n, D_MODEL), F32),
        grid=(n // ROW_TILE,),
        in_specs=[pl.BlockSpec((ROW_TILE, ATT_WIDTH), row),
                  pl.BlockSpec((ROW_TILE, ATT_WIDTH), row),
                  pl.BlockSpec((ROW_TILE, C_WIDTH), row),
                  pl.BlockSpec((ROW_TILE, D_MODEL), row),
                  _const_spec((ATT_WIDTH, D_MODEL)),
                  _const_spec((ATT_WIDTH, D_MODEL)),
                  _const_spec((C_WIDTH, D_MODEL)),
                  _const_spec((1, D_MODEL)),
                  _const_spec((1, D_MODEL)),
                  _const_spec((1, D_MODEL)),
                  _const_spec((D_MODEL, D_FF)),
                  _const_spec((D_FF, D_MODEL))],
        out_specs=pl.BlockSpec((ROW_TILE, D_MODEL), row),
        compiler_params=_params(("parallel",)),
        name="outmlp",
    )(ya, yb, yc, x, woa, wob, woc, g1, g2, g3, w1, w2)


def _pad_rows(x, rows):
    return jnp.pad(x, ((0, 0), (0, rows - x.shape[1]), (0, 0)))


def _layer(x, weights, tri, *, seq, band_tab, band_tq, band_shift, sparse_tq, q_pos0,
           past=None):
    (g_pre_mix, w_proj, w_pool_blk, pool_scale, woa, wob, woc, g_post_mix,
     g_pre_mlp, w1, w2, g_post_mlp) = weights
    bn = x.shape[0] // seq
    qa, ka, va, qb, kb, vb, qi, ki2, wi, u = _proj(x, g_pre_mix, w_proj)
    per_batch = lambda t: t.reshape(bn, seq, t.shape[-1])
    qa, ka, va, qb, kb, vb, qi, ki2, wi, u = map(per_batch, (qa, ka, va, qb, kb, vb, qi, ki2, wi, u))

    if past is None:
        ka_all, va_all, kb_all, vb_all, ki2_all = ka, va, kb, vb, ki2
        prev16 = jnp.zeros((bn, POOL_MAX, C_WIDTH), F32)
        n_keys = seq
    else:
        c_a_k, c_a_v, c_b_k, c_b_v, c_b_kidx, c_pool = past
        flat = lambda t: t.reshape(bn, t.shape[1], ATT_WIDTH)
        band_rows = 3 * KEY_TILE
        ka_all = _pad_rows(jnp.concatenate([flat(c_a_k), ka], axis=1), band_rows)
        va_all = _pad_rows(jnp.concatenate([flat(c_a_v), va], axis=1), band_rows)
        n_keys = c_b_k.shape[1] + seq
        key_rows = -(-n_keys // KEY_TILE) * KEY_TILE
        kb_all = _pad_rows(jnp.concatenate([flat(c_b_k), kb], axis=1), key_rows)
        vb_all = _pad_rows(jnp.concatenate([flat(c_b_v), vb], axis=1), key_rows)
        c_ki2 = jnp.concatenate([c_b_kidx, c_b_kidx], axis=-1)
        ki2_all = _pad_rows(jnp.concatenate([c_ki2, ki2], axis=1), key_rows)
        prev16 = jnp.pad(c_pool, ((0, 0), (1, 0), (0, 0)))

    ya = _band(qa, ka_all, va_all, band_tab, tq=band_tq, shift=band_shift)
    yb = _sparse(qi, wi, ki2_all, qb, kb_all, vb_all, tri, tq=sparse_tq, q_pos0=q_pos0,
                 n_keys=n_keys)
    yc = _pool(u, prev16, w_pool_blk, pool_scale, pos0=q_pos0)
    flat2 = lambda t: t.reshape(bn * seq, t.shape[-1])
    x = _outmlp(flat2(ya), flat2(yb), flat2(yc), x, woa, wob, woc,
                g_post_mix, g_pre_mlp, g_post_mlp, w1, w2)
    heads = lambda t: t.reshape(bn, t.shape[1], N_HEADS, HEAD_DIM)
    return x, (heads(ka), heads(va), heads(kb), heads(vb), ki2[..., :IDX_DIM], u)


def kernel(x_prompt, x_sample, cache_a_k, cache_a_v, cache_b_k, cache_b_v, cache_b_kidx, state_pool, g_pre_mix, w_in, rel_bias, w_pool, pool_scale, w_out, g_post_mix, g_pre_mlp, w_ff1, w_ff2, g_post_mlp):
    batch, seq, _ = x_prompt.shape
    dec_batch, dec_seq, _ = x_sample.shape
    depth = w_in.shape[0]
    past_len = cache_b_k.shape[2]
    n_a = cache_a_k.shape[2]
    assert seq % KEY_TILE == 0 and (batch * seq) % ROW_TILE == 0
    assert (dec_batch * dec_seq) % ROW_TILE == 0 and n_a + dec_seq <= 3 * KEY_TILE

    tri = (jnp.arange(KEY_TILE)[:, None] <= jnp.arange(KEY_TILE)[None, :]).astype(BF16)
    p_q_pos = 2 * KEY_TILE + jnp.arange(KEY_TILE)
    p_k_pos = jnp.arange(3 * KEY_TILE)
    p_real = jnp.ones((3 * KEY_TILE,), bool)
    s_q_pos = past_len + jnp.arange(dec_seq)
    s_k_pos = past_len - n_a + jnp.arange(3 * KEY_TILE)
    s_real = jnp.arange(3 * KEY_TILE) < n_a + dec_seq

    xp = x_prompt.reshape(batch * seq, D_MODEL)
    xs = x_sample.reshape(dec_batch * dec_seq, D_MODEL)
    p_states, s_states = [], []
    row = lambda t: t.reshape(1, -1)
    for l in range(depth):
        weights = (row(g_pre_mix[l]), _proj_weight(w_in[l]), _pool_weight(w_pool[l]),
                   row(pool_scale[l]),
                   w_out[l, :ATT_WIDTH].astype(BF16),
                   w_out[l, ATT_WIDTH:2 * ATT_WIDTH].astype(BF16),
                   w_out[l, 2 * ATT_WIDTH:].astype(BF16),
                   row(g_post_mix[l]), row(g_pre_mlp[l]),
                   w_ff1[l].astype(BF16), w_ff2[l].astype(BF16), row(g_post_mlp[l]))
        xp, sp = _layer(xp, weights, tri, seq=seq,
                        band_tab=_band_table(rel_bias[l], p_q_pos, p_k_pos, p_real),
                        band_tq=KEY_TILE, band_shift=2, sparse_tq=KEY_TILE, q_pos0=0)
        xs, ss = _layer(xs, weights, tri, seq=dec_seq,
                        band_tab=_band_table(rel_bias[l], s_q_pos, s_k_pos, s_real),
                        band_tq=dec_seq, band_shift=0, sparse_tq=dec_seq, q_pos0=past_len,
                        past=(cache_a_k[l], cache_a_v[l], cache_b_k[l], cache_b_v[l],
                              cache_b_kidx[l], state_pool[l]))
        n_keep = min(A_WINDOW, seq)
        ka, va, kb, vb, ki, u = sp
        p_states.append((ka[:, seq - n_keep:], va[:, seq - n_keep:], kb, vb, ki,
                         u[:, seq - (POOL_MAX - 1):]))
        ka, va, kb, vb, ki, u = ss
        new_pool = jnp.concatenate([state_pool[l], u], axis=1)[:, dec_seq:]
        s_states.append((ka, va, kb, vb, ki, new_pool))
    stk = lambda states, i: jnp.stack([st[i] for st in states], axis=0)
    return ((xp.reshape(batch, seq, D_MODEL), xs.reshape(dec_batch, dec_seq, D_MODEL))
            + tuple(stk(p_states, i) for i in range(6))
            + tuple(stk(s_states, i) for i in range(6)))
```

```python
import functools

import jax
import jax.numpy as jnp
from jax import lax
from jax.experimental import pallas as pl
from jax.experimental.pallas import tpu as pltpu

D_MODEL = 1024
CHUNK = 64
CHUNK_SHIFT = 6
HEAD_DIM = 64
N_HEADS = 6
ATT_WIDTH = N_HEADS * HEAD_DIM
N_PAIRS = N_HEADS // 2
C_WIDTH = 256
A_LEFT_CHUNKS = 8
A_WINDOW = A_LEFT_CHUNKS * CHUNK
REL_CLIP = 128
IDX_HEADS = 8
IDX_DIM = 64
TOPK = 256
POOL_WINDOWS = (2, 4, 8, 16)
POOL_GROUP = 64
POOL_MAX = 16
POOL_PAD = 32
D_FF = 4 * D_MODEL
RMS_EPS = 1e-6
IDX_SCALE = IDX_HEADS ** -0.5 * IDX_DIM ** -0.5
ATT_SCALE = HEAD_DIM ** -0.5

LANES = 128
KEY_TILE = 256
ROW_TILE = 512
NEG = -1e30
SEARCH_STEPS = 48
VMEM_LIMIT = 48 * 1024 * 1024

_OFF_QA, _OFF_KA, _OFF_QB, _OFF_KB, _OFF_QI, _OFF_KI, _OFF_WI, _OFF_U, _OFF_END = (
    0, 384, 1152, 1536, 2304, 2816, 2880, 2888, 3144)
PROJ_COLS = 3328

F32 = jnp.float32
BF16 = jnp.bfloat16


def _const_spec(shape):
    zeros = (0,) * len(shape)
    return pl.BlockSpec(shape, lambda *_: zeros, pipeline_mode=pl.Buffered(1))


def _params(semantics):
    return pltpu.CompilerParams(dimension_semantics=semantics, vmem_limit_bytes=VMEM_LIMIT)


def _rms(x, g):
    ms = jnp.mean(x * x, axis=-1, keepdims=True)
    return x * lax.rsqrt(ms + RMS_EPS) * g


def _dot(a, b):
    return jnp.dot(a, b, preferred_element_type=F32)


def _dot_t(a, b):
    return lax.dot_general(a, b, (((1,), (1,)), ((), ())), preferred_element_type=F32)


def _softmax_step(s, vt, m, l, acc):
    m_new = jnp.maximum(m, jnp.max(s, axis=-1, keepdims=True))
    alpha = jnp.exp(m - m_new)
    p = jnp.exp(s - m_new)
    l = alpha * l + jnp.sum(p, axis=-1, keepdims=True)
    acc = alpha * acc + _dot(p.astype(BF16), vt)
    return m_new, l, acc


def _half_masks(rows):
    lane = lax.broadcasted_iota(jnp.int32, (rows, LANES), 1)
    first = lane < HEAD_DIM
    m0 = jnp.where(first, 1.0, 0.0).astype(BF16)
    m1 = jnp.where(first, 0.0, 1.0).astype(BF16)
    return first, m0, m1


def _proj_kernel(x_ref, g_ref, w_ref, qa_ref, ka_ref, va_ref, qb_ref, kb_ref, vb_ref,
                 qi_ref, ki2_ref, wi_ref, u_ref):
    h = _rms(x_ref[...], g_ref[...]).astype(BF16)
    z = _dot(h, w_ref[:, 0:768])
    qa_ref[...] = z[:, 0:384].astype(BF16)
    ka_ref[...] = z[:, 384:768]
    z = _dot(h, w_ref[:, 768:1536])
    va_ref[...] = z[:, 0:384]
    qb_ref[...] = z[:, 384:768].astype(BF16)
    z = _dot(h, w_ref[:, 1536:2304])
    kb_ref[...] = z[:, 0:384]
    vb_ref[...] = z[:, 384:768]
    qi_ref[...] = _dot(h, w_ref[:, 2304:2816]).astype(BF16)
    z = _dot(h, w_ref[:, 2816:3328])
    ki2_ref[...] = z[:, 0:128]
    wi_ref[...] = z[:, 128:256]
    u_ref[...] = z[:, 256:512]


def _proj(x, g, w):
    n = x.shape[0]
    assert n % ROW_TILE == 0
    widths = (ATT_WIDTH,) * 6 + (IDX_HEADS * IDX_DIM, LANES, LANES, C_WIDTH)
    dtypes = (BF16, F32, F32, BF16, F32, F32, BF16, F32, F32, F32)
    row = lambda i: (i, 0)
    return pl.pallas_call(
        _proj_kernel,
        out_shape=tuple(jax.ShapeDtypeStruct((n, wd), dt) for wd, dt in zip(widths, dtypes)),
        grid=(n // ROW_TILE,),
        in_specs=[pl.BlockSpec((ROW_TILE, D_MODEL), row),
                  _const_spec((1, D_MODEL)),
                  _const_spec((D_MODEL, PROJ_COLS))],
        out_specs=tuple(pl.BlockSpec((ROW_TILE, wd), row) for wd in widths),
        compiler_params=_params(("parallel",)),
        name="proj",
    )(x, g, w)


def _proj_weight(w_in):
    qa = w_in[:, _OFF_QA:_OFF_KA] * ATT_SCALE
    kava = w_in[:, _OFF_KA:_OFF_QB]
    qb = w_in[:, _OFF_QB:_OFF_KB] * ATT_SCALE
    kbvb = w_in[:, _OFF_KB:_OFF_QI]
    qi = w_in[:, _OFF_QI:_OFF_KI]
    ki = w_in[:, _OFF_KI:_OFF_WI]
    wi = w_in[:, _OFF_WI:_OFF_U]
    u = w_in[:, _OFF_U:_OFF_END]
    pad = jnp.zeros((D_MODEL, LANES - IDX_HEADS), w_in.dtype)
    w = jnp.concatenate([qa, kava, qb, kbvb, qi, ki, ki, wi, pad, u], axis=1)
    assert w.shape[1] == PROJ_COLS
    return w.astype(BF16)


def _band_kernel(q_ref, k_ref, v_ref, tab_ref, o_ref, *, tq, shift):
    j = pl.program_id(1)
    first, m0, m1 = _half_masks(tq)
    for pair in range(N_PAIRS):
        cols = slice(LANES * pair, LANES * (pair + 1))
        qp = q_ref[0, :, cols]
        outs = []
        for hh in range(2):
            qm = qp * (m0 if hh == 0 else m1)
            m = jnp.full((tq, 1), NEG, F32)
            l = jnp.zeros((tq, 1), F32)
            acc = jnp.zeros((tq, LANES), F32)
            for t in range(3):
                blk = j + (t - shift)
                start = pl.multiple_of(jnp.maximum(blk, 0) * KEY_TILE, KEY_TILE)
                kt = k_ref[0, pl.ds(start, KEY_TILE), cols].astype(BF16)
                vt = v_ref[0, pl.ds(start, KEY_TILE), cols].astype(BF16)
                s = _dot_t(qm, kt) + tab_ref[2 * pair + hh, t]
                s = jnp.where(blk >= 0, s, NEG)
                m, l, acc = _softmax_step(s, vt, m, l, acc)
            outs.append(acc / l)
        o_ref[0, :, cols] = jnp.where(first, outs[0], outs[1]).astype(BF16)


def _band(q, k, v, tab, *, tq, shift):
    bn, tq_total, _ = q.shape
    tk_total = k.shape[1]
    kern = functools.partial(_band_kernel, tq=tq, shift=shift)
    qmap = lambda b, j: (b, j, 0)
    kmap = lambda b, j: (b, 0, 0)
    return pl.pallas_call(
        kern,
        out_shape=jax.ShapeDtypeStruct((bn, tq_total, ATT_WIDTH), BF16),
        grid=(bn, tq_total // tq),
        in_specs=[pl.BlockSpec((1, tq, ATT_WIDTH), qmap),
                  pl.BlockSpec((1, tk_total, ATT_WIDTH), kmap),
                  pl.BlockSpec((1, tk_total, ATT_WIDTH), kmap),
                  _const_spec(tab.shape)],
        out_specs=pl.BlockSpec((1, tq, ATT_WIDTH), qmap),
        compiler_params=_params(("parallel", "arbitrary")),
        name="band",
    )(q, k, v, tab)


def _band_table(rel_bias, q_pos0, k_pos0, tq, n_real):
    tk = 3 * KEY_TILE
    span = tq + tk - 1
    diff = (q_pos0 - k_pos0) - (tk - 1) + jnp.arange(span)
    vec = jnp.take(rel_bias.astype(F32), jnp.clip(diff, -REL_CLIP, REL_CLIP) + REL_CLIP, axis=1)
    h = vec.shape[0]
    skew = jnp.tile(vec, (1, tq + 1))[:, :tq * (span + 1)].reshape(h, tq, span + 1)
    bias = jnp.flip(skew[:, :, :tk], axis=-1)
    qc = (q_pos0 + jnp.arange(tq)) // CHUNK
    kc = (k_pos0 + jnp.arange(tk)) // CHUNK
    valid = ((kc[None, :] <= qc[:, None]) & (kc[None, :] >= qc[:, None] - A_LEFT_CHUNKS)
             & (jnp.arange(tk) < n_real)[None, :])
    tab = jnp.where(valid[None], bias, NEG)
    return tab.reshape(h, tq, 3, KEY_TILE).transpose(0, 2, 1, 3)


def _sparse_kernel(qi_ref, wi_ref, ki2_ref, qb_ref, kb_ref, vb_ref, tri_ref, o_ref,
                   sc_ref, mb_ref, wb_ref, qim_ref, qbm_ref, m_ref, l_ref, acc_ref,
                   *, tq, q_pos0, n_keys):
    j = pl.program_id(1)
    q_first = q_pos0 + j * tq
    last_chunk = lax.shift_right_logical(q_first + (tq - 1), CHUNK_SHIFT)
    k_end = jnp.minimum((last_chunk + 1) * CHUNK, n_keys)
    n_tiles = lax.shift_right_logical(k_end + (KEY_TILE - 1), 8)

    first, m0, m1 = _half_masks(tq)
    q_pos = q_first + lax.broadcasted_iota(jnp.int32, (tq, 1), 0)
    k_lim = jnp.minimum((lax.shift_right_logical(q_pos, CHUNK_SHIFT) + 1) * CHUNK, n_keys)
    col = lax.broadcasted_iota(jnp.int32, (tq, KEY_TILE), 1)

    w = wi_ref[0]
    for h in range(IDX_HEADS):
        wb_ref[h] = jnp.broadcast_to(w[:, h:h + 1], (tq, KEY_TILE))
        pair = slice(LANES * (h // 2), LANES * (h // 2 + 1))
        qim_ref[h] = qi_ref[0, :, pair] * (m0 if h % 2 == 0 else m1)
    for h in range(N_HEADS):
        pair = slice(LANES * (h // 2), LANES * (h // 2 + 1))
        qbm_ref[h] = qb_ref[0, :, pair] * (m0 if h % 2 == 0 else m1)
        m_ref[h] = jnp.full((tq, LANES), NEG, F32)
        l_ref[h] = jnp.zeros((tq, LANES), F32)
        acc_ref[h] = jnp.zeros((tq, LANES), F32)

    def score_tile(t, carry):
        rmin, rmax = carry
        start = pl.multiple_of(t * KEY_TILE, KEY_TILE)
        kt = ki2_ref[0, pl.ds(start, KEY_TILE), :].astype(BF16)
        acc = jnp.zeros((tq, KEY_TILE), F32)
        for h in range(IDX_HEADS):
            acc = acc + jnp.maximum(_dot_t(qim_ref[h], kt), 0.0) * wb_ref[h]
        sc = acc * IDX_SCALE
        adm = (start + col) < k_lim
        sc_ref[t] = jnp.where(adm, sc, -jnp.inf)
        rmax = jnp.maximum(rmax, jnp.max(jnp.where(adm, sc, -jnp.inf), axis=-1, keepdims=True))
        rmin = jnp.minimum(rmin, jnp.min(jnp.where(adm, sc, jnp.inf), axis=-1, keepdims=True))
        return rmin, rmax

    rmin, rmax = lax.fori_loop(
        0, n_tiles, score_tile,
        (jnp.full((tq, 1), jnp.inf, F32), jnp.full((tq, 1), -jnp.inf, F32)))

    def count_ge(thr):
        def body(t, c):
            ind = jnp.where(sc_ref[t] >= thr, 1.0, 0.0)
            return c + (ind[:, :LANES] + ind[:, LANES:])
        c = lax.fori_loop(0, n_tiles, body, jnp.zeros((tq, LANES), F32))
        return jnp.sum(c, axis=-1, keepdims=True)

    def n_unsettled(c_lo):
        return jnp.sum(jnp.where(c_lo > TOPK, 1.0, 0.0))

    def search_cond(carry):
        step, n_open = carry[0], carry[1]
        return jnp.logical_and(step < SEARCH_STEPS, n_open > 0.0)

    def search_step(carry):
        step, _, lo, hi, c_lo, c_hi = carry
        interp = (c_lo - (TOPK - 0.5)) / (c_lo - c_hi)
        frac = jnp.where(lax.rem(step, 2) == 0, interp, 0.5)
        mid = lo + (hi - lo) * frac
        c_mid = count_ge(mid)
        unsettled = c_lo > TOPK
        enough = c_mid >= TOPK
        lo, c_lo = (jnp.where(unsettled, jnp.where(enough, mid, lo), lo),
                    jnp.where(unsettled, jnp.where(enough, c_mid, c_lo), c_lo))
        hi, c_hi = (jnp.where(unsettled, jnp.where(enough, hi, mid), hi),
                    jnp.where(unsettled, jnp.where(enough, c_hi, c_mid), c_hi))
        return step + 1, n_unsettled(c_lo), lo, hi, c_lo, c_hi

    hi0 = rmax + jnp.maximum(jnp.abs(rmax), 1e-30) * (2.0 ** -10)
    c_lo0 = k_lim.astype(F32)
    _, _, lo, hi, _, c_hi = lax.while_loop(
        search_cond, search_step,
        (jnp.int32(0), n_unsettled(c_lo0), rmin, hi0, c_lo0, jnp.zeros((tq, 1), F32)))

    need = TOPK - c_hi

    def mask_tile(t, run):
        x = sc_ref[t]
        inr = jnp.where(x >= lo, jnp.where(x < hi, 1.0, 0.0), 0.0)
        rank = _dot(inr.astype(BF16), tri_ref[...]) + run
        mb_ref[t] = jnp.where(
            x >= hi, 0.0, jnp.where(inr > 0.0, jnp.where(rank <= need, 0.0, NEG), NEG))
        return run + jnp.sum(inr, axis=-1, keepdims=True)

    lax.fori_loop(0, n_tiles, mask_tile, jnp.zeros((tq, 1), F32))

    def attend(t, carry):
        start = pl.multiple_of(t * KEY_TILE, KEY_TILE)
        mb = mb_ref[t]
        for pair in range(N_PAIRS):
            cols = slice(LANES * pair, LANES * (pair + 1))
            kt = kb_ref[0, pl.ds(start, KEY_TILE), cols].astype(BF16)
            vt = vb_ref[0, pl.ds(start, KEY_TILE), cols].astype(BF16)
            for h in (2 * pair, 2 * pair + 1):
                s = _dot_t(qbm_ref[h], kt) + mb
                m_old = m_ref[h]
                m_new = jnp.maximum(m_old, jnp.max(s, axis=-1, keepdims=True))
                alpha = jnp.exp(m_old - m_new)
                p = jnp.exp(s - jnp.concatenate([m_new, m_new], axis=1))
                l_ref[h] = alpha * l_ref[h] + jnp.sum(p, axis=-1, keepdims=True)
                acc_ref[h] = alpha * acc_ref[h] + _dot(p.astype(BF16), vt)
                m_ref[h] = m_new
        return carry

    lax.fori_loop(0, n_tiles, attend, 0)
    for pair in range(N_PAIRS):
        h0, h1 = 2 * pair, 2 * pair + 1
        out = jnp.where(first, acc_ref[h0] / l_ref[h0], acc_ref[h1] / l_ref[h1])
        o_ref[0, :, LANES * pair:LANES * (pair + 1)] = out.astype(BF16)


def _sparse(qi, wi, ki2, qb, kb, vb, tri, *, tq, q_pos0, n_keys):
    bn, tq_total, _ = qb.shape
    tk_total = kb.shape[1]
    assert tk_total % KEY_TILE == 0 and n_keys <= tk_total
    max_tiles = tk_total // KEY_TILE
    kern = functools.partial(_sparse_kernel, tq=tq, q_pos0=q_pos0, n_keys=n_keys)
    qmap = lambda b, j: (b, j, 0)
    kmap = lambda b, j: (b, 0, 0)
    return pl.pallas_call(
        kern,
        out_shape=jax.ShapeDtypeStruct((bn, tq_total, ATT_WIDTH), BF16),
        grid=(bn, tq_total // tq),
        in_specs=[pl.BlockSpec((1, tq, IDX_HEADS * IDX_DIM), qmap),
                  pl.BlockSpec((1, tq, LANES), qmap),
                  pl.BlockSpec((1, tk_total, LANES), kmap),
                  pl.BlockSpec((1, tq, ATT_WIDTH), qmap),
                  pl.BlockSpec((1, tk_total, ATT_WIDTH), kmap),
                  pl.BlockSpec((1, tk_total, ATT_WIDTH), kmap),
                  _const_spec((KEY_TILE, KEY_TILE))],
        out_specs=pl.BlockSpec((1, tq, ATT_WIDTH), qmap),
        scratch_shapes=[pltpu.VMEM((max_tiles, tq, KEY_TILE), F32),
                        pltpu.VMEM((max_tiles, tq, KEY_TILE), F32),
                        pltpu.VMEM((IDX_HEADS, tq, KEY_TILE), F32),
                        pltpu.VMEM((IDX_HEADS, tq, LANES), BF16),
                        pltpu.VMEM((N_HEADS, tq, LANES), BF16),
                        pltpu.VMEM((N_HEADS, tq, LANES), F32),
                        pltpu.VMEM((N_HEADS, tq, LANES), F32),
                        pltpu.VMEM((N_HEADS, tq, LANES), F32)],
        compiler_params=_params(("parallel", "arbitrary")),
        name="sparse",
    )(qi, wi, ki2, qb, kb, vb, tri)


def _pool_kernel(u_ref, prev_ref, w_ref, scale_ref, o_ref, f_ref, a_ref, b_ref, *, t, pos0):
    n = t + POOL_PAD
    u = u_ref[0]
    f_ref[0:16, :] = jnp.zeros((16, C_WIDTH), F32)
    f_ref[16:32, :] = prev_ref[0]
    f_ref[pl.ds(POOL_PAD, t), :] = u
    s2 = f_ref[pl.ds(8, n - 8), :] + f_ref[pl.ds(7, n - 8), :]
    a_ref[pl.ds(8, n - 8), :] = s2
    s4 = a_ref[pl.ds(16, n - 16), :] + a_ref[pl.ds(14, n - 16), :]
    b_ref[pl.ds(16, n - 16), :] = s4
    s8 = b_ref[pl.ds(24, n - 24), :] + b_ref[pl.ds(20, n - 24), :]
    a_ref[pl.ds(24, n - 24), :] = s8
    s16 = a_ref[pl.ds(32, t), :] + a_ref[pl.ds(24, t), :]
    lane = lax.broadcasted_iota(jnp.int32, (t, C_WIDTH), 1)
    g0, g1, g2 = lane < POOL_GROUP, lane < 2 * POOL_GROUP, lane < 3 * POOL_GROUP
    total = jnp.where(g0, s2[24:], jnp.where(g1, s4[16:], jnp.where(g2, s8[8:], s16)))
    win = jnp.where(g0, POOL_WINDOWS[0],
                    jnp.where(g1, POOL_WINDOWS[1],
                              jnp.where(g2, POOL_WINDOWS[2], POOL_WINDOWS[3])))
    pos = pos0 + lax.broadcasted_iota(jnp.int32, (t, C_WIDTH), 0)
    cnt = jnp.minimum(pos + 1, win).astype(F32)
    pooled = total / cnt - u
    o_ref[0] = (_dot(pooled.astype(BF16), w_ref[...]) * scale_ref[...]).astype(BF16)


def _pool(u, prev16, w_blk, scale, *, pos0):
    bn, t, _ = u.shape
    kern = functools.partial(_pool_kernel, t=t, pos0=pos0)
    bmap = lambda b: (b, 0, 0)
    n = t + POOL_PAD
    return pl.pallas_call(
        kern,
        out_shape=jax.ShapeDtypeStruct((bn, t, C_WIDTH), BF16),
        grid=(bn,),
        in_specs=[pl.BlockSpec((1, t, C_WIDTH), bmap),
                  pl.BlockSpec((1, POOL_MAX, C_WIDTH), bmap),
                  _const_spec((C_WIDTH, C_WIDTH)),
                  _const_spec((1, C_WIDTH))],
        out_specs=pl.BlockSpec((1, t, C_WIDTH), bmap),
        scratch_shapes=[pltpu.VMEM((n, C_WIDTH), F32)] * 3,
        compiler_params=_params(("parallel",)),
        name="pool",
    )(u, prev16, w_blk, scale)


def _pool_weight(w_pool):
    w = jnp.zeros((C_WIDTH, C_WIDTH), w_pool.dtype)
    for g in range(len(POOL_WINDOWS)):
        sl = slice(g * POOL_GROUP, (g + 1) * POOL_GROUP)
        w = w.at[sl, sl].set(w_pool[g])
    return w.astype(BF16)


def _outmlp_kernel(ya_ref, yb_ref, yc_ref, x_ref, woa_ref, wob_ref, woc_ref,
                   g1_ref, g2_ref, g3_ref, w1_ref, w2_ref, o_ref):
    y = _dot(ya_ref[...], woa_ref[...]) + _dot(yb_ref[...], wob_ref[...])
    y = y + _dot(yc_ref[...], woc_ref[...])
    x1 = x_ref[...] + _rms(y, g1_ref[...])
    h = _rms(x1, g2_ref[...]).astype(BF16)
    m = jnp.zeros_like(x1)
    for f in range(D_FF // D_MODEL):
        sl = slice(f * D_MODEL, (f + 1) * D_MODEL)
        a = jnp.maximum(_dot(h, w1_ref[:, sl]), 0.0)
        m = m + _dot((a * a).astype(BF16), w2_ref[sl, :])
    o_ref[...] = x1 + _rms(m, g3_ref[...])


def _outmlp(ya, yb, yc, x, woa, wob, woc, g1, g2, g3, w1, w2):
    n = x.shape[0]
    assert n % ROW_TILE == 0
    row = lambda i: (i, 0)
    return pl.pallas_call(
        _outmlp_kernel,
        out_shape=jax.ShapeDtypeStruct((n, D_MODEL), F32),
        grid=(n // ROW_TILE,),
        in_specs=[pl.BlockSpec((ROW_TILE, ATT_WIDTH), row),
                  pl.BlockSpec((ROW_TILE, ATT_WIDTH), row),
                  pl.BlockSpec((ROW_TILE, C_WIDTH), row),
                  pl.BlockSpec((ROW_TILE, D_MODEL), row),
                  _const_spec((ATT_WIDTH, D_MODEL)),
                  _const_spec((ATT_WIDTH, D_MODEL)),
                  _const_spec((C_WIDTH, D_MODEL)),
                  _const_spec((1, D_MODEL)),
                  _const_spec((1, D_MODEL)),
                  _const_spec((1, D_MODEL)),
                  _const_spec((D_MODEL, D_FF)),
                  _const_spec((D_FF, D_MODEL))],
        out_specs=pl.BlockSpec((ROW_TILE, D_MODEL), row),
        compiler_params=_params(("parallel",)),
        name="outmlp",
    )(ya, yb, yc, x, woa, wob, woc, g1, g2, g3, w1, w2)


def _pad_rows(x, rows):
    return jnp.pad(x, ((0, 0), (0, rows - x.shape[1]), (0, 0)))


def _layer(x, weights, tri, *, seq, band_tab, band_tq, band_shift, sparse_tq, q_pos0,
           past=None):
    (g_pre_mix, w_proj, w_pool_blk, pool_scale, woa, wob, woc, g_post_mix,
     g_pre_mlp, w1, w2, g_post_mlp) = weights
    bn = x.shape[0] // seq
    qa, ka, va, qb, kb, vb, qi, ki2, wi, u = _proj(x, g_pre_mix, w_proj)
    per_batch = lambda t: t.reshape(bn, seq, t.shape[-1])
    qa, ka, va, qb, kb, vb, qi, ki2, wi, u = map(per_batch, (qa, ka, va, qb, kb, vb, qi, ki2, wi, u))

    if past is None:
        ka_all, va_all, kb_all, vb_all, ki2_all = ka, va, kb, vb, ki2
        prev16 = jnp.zeros((bn, POOL_MAX, C_WIDTH), F32)
        n_keys = seq
    else:
        c_a_k, c_a_v, c_b_k, c_b_v, c_b_kidx, c_pool = past
        flat = lambda t: t.reshape(bn, t.shape[1], ATT_WIDTH)
        band_rows = 3 * KEY_TILE
        ka_all = _pad_rows(jnp.concatenate([flat(c_a_k), ka], axis=1), band_rows)
        va_all = _pad_rows(jnp.concatenate([flat(c_a_v), va], axis=1), band_rows)
        n_keys = c_b_k.shape[1] + seq
        key_rows = -(-n_keys // KEY_TILE) * KEY_TILE
        kb_all = _pad_rows(jnp.concatenate([flat(c_b_k), kb], axis=1), key_rows)
        vb_all = _pad_rows(jnp.concatenate([flat(c_b_v), vb], axis=1), key_rows)
        c_ki2 = jnp.concatenate([c_b_kidx, c_b_kidx], axis=-1)
        ki2_all = _pad_rows(jnp.concatenate([c_ki2, ki2], axis=1), key_rows)
        prev16 = jnp.pad(c_pool, ((0, 0), (1, 0), (0, 0)))

    ya = _band(qa, ka_all, va_all, band_tab, tq=band_tq, shift=band_shift)
    yb = _sparse(qi, wi, ki2_all, qb, kb_all, vb_all, tri, tq=sparse_tq, q_pos0=q_pos0,
                 n_keys=n_keys)
    yc = _pool(u, prev16, w_pool_blk, pool_scale, pos0=q_pos0)
    flat2 = lambda t: t.reshape(bn * seq, t.shape[-1])
    x = _outmlp(flat2(ya), flat2(yb), flat2(yc), x, woa, wob, woc,
                g_post_mix, g_pre_mlp, g_post_mlp, w1, w2)
    heads = lambda t: t.reshape(bn, t.shape[1], N_HEADS, HEAD_DIM)
    return x, (heads(ka), heads(va), heads(kb), heads(vb), ki2[..., :IDX_DIM], u)


def kernel(x_prompt, x_sample, cache_a_k, cache_a_v, cache_b_k, cache_b_v, cache_b_kidx, state_pool, g_pre_mix, w_in, rel_bias, w_pool, pool_scale, w_out, g_post_mix, g_pre_mlp, w_ff1, w_ff2, g_post_mlp):
    batch, seq, _ = x_prompt.shape
    dec_batch, dec_seq, _ = x_sample.shape
    depth = w_in.shape[0]
    past_len = cache_b_k.shape[2]
    n_a = cache_a_k.shape[2]
    assert seq % KEY_TILE == 0 and (batch * seq) % ROW_TILE == 0
    assert (dec_batch * dec_seq) % ROW_TILE == 0 and n_a + dec_seq <= 3 * KEY_TILE

    tri = (jnp.arange(KEY_TILE)[:, None] <= jnp.arange(KEY_TILE)[None, :]).astype(BF16)

    xp = x_prompt.reshape(batch * seq, D_MODEL)
    xs = x_sample.reshape(dec_batch * dec_seq, D_MODEL)
    p_states, s_states = [], []
    row = lambda t: t.reshape(1, -1)
    for l in range(depth):
        weights = (row(g_pre_mix[l]), _proj_weight(w_in[l]), _pool_weight(w_pool[l]),
                   row(pool_scale[l]),
                   w_out[l, :ATT_WIDTH].astype(BF16),
                   w_out[l, ATT_WIDTH:2 * ATT_WIDTH].astype(BF16),
                   w_out[l, 2 * ATT_WIDTH:].astype(BF16),
                   row(g_post_mix[l]), row(g_pre_mlp[l]),
                   w_ff1[l].astype(BF16), w_ff2[l].astype(BF16), row(g_post_mlp[l]))
        xp, sp = _layer(xp, weights, tri, seq=seq,
                        band_tab=_band_table(rel_bias[l], 2 * KEY_TILE, 0, KEY_TILE, 3 * KEY_TILE),
                        band_tq=KEY_TILE, band_shift=2, sparse_tq=KEY_TILE, q_pos0=0)
        xs, ss = _layer(xs, weights, tri, seq=dec_seq,
                        band_tab=_band_table(rel_bias[l], past_len, past_len - n_a, dec_seq,
                                             n_a + dec_seq),
                        band_tq=dec_seq, band_shift=0, sparse_tq=dec_seq, q_pos0=past_len,
                        past=(cache_a_k[l], cache_a_v[l], cache_b_k[l], cache_b_v[l],
                              cache_b_kidx[l], state_pool[l]))
        n_keep = min(A_WINDOW, seq)
        ka, va, kb, vb, ki, u = sp
        p_states.append((ka[:, seq - n_keep:], va[:, seq - n_keep:], kb, vb, ki,
                         u[:, seq - (POOL_MAX - 1):]))
        ka, va, kb, vb, ki, u = ss
        new_pool = jnp.concatenate([state_pool[l], u], axis=1)[:, dec_seq:]
        s_states.append((ka, va, kb, vb, ki, new_pool))
    stk = lambda states, i: jnp.stack([st[i] for st in states], axis=0)
    return ((xp.reshape(batch, seq, D_MODEL), xs.reshape(dec_batch, dec_seq, D_MODEL))
            + tuple(stk(p_states, i) for i in range(6))
            + tuple(stk(s_states, i) for i in range(6)))
```

```python
import functools

import jax
import jax.numpy as jnp
from jax import lax
from jax.experimental import pallas as pl
from jax.experimental.pallas import tpu as pltpu

D_MODEL = 1024
CHUNK = 64
CHUNK_SHIFT = 6
HEAD_DIM = 64
N_HEADS = 6
ATT_WIDTH = N_HEADS * HEAD_DIM
N_PAIRS = N_HEADS // 2
C_WIDTH = 256
A_LEFT_CHUNKS = 8
A_WINDOW = A_LEFT_CHUNKS * CHUNK
REL_CLIP = 128
IDX_HEADS = 8
IDX_DIM = 64
TOPK = 256
POOL_WINDOWS = (2, 4, 8, 16)
POOL_GROUP = 64
POOL_MAX = 16
POOL_PAD = 32
D_FF = 4 * D_MODEL
RMS_EPS = 1e-6
IDX_SCALE = IDX_HEADS ** -0.5 * IDX_DIM ** -0.5
ATT_SCALE = HEAD_DIM ** -0.5

LANES = 128
SUBLANES = 8
KEY_TILE = 256
KEY_TILE_SHIFT = 8
ROW_TILE = 512
NEG = -1e30
FIRST_STEPS = 24
SEARCH_STEPS = 40
VMEM_LIMIT = 48 * 1024 * 1024

_OFF_QA, _OFF_KA, _OFF_QB, _OFF_KB, _OFF_QI, _OFF_KI, _OFF_WI, _OFF_U, _OFF_END = (
    0, 384, 1152, 1536, 2304, 2816, 2880, 2888, 3144)
PROJ_COLS = 3328

F32 = jnp.float32
BF16 = jnp.bfloat16


def _const_spec(shape):
    zeros = (0,) * len(shape)
    return pl.BlockSpec(shape, lambda *_: zeros, pipeline_mode=pl.Buffered(1))


def _params(semantics):
    return pltpu.CompilerParams(dimension_semantics=semantics, vmem_limit_bytes=VMEM_LIMIT)


def _rms(x, g):
    ms = jnp.mean(x * x, axis=-1, keepdims=True)
    return x * lax.rsqrt(ms + RMS_EPS) * g


def _dot(a, b):
    return jnp.dot(a, b, preferred_element_type=F32)


def _dot_t(a, b):
    return lax.dot_general(a, b, (((1,), (1,)), ((), ())), preferred_element_type=F32)


def _softmax_step(s, vt, m, l, acc):
    m_new = jnp.maximum(m, jnp.max(s, axis=-1, keepdims=True))
    alpha = jnp.exp(m - m_new)
    p = jnp.exp(s - m_new)
    l = alpha * l + jnp.sum(p, axis=-1, keepdims=True)
    acc = alpha * acc + _dot(p.astype(BF16), vt)
    return m_new, l, acc


def _half_masks(rows):
    lane = lax.broadcasted_iota(jnp.int32, (rows, LANES), 1)
    first = lane < HEAD_DIM
    m0 = jnp.where(first, 1.0, 0.0).astype(BF16)
    m1 = jnp.where(first, 0.0, 1.0).astype(BF16)
    return first, m0, m1


def _fold(x, op):
    return op(x.reshape(x.shape[0] // SUBLANES, SUBLANES, x.shape[1]), axis=0)


def _proj_kernel(x_ref, g_ref, w_ref, wt_ref, qa_ref, ka_ref, va_ref, qb_ref, kb_ref, vb_ref,
                 qi_ref, ki2_ref, u_ref, wit_ref, vbt_ref):
    h = _rms(x_ref[...], g_ref[...]).astype(BF16)
    z = _dot(h, w_ref[:, 0:768])
    qa_ref[...] = z[:, 0:384].astype(BF16)
    ka_ref[...] = z[:, 384:768]
    z = _dot(h, w_ref[:, 768:1536])
    va_ref[...] = z[:, 0:384]
    qb_ref[...] = z[:, 384:768].astype(BF16)
    z = _dot(h, w_ref[:, 1536:2304])
    kb_ref[...] = z[:, 0:384]
    vb_ref[...] = z[:, 384:768]
    qi_ref[...] = _dot(h, w_ref[:, 2304:2816]).astype(BF16)
    z = _dot(h, w_ref[:, 2816:3328])
    ki2_ref[...] = z[:, 0:128]
    u_ref[...] = z[:, 256:512]
    zt = _dot_t(wt_ref[...], h)
    wit_ref[...] = zt[0:LANES]
    vbt_ref[...] = zt[LANES:].astype(BF16)


def _proj(x, g, w, wt):
    n = x.shape[0]
    assert n % ROW_TILE == 0
    widths = (ATT_WIDTH,) * 6 + (IDX_HEADS * IDX_DIM, LANES, C_WIDTH)
    dtypes = (BF16, F32, F32, BF16, F32, F32, BF16, F32, F32)
    row = lambda i: (i, 0)
    col = lambda i: (0, i)
    out_shape = tuple(jax.ShapeDtypeStruct((n, wd), dt) for wd, dt in zip(widths, dtypes))
    out_shape += (jax.ShapeDtypeStruct((LANES, n), F32), jax.ShapeDtypeStruct((ATT_WIDTH, n), BF16))
    out_specs = tuple(pl.BlockSpec((ROW_TILE, wd), row) for wd in widths)
    out_specs += (pl.BlockSpec((LANES, ROW_TILE), col), pl.BlockSpec((ATT_WIDTH, ROW_TILE), col))
    return pl.pallas_call(
        _proj_kernel,
        out_shape=out_shape,
        grid=(n // ROW_TILE,),
        in_specs=[pl.BlockSpec((ROW_TILE, D_MODEL), row),
                  _const_spec((1, D_MODEL)),
                  _const_spec((D_MODEL, PROJ_COLS)),
                  _const_spec((LANES + ATT_WIDTH, D_MODEL))],
        out_specs=out_specs,
        compiler_params=_params(("parallel",)),
        name="proj",
    )(x, g, w, wt)


def _proj_weight(w_in):
    qa = w_in[:, _OFF_QA:_OFF_KA] * ATT_SCALE
    kava = w_in[:, _OFF_KA:_OFF_QB]
    qb = w_in[:, _OFF_QB:_OFF_KB] * ATT_SCALE
    kbvb = w_in[:, _OFF_KB:_OFF_QI]
    qi = w_in[:, _OFF_QI:_OFF_KI]
    ki = w_in[:, _OFF_KI:_OFF_WI]
    wi = w_in[:, _OFF_WI:_OFF_U]
    u = w_in[:, _OFF_U:_OFF_END]
    pad = jnp.zeros((D_MODEL, LANES - IDX_HEADS), w_in.dtype)
    w = jnp.concatenate([qa, kava, qb, kbvb, qi, ki, ki, wi, pad, u], axis=1)
    assert w.shape[1] == PROJ_COLS
    vb = w_in[:, _OFF_KB + ATT_WIDTH:_OFF_QI]
    wt = jnp.concatenate([wi, pad, vb], axis=1).T
    return w.astype(BF16), wt.astype(BF16)


def _band_kernel(q_ref, k_ref, v_ref, tab_ref, o_ref, *, tq, shift):
    j = pl.program_id(1)
    first, m0, m1 = _half_masks(tq)
    for pair in range(N_PAIRS):
        cols = slice(LANES * pair, LANES * (pair + 1))
        qp = q_ref[0, :, cols]
        outs = []
        for hh in range(2):
            qm = qp * (m0 if hh == 0 else m1)
            m = jnp.full((tq, 1), NEG, F32)
            l = jnp.zeros((tq, 1), F32)
            acc = jnp.zeros((tq, LANES), F32)
            for t in range(3):
                blk = j + (t - shift)
                start = pl.multiple_of(jnp.maximum(blk, 0) * KEY_TILE, KEY_TILE)
                kt = k_ref[0, pl.ds(start, KEY_TILE), cols].astype(BF16)
                vt = v_ref[0, pl.ds(start, KEY_TILE), cols].astype(BF16)
                s = _dot_t(qm, kt) + tab_ref[2 * pair + hh, t]
                s = jnp.where(blk >= 0, s, NEG)
                m, l, acc = _softmax_step(s, vt, m, l, acc)
            outs.append(acc / l)
        o_ref[0, :, cols] = jnp.where(first, outs[0], outs[1]).astype(BF16)


def _band(q, k, v, tab, *, tq, shift):
    bn, tq_total, _ = q.shape
    tk_total = k.shape[1]
    kern = functools.partial(_band_kernel, tq=tq, shift=shift)
    qmap = lambda b, j: (b, j, 0)
    kmap = lambda b, j: (b, 0, 0)
    return pl.pallas_call(
        kern,
        out_shape=jax.ShapeDtypeStruct((bn, tq_total, ATT_WIDTH), BF16),
        grid=(bn, tq_total // tq),
        in_specs=[pl.BlockSpec((1, tq, ATT_WIDTH), qmap),
                  pl.BlockSpec((1, tk_total, ATT_WIDTH), kmap),
                  pl.BlockSpec((1, tk_total, ATT_WIDTH), kmap),
                  _const_spec(tab.shape)],
        out_specs=pl.BlockSpec((1, tq, ATT_WIDTH), qmap),
        compiler_params=_params(("parallel", "arbitrary")),
        name="band",
    )(q, k, v, tab)


def _band_table(rel_bias, q_pos0, k_pos0, tq, n_real):
    tk = 3 * KEY_TILE
    span = tq + tk - 1
    diff = (q_pos0 - k_pos0) - (tk - 1) + jnp.arange(span)
    vec = jnp.take(rel_bias.astype(F32), jnp.clip(diff, -REL_CLIP, REL_CLIP) + REL_CLIP, axis=1)
    h = vec.shape[0]
    skew = jnp.tile(vec, (1, tq + 1))[:, :tq * (span + 1)].reshape(h, tq, span + 1)
    bias = jnp.flip(skew[:, :, :tk], axis=-1)
    qc = (q_pos0 + jnp.arange(tq)) // CHUNK
    kc = (k_pos0 + jnp.arange(tk)) // CHUNK
    valid = ((kc[None, :] <= qc[:, None]) & (kc[None, :] >= qc[:, None] - A_LEFT_CHUNKS)
             & (jnp.arange(tk) < n_real)[None, :])
    tab = jnp.where(valid[None], bias, NEG)
    return tab.reshape(h, tq, 3, KEY_TILE).transpose(0, 2, 1, 3)


def _sparse_kernel(qi_ref, wit_ref, ki2_ref, qb_ref, kb_ref, vt_ref, tri_ref, o_ref,
                   sc_ref, mb_ref, qim_ref, qbm_ref, acc_ref, out_ref, s_ref, p_ref,
                   *, tq, n_real, q_pos0, n_keys):
    j = pl.program_id(1)
    q_first = q_pos0 + j * n_real
    last_chunk = lax.shift_right_logical(q_first + (n_real - 1), CHUNK_SHIFT)
    k_end = jnp.minimum((last_chunk + 1) * CHUNK, n_keys)
    n_tiles = lax.shift_right_logical(k_end + (KEY_TILE - 1), KEY_TILE_SHIFT)

    lane = lax.broadcasted_iota(jnp.int32, (1, tq), 1)
    q_pos = q_first + lane
    k_lim = jnp.minimum((lax.shift_right_logical(q_pos, CHUNK_SHIFT) + 1) * CHUNK, n_keys)
    key_row = lax.broadcasted_iota(jnp.int32, (KEY_TILE, tq), 0)

    _, m0, m1 = _half_masks(tq)
    for h in range(IDX_HEADS):
        pair = slice(LANES * (h // 2), LANES * (h // 2 + 1))
        qim_ref[h] = qi_ref[0, :, pair] * (m0 if h % 2 == 0 else m1)
    for h in range(N_HEADS):
        pair = slice(LANES * (h // 2), LANES * (h // 2 + 1))
        qbm_ref[h] = qb_ref[0, :, pair] * (m0 if h % 2 == 0 else m1)
        acc_ref[h] = jnp.zeros((HEAD_DIM, tq), F32)
    w = wit_ref[...]

    def score_tile(t, carry):
        rmin8, rmax8 = carry
        start = pl.multiple_of(t * KEY_TILE, KEY_TILE)
        kt = ki2_ref[0, pl.ds(start, KEY_TILE), :].astype(BF16)
        acc = jnp.zeros((KEY_TILE, tq), F32)
        for h in range(IDX_HEADS):
            acc = acc + jnp.maximum(_dot_t(kt, qim_ref[h]), 0.0) * w[h:h + 1, :]
        sc = acc * IDX_SCALE
        adm = (start + key_row) < k_lim
        lowest = jnp.where(adm, sc, -jnp.inf)
        sc_ref[t] = lowest
        rmax8 = jnp.maximum(rmax8, _fold(lowest, jnp.max))
        rmin8 = jnp.minimum(rmin8, _fold(jnp.where(adm, sc, jnp.inf), jnp.min))
        return rmin8, rmax8

    rmin8, rmax8 = lax.fori_loop(
        0, n_tiles, score_tile,
        (jnp.full((SUBLANES, tq), jnp.inf, F32), jnp.full((SUBLANES, tq), -jnp.inf, F32)))
    rmin = jnp.min(rmin8, axis=0, keepdims=True)
    rmax = jnp.max(rmax8, axis=0, keepdims=True)

    def count_ge(thr):
        def body(t, c8):
            return c8 + _fold(jnp.where(sc_ref[t] >= thr, 1.0, 0.0), jnp.sum)
        c8 = lax.fori_loop(0, n_tiles, body, jnp.zeros((SUBLANES, tq), F32))
        return jnp.sum(c8, axis=0, keepdims=True)

    def n_unsettled(c_lo):
        return jnp.sum(jnp.where(c_lo > TOPK, 1.0, 0.0))

    def search(carry, last_step):
        def cond(c):
            return jnp.logical_and(c[0] < last_step, c[1] > 0.0)

        def step(c):
            s, _, lo, hi, c_lo, c_hi = c
            mid = lo + (hi - lo) * 0.5
            c_mid = count_ge(mid)
            unsettled = c_lo > TOPK
            enough = c_mid >= TOPK
            lo, c_lo = (jnp.where(unsettled, jnp.where(enough, mid, lo), lo),
                        jnp.where(unsettled, jnp.where(enough, c_mid, c_lo), c_lo))
            hi, c_hi = (jnp.where(unsettled, jnp.where(enough, hi, mid), hi),
                        jnp.where(unsettled, jnp.where(enough, c_hi, c_mid), c_hi))
            return s + 1, n_unsettled(c_lo), lo, hi, c_lo, c_hi

        return lax.while_loop(cond, step, carry)

    hi0 = rmax + jnp.maximum(jnp.abs(rmax), 1e-30) * (2.0 ** -10)
    c_lo0 = jnp.where(lane < n_real, k_lim, 0).astype(F32)
    steps, n_open, lo, hi, c_lo, c_hi = search(
        (jnp.int32(0), n_unsettled(c_lo0), rmin, hi0, c_lo0, jnp.zeros((1, tq), F32)),
        FIRST_STEPS)

    def n_untied():
        def body(t, carry):
            vmin8, vmax8 = carry
            x = sc_ref[t]
            vmax8 = jnp.maximum(vmax8, _fold(
                jnp.where(x >= lo, jnp.where(x < hi, x, -jnp.inf), -jnp.inf), jnp.max))
            vmin8 = jnp.minimum(vmin8, _fold(
                jnp.where(x >= lo, jnp.where(x < hi, x, jnp.inf), jnp.inf), jnp.min))
            return vmin8, vmax8
        vmin8, vmax8 = lax.fori_loop(
            0, n_tiles, body,
            (jnp.full((SUBLANES, tq), jnp.inf, F32), jnp.full((SUBLANES, tq), -jnp.inf, F32)))
        vmin = jnp.min(vmin8, axis=0, keepdims=True)
        vmax = jnp.max(vmax8, axis=0, keepdims=True)
        return jnp.sum(jnp.where(c_lo > TOPK, jnp.where(vmax == vmin, 0.0, 1.0), 0.0))

    n_open = lax.cond(n_open > 0.0, n_untied, lambda: jnp.float32(0.0))
    _, _, lo, hi, _, c_hi = search((steps, n_open, lo, hi, c_lo, c_hi), SEARCH_STEPS)

    need = TOPK - c_hi

    def mask_tile(t, run):
        x = sc_ref[t]
        inr = jnp.where(x >= lo, jnp.where(x < hi, 1.0, 0.0), 0.0)
        rank = _dot(tri_ref[...], inr.astype(BF16)) + run
        mb_ref[t] = jnp.where(
            x >= hi, 0.0, jnp.where(inr > 0.0, jnp.where(rank <= need, 0.0, NEG), NEG))
        return rank[KEY_TILE - 1:KEY_TILE, :]

    lax.fori_loop(0, n_tiles, mask_tile, jnp.zeros((1, tq), F32))

    def attend(t, carry):
        start = pl.multiple_of(t * KEY_TILE, KEY_TILE)
        m_new = []
        for pair in range(N_PAIRS):
            kt = kb_ref[0, pl.ds(start, KEY_TILE), LANES * pair:LANES * (pair + 1)].astype(BF16)
            for h in (2 * pair, 2 * pair + 1):
                s = _dot_t(kt, qbm_ref[h]) + mb_ref[t]
                s_ref[h] = s
                col_max = jnp.max(_fold(s, jnp.max), axis=0, keepdims=True)
                m_new.append(jnp.maximum(carry[h][0], col_max))
        new = []
        for h in range(N_HEADS):
            m_old, l_old = carry[h]
            alpha = jnp.exp(m_old - m_new[h])
            p = jnp.exp(s_ref[h] - m_new[h])
            p_ref[h] = p.astype(BF16)
            l_new = alpha * l_old + jnp.sum(_fold(p, jnp.sum), axis=0, keepdims=True)
            acc_ref[h] = alpha * acc_ref[h]
            new.append((m_new[h], l_new))
        for h in range(N_HEADS):
            vt = vt_ref[0, t, HEAD_DIM * h:HEAD_DIM * (h + 1), :]
            acc_ref[h] = acc_ref[h] + _dot(vt, p_ref[h])
        return tuple(new)

    stats = lax.fori_loop(
        0, n_tiles, attend,
        tuple((jnp.full((1, tq), NEG, F32), jnp.zeros((1, tq), F32)) for _ in range(N_HEADS)))
    for h in range(N_HEADS):
        out_ref[HEAD_DIM * h:HEAD_DIM * (h + 1), :] = acc_ref[h] / stats[h][1]
    o_ref[0] = out_ref[...].T.astype(BF16)


def _sparse(qi, wit, ki2, qb, kb, vt, tri, *, tq, n_real, q_pos0, n_keys):
    bn, tq_total, _ = qb.shape
    tk_total = kb.shape[1]
    assert tk_total % KEY_TILE == 0 and n_keys <= tk_total and tq % LANES == 0
    max_tiles = tk_total // KEY_TILE
    blocks = tq_total // tq
    kern = functools.partial(_sparse_kernel, tq=tq, n_real=n_real, q_pos0=q_pos0, n_keys=n_keys)
    qmap = lambda b, j: (b, j, 0)
    kmap = lambda b, j: (b, 0, 0)
    return pl.pallas_call(
        kern,
        out_shape=jax.ShapeDtypeStruct((bn, tq_total, ATT_WIDTH), BF16),
        grid=(bn, blocks),
        in_specs=[pl.BlockSpec((1, tq, IDX_HEADS * IDX_DIM), qmap),
                  pl.BlockSpec((LANES, tq), lambda b, j: (0, b * blocks + j)),
                  pl.BlockSpec((1, tk_total, LANES), kmap),
                  pl.BlockSpec((1, tq, ATT_WIDTH), qmap),
                  pl.BlockSpec((1, tk_total, ATT_WIDTH), kmap),
                  pl.BlockSpec((1, max_tiles, ATT_WIDTH, KEY_TILE), lambda b, j: (b, 0, 0, 0)),
                  _const_spec((KEY_TILE, KEY_TILE))],
        out_specs=pl.BlockSpec((1, tq, ATT_WIDTH), qmap),
        scratch_shapes=[pltpu.VMEM((max_tiles, KEY_TILE, tq), F32),
                        pltpu.VMEM((max_tiles, KEY_TILE, tq), F32),
                        pltpu.VMEM((IDX_HEADS, tq, LANES), BF16),
                        pltpu.VMEM((N_HEADS, tq, LANES), BF16),
                        pltpu.VMEM((N_HEADS, HEAD_DIM, tq), F32),
                        pltpu.VMEM((ATT_WIDTH, tq), F32),
                        pltpu.VMEM((N_HEADS, KEY_TILE, tq), F32),
                        pltpu.VMEM((N_HEADS, KEY_TILE, tq), BF16)],
        compiler_params=_params(("parallel", "arbitrary")),
        name="sparse",
    )(qi, wit, ki2, qb, kb, vt, tri)


def _value_tiles(vt_all):
    bn, width, keys = vt_all.shape
    return vt_all.reshape(bn, width, keys // KEY_TILE, KEY_TILE).transpose(0, 2, 1, 3)


def _pool_kernel(u_ref, prev_ref, w_ref, scale_ref, o_ref, f_ref, a_ref, b_ref, *, t, pos0):
    n = t + POOL_PAD
    u = u_ref[0]
    f_ref[0:16, :] = jnp.zeros((16, C_WIDTH), F32)
    f_ref[16:32, :] = prev_ref[0]
    f_ref[pl.ds(POOL_PAD, t), :] = u
    s2 = f_ref[pl.ds(8, n - 8), :] + f_ref[pl.ds(7, n - 8), :]
    a_ref[pl.ds(8, n - 8), :] = s2
    s4 = a_ref[pl.ds(16, n - 16), :] + a_ref[pl.ds(14, n - 16), :]
    b_ref[pl.ds(16, n - 16), :] = s4
    s8 = b_ref[pl.ds(24, n - 24), :] + b_ref[pl.ds(20, n - 24), :]
    a_ref[pl.ds(24, n - 24), :] = s8
    s16 = a_ref[pl.ds(32, t), :] + a_ref[pl.ds(24, t), :]
    lane = lax.broadcasted_iota(jnp.int32, (t, C_WIDTH), 1)
    g0, g1, g2 = lane < POOL_GROUP, lane < 2 * POOL_GROUP, lane < 3 * POOL_GROUP
    total = jnp.where(g0, s2[24:], jnp.where(g1, s4[16:], jnp.where(g2, s8[8:], s16)))
    win = jnp.where(g0, POOL_WINDOWS[0],
                    jnp.where(g1, POOL_WINDOWS[1],
                              jnp.where(g2, POOL_WINDOWS[2], POOL_WINDOWS[3])))
    pos = pos0 + lax.broadcasted_iota(jnp.int32, (t, C_WIDTH), 0)
    cnt = jnp.minimum(pos + 1, win).astype(F32)
    pooled = total / cnt - u
    o_ref[0] = (_dot(pooled.astype(BF16), w_ref[...]) * scale_ref[...]).astype(BF16)


def _pool(u, prev16, w_blk, scale, *, pos0):
    bn, t, _ = u.shape
    kern = functools.partial(_pool_kernel, t=t, pos0=pos0)
    bmap = lambda b: (b, 0, 0)
    n = t + POOL_PAD
    return pl.pallas_call(
        kern,
        out_shape=jax.ShapeDtypeStruct((bn, t, C_WIDTH), BF16),
        grid=(bn,),
        in_specs=[pl.BlockSpec((1, t, C_WIDTH), bmap),
                  pl.BlockSpec((1, POOL_MAX, C_WIDTH), bmap),
                  _const_spec((C_WIDTH, C_WIDTH)),
                  _const_spec((1, C_WIDTH))],
        out_specs=pl.BlockSpec((1, t, C_WIDTH), bmap),
        scratch_shapes=[pltpu.VMEM((n, C_WIDTH), F32)] * 3,
        compiler_params=_params(("parallel",)),
        name="pool",
    )(u, prev16, w_blk, scale)


def _pool_weight(w_pool):
    w = jnp.zeros((C_WIDTH, C_WIDTH), w_pool.dtype)
    for g in range(len(POOL_WINDOWS)):
        sl = slice(g * POOL_GROUP, (g + 1) * POOL_GROUP)
        w = w.at[sl, sl].set(w_pool[g])
    return w.astype(BF16)


def _outmlp_kernel(ya_ref, yb_ref, yc_ref, x_ref, woa_ref, wob_ref, woc_ref,
                   g1_ref, g2_ref, g3_ref, w1_ref, w2_ref, o_ref):
    y = _dot(ya_ref[...], woa_ref[...]) + _dot(yb_ref[...], wob_ref[...])
    y = y + _dot(yc_ref[...], woc_ref[...])
    x1 = x_ref[...] + _rms(y, g1_ref[...])
    h = _rms(x1, g2_ref[...]).astype(BF16)
    m = jnp.zeros_like(x1)
    for f in range(D_FF // D_MODEL):
        sl = slice(f * D_MODEL, (f + 1) * D_MODEL)
        a = jnp.maximum(_dot(h, w1_ref[:, sl]), 0.0)
        m = m + _dot((a * a).astype(BF16), w2_ref[sl, :])
    o_ref[...] = x1 + _rms(m, g3_ref[...])


def _outmlp(ya, yb, yc, x, woa, wob, woc, g1, g2, g3, w1, w2):
    n = x.shape[0]
    assert n % ROW_TILE == 0
    row = lambda i: (i, 0)
    return pl.pallas_call(
        _outmlp_kernel,
        out_shape=jax.ShapeDtypeStruct((n, D_MODEL), F32),
        grid=(n // ROW_TILE,),
        in_specs=[pl.BlockSpec((ROW_TILE, ATT_WIDTH), row),
                  pl.BlockSpec((ROW_TILE, ATT_WIDTH), row),
                  pl.BlockSpec((ROW_TILE, C_WIDTH), row),
                  pl.BlockSpec((ROW_TILE, D_MODEL), row),
                  _const_spec((ATT_WIDTH, D_MODEL)),
                  _const_spec((ATT_WIDTH, D_MODEL)),
                  _const_spec((C_WIDTH, D_MODEL)),
                  _const_spec((1, D_MODEL)),
                  _const_spec((1, D_MODEL)),
                  _const_spec((1, D_MODEL)),
                  _const_spec((D_MODEL, D_FF)),
                  _const_spec((D_FF, D_MODEL))],
        out_specs=pl.BlockSpec((ROW_TILE, D_MODEL), row),
        compiler_params=_params(("parallel",)),
        name="outmlp",
    )(ya, yb, yc, x, woa, wob, woc, g1, g2, g3, w1, w2)


def _pad_axis(x, axis, size):
    pads = [(0, 0)] * x.ndim
    pads[axis] = (0, size - x.shape[axis])
    return jnp.pad(x, pads)


def _layer(x, weights, tri, *, seq, band_tab, band_tq, band_shift, q_pos0, past=None):
    (g_pre_mix, w_proj, w_proj_t, w_pool_blk, pool_scale, woa, wob, woc, g_post_mix,
     g_pre_mlp, w1, w2, g_post_mlp) = weights
    bn = x.shape[0] // seq
    qa, ka, va, qb, kb, vb, qi, ki2, u, wit, vbt = _proj(x, g_pre_mix, w_proj, w_proj_t)
    per_batch = lambda t: t.reshape(bn, seq, t.shape[-1])
    qa, ka, va, qb, kb, vb, qi, ki2, u = map(per_batch, (qa, ka, va, qb, kb, vb, qi, ki2, u))

    if past is None:
        ka_all, va_all, kb_all, ki2_all = ka, va, kb, ki2
        vt_all = vbt.reshape(ATT_WIDTH, bn, seq).transpose(1, 0, 2)
        prev16 = jnp.zeros((bn, POOL_MAX, C_WIDTH), F32)
        n_keys = seq
        sparse_tq, qi_q, qb_q, wit_q = KEY_TILE, qi, qb, wit
    else:
        c_a_k, c_a_v, c_b_k, c_b_v, c_b_kidx, c_pool = past
        flat = lambda t: t.reshape(bn, t.shape[1], ATT_WIDTH)
        band_rows = 3 * KEY_TILE
        ka_all = _pad_axis(jnp.concatenate([flat(c_a_k), ka], axis=1), 1, band_rows)
        va_all = _pad_axis(jnp.concatenate([flat(c_a_v), va], axis=1), 1, band_rows)
        n_keys = c_b_k.shape[1] + seq
        key_rows = -(-n_keys // KEY_TILE) * KEY_TILE
        kb_all = _pad_axis(jnp.concatenate([flat(c_b_k), kb], axis=1), 1, key_rows)
        vb_all = _pad_axis(jnp.concatenate([flat(c_b_v), vb], axis=1), 1, key_rows)
        vt_all = vb_all.astype(BF16).transpose(0, 2, 1)
        c_ki2 = jnp.concatenate([c_b_kidx, c_b_kidx], axis=-1)
        ki2_all = _pad_axis(jnp.concatenate([c_ki2, ki2], axis=1), 1, key_rows)
        prev16 = jnp.pad(c_pool, ((0, 0), (1, 0), (0, 0)))
        sparse_tq = LANES
        qi_q, qb_q = _pad_axis(qi, 1, LANES), _pad_axis(qb, 1, LANES)
        wit_q = _pad_axis(wit.reshape(LANES, bn, seq), 2, LANES).reshape(LANES, bn * LANES)

    ya = _band(qa, ka_all, va_all, band_tab, tq=band_tq, shift=band_shift)
    yb = _sparse(qi_q, wit_q, ki2_all, qb_q, kb_all, _value_tiles(vt_all), tri,
                 tq=sparse_tq, n_real=min(seq, sparse_tq), q_pos0=q_pos0, n_keys=n_keys)[:, :seq]
    yc = _pool(u, prev16, w_pool_blk, pool_scale, pos0=q_pos0)
    flat2 = lambda t: t.reshape(bn * seq, t.shape[-1])
    x = _outmlp(flat2(ya), flat2(yb), flat2(yc), x, woa, wob, woc,
                g_post_mix, g_pre_mlp, g_post_mlp, w1, w2)
    heads = lambda t: t.reshape(bn, t.shape[1], N_HEADS, HEAD_DIM)
    return x, (heads(ka), heads(va), heads(kb), heads(vb), ki2[..., :IDX_DIM], u)


def kernel(x_prompt, x_sample, cache_a_k, cache_a_v, cache_b_k, cache_b_v, cache_b_kidx, state_pool, g_pre_mix, w_in, rel_bias, w_pool, pool_scale, w_out, g_post_mix, g_pre_mlp, w_ff1, w_ff2, g_post_mlp):
    batch, seq, _ = x_prompt.shape
    dec_batch, dec_seq, _ = x_sample.shape
    depth = w_in.shape[0]
    past_len = cache_b_k.shape[2]
    n_a = cache_a_k.shape[2]
    assert seq % KEY_TILE == 0 and (batch * seq) % ROW_TILE == 0
    assert (dec_batch * dec_seq) % ROW_TILE == 0 and n_a + dec_seq <= 3 * KEY_TILE
    assert dec_seq <= LANES

    tri = (jnp.arange(KEY_TILE)[:, None] >= jnp.arange(KEY_TILE)[None, :]).astype(BF16)
    xp = x_prompt.reshape(batch * seq, D_MODEL)
    xs = x_sample.reshape(dec_batch * dec_seq, D_MODEL)
    p_states, s_states = [], []
    row = lambda t: t.reshape(1, -1)
    for l in range(depth):
        weights = (row(g_pre_mix[l]), *_proj_weight(w_in[l]), _pool_weight(w_pool[l]),
                   row(pool_scale[l]),
                   w_out[l, :ATT_WIDTH].astype(BF16),
                   w_out[l, ATT_WIDTH:2 * ATT_WIDTH].astype(BF16),
                   w_out[l, 2 * ATT_WIDTH:].astype(BF16),
                   row(g_post_mix[l]), row(g_pre_mlp[l]),
                   w_ff1[l].astype(BF16), w_ff2[l].astype(BF16), row(g_post_mlp[l]))
        xp, sp = _layer(xp, weights, tri, seq=seq,
                        band_tab=_band_table(rel_bias[l], 2 * KEY_TILE, 0, KEY_TILE, 3 * KEY_TILE),
                        band_tq=KEY_TILE, band_shift=2, q_pos0=0)
        xs, ss = _layer(xs, weights, tri, seq=dec_seq,
                        band_tab=_band_table(rel_bias[l], past_len, past_len - n_a, dec_seq,
                                             n_a + dec_seq),
                        band_tq=dec_seq, band_shift=0, q_pos0=past_len,
                        past=(cache_a_k[l], cache_a_v[l], cache_b_k[l], cache_b_v[l],
                              cache_b_kidx[l], state_pool[l]))
        n_keep = min(A_WINDOW, seq)
        ka, va, kb, vb, ki, u = sp
        p_states.append((ka[:, seq - n_keep:], va[:, seq - n_keep:], kb, vb, ki,
                         u[:, seq - (POOL_MAX - 1):]))
        ka, va, kb, vb, ki, u = ss
        new_pool = jnp.concatenate([state_pool[l], u], axis=1)[:, dec_seq:]
        s_states.append((ka, va, kb, vb, ki, new_pool))
    stk = lambda states, i: jnp.stack([st[i] for st in states], axis=0)
    return ((xp.reshape(batch, seq, D_MODEL), xs.reshape(dec_batch, dec_seq, D_MODEL))
            + tuple(stk(p_states, i) for i in range(6))
            + tuple(stk(s_states, i) for i in range(6)))
```

```python
import functools

import jax
import jax.numpy as jnp
from jax import lax
from jax.experimental import pallas as pl
from jax.experimental.pallas import tpu as pltpu

D_MODEL = 1024
CHUNK = 64
CHUNK_SHIFT = 6
HEAD_DIM = 64
N_HEADS = 6
ATT_WIDTH = N_HEADS * HEAD_DIM
N_PAIRS = N_HEADS // 2
C_WIDTH = 256
A_LEFT_CHUNKS = 8
A_WINDOW = A_LEFT_CHUNKS * CHUNK
REL_CLIP = 128
IDX_HEADS = 8
IDX_DIM = 64
TOPK = 256
POOL_WINDOWS = (2, 4, 8, 16)
POOL_GROUP = 64
POOL_MAX = 16
POOL_PAD = 32
D_FF = 4 * D_MODEL
RMS_EPS = 1e-6
IDX_SCALE = IDX_HEADS ** -0.5 * IDX_DIM ** -0.5
ATT_SCALE = HEAD_DIM ** -0.5

LANES = 128
SUBLANES = 8
KEY_TILE = 256
KEY_TILE_SHIFT = 8
ROW_TILE = 512
NEG = -1e30
STEP_GROUP = 4
FIRST_STEPS = 24
SEARCH_STEPS = 40
VMEM_LIMIT = 48 * 1024 * 1024

_OFF_KB, _OFF_QI, _OFF_KI, _OFF_WI, _OFF_U, _OFF_END = 1536, 2304, 2816, 2880, 2888, 3144

F32 = jnp.float32
BF16 = jnp.bfloat16


def _const_spec(shape):
    zeros = (0,) * len(shape)
    return pl.BlockSpec(shape, lambda *_: zeros, pipeline_mode=pl.Buffered(1))


def _params(semantics):
    return pltpu.CompilerParams(dimension_semantics=semantics, vmem_limit_bytes=VMEM_LIMIT)


def _rms(x, g):
    ms = jnp.mean(x * x, axis=-1, keepdims=True)
    return x * lax.rsqrt(ms + RMS_EPS) * g


def _dot(a, b):
    return jnp.dot(a, b, preferred_element_type=F32)


def _dot_t(a, b):
    return lax.dot_general(a, b, (((1,), (1,)), ((), ())), preferred_element_type=F32)


def _softmax_step(s, vt, m, l, acc):
    m_new = jnp.maximum(m, jnp.max(s, axis=-1, keepdims=True))
    alpha = jnp.exp(m - m_new)
    p = jnp.exp(s - m_new)
    l = alpha * l + jnp.sum(p, axis=-1, keepdims=True)
    acc = alpha * acc + _dot(p.astype(BF16), vt)
    return m_new, l, acc


def _half_masks(rows):
    lane = lax.broadcasted_iota(jnp.int32, (rows, LANES), 1)
    first = lane < HEAD_DIM
    m0 = jnp.where(first, 1.0, 0.0).astype(BF16)
    m1 = jnp.where(first, 0.0, 1.0).astype(BF16)
    return first, m0, m1


def _fold(x, op):
    parts = [x[i:i + SUBLANES] for i in range(0, x.shape[0], SUBLANES)]
    while len(parts) > 1:
        parts = [op(parts[i], parts[i + 1]) for i in range(0, len(parts), 2)]
    return parts[0]


def _proj_kernel(x_ref, g_ref, w_ref, wtail_ref, wt_ref, qa_ref, ka_ref, va_ref, qb_ref, kb_ref,
                 vb_ref, qi_ref, ki2_ref, u_ref, wit_ref, vbt_ref):
    h = _rms(x_ref[...], g_ref[...]).astype(BF16)
    z = _dot(h, w_ref[:, 0:768])
    qa_ref[...] = (z[:, 0:384] * ATT_SCALE).astype(BF16)
    ka_ref[...] = z[:, 384:768]
    z = _dot(h, w_ref[:, 768:1536])
    va_ref[...] = z[:, 0:384]
    qb_ref[...] = (z[:, 384:768] * ATT_SCALE).astype(BF16)
    z = _dot(h, w_ref[:, 1536:2304])
    kb_ref[...] = z[:, 0:384]
    vb_ref[...] = z[:, 384:768]
    qi_ref[...] = _dot(h, w_ref[:, 2304:2816]).astype(BF16)
    z = _dot(h, wtail_ref[...])
    ki2_ref[...] = z[:, 0:128]
    u_ref[...] = z[:, 128:384]
    zt = _dot_t(wt_ref[...], h)
    wit_ref[...] = zt[0:LANES]
    vbt_ref[...] = zt[LANES:].astype(BF16)


def _proj(x, g, w, wtail, wt):
    n = x.shape[0]
    assert n % ROW_TILE == 0
    widths = (ATT_WIDTH,) * 6 + (IDX_HEADS * IDX_DIM, LANES, C_WIDTH)
    dtypes = (BF16, F32, F32, BF16, F32, F32, BF16, F32, F32)
    row = lambda i: (i, 0)
    col = lambda i: (0, i)
    out_shape = tuple(jax.ShapeDtypeStruct((n, wd), dt) for wd, dt in zip(widths, dtypes))
    out_shape += (jax.ShapeDtypeStruct((LANES, n), F32), jax.ShapeDtypeStruct((ATT_WIDTH, n), BF16))
    out_specs = tuple(pl.BlockSpec((ROW_TILE, wd), row) for wd in widths)
    out_specs += (pl.BlockSpec((LANES, ROW_TILE), col), pl.BlockSpec((ATT_WIDTH, ROW_TILE), col))
    return pl.pallas_call(
        _proj_kernel,
        out_shape=out_shape,
        grid=(n // ROW_TILE,),
        in_specs=[pl.BlockSpec((ROW_TILE, D_MODEL), row),
                  _const_spec((1, D_MODEL)),
                  _const_spec((D_MODEL, _OFF_KI)),
                  _const_spec((D_MODEL, 2 * IDX_DIM + C_WIDTH)),
                  _const_spec((LANES + ATT_WIDTH, D_MODEL))],
        out_specs=out_specs,
        compiler_params=_params(("parallel",)),
        name="proj",
    )(x, g, w, wtail, wt)


def _proj_weight(w_in):
    w_bf = w_in.astype(BF16)
    ki = w_bf[:, _OFF_KI:_OFF_WI]
    wtail = jnp.concatenate([ki, ki, w_bf[:, _OFF_U:_OFF_END]], axis=1)
    wi_t = jnp.pad(w_bf[:, _OFF_WI:_OFF_U].T, ((0, LANES - IDX_HEADS), (0, 0)))
    vb_t = w_bf[:, _OFF_KB + ATT_WIDTH:_OFF_QI].T
    return w_bf[:, :_OFF_KI], wtail, jnp.concatenate([wi_t, vb_t], axis=0)


def _band_kernel(q_ref, k_ref, v_ref, tab_ref, o_ref, *, tq, shift):
    j = pl.program_id(1)
    first, m0, m1 = _half_masks(tq)
    for pair in range(N_PAIRS):
        cols = slice(LANES * pair, LANES * (pair + 1))
        qp = q_ref[0, :, cols]
        outs = []
        for hh in range(2):
            qm = qp * (m0 if hh == 0 else m1)
            m = jnp.full((tq, 1), NEG, F32)
            l = jnp.zeros((tq, 1), F32)
            acc = jnp.zeros((tq, LANES), F32)
            for t in range(3):
                blk = j + (t - shift)
                start = pl.multiple_of(jnp.maximum(blk, 0) * KEY_TILE, KEY_TILE)
                kt = k_ref[0, pl.ds(start, KEY_TILE), cols].astype(BF16)
                vt = v_ref[0, pl.ds(start, KEY_TILE), cols].astype(BF16)
                s = _dot_t(qm, kt) + tab_ref[2 * pair + hh, t]
                s = jnp.where(blk >= 0, s, NEG)
                m, l, acc = _softmax_step(s, vt, m, l, acc)
            outs.append(acc / l)
        o_ref[0, :, cols] = jnp.where(first, outs[0], outs[1]).astype(BF16)


def _band(q, k, v, tab, *, tq, shift):
    bn, tq_total, _ = q.shape
    tk_total = k.shape[1]
    kern = functools.partial(_band_kernel, tq=tq, shift=shift)
    qmap = lambda b, j: (b, j, 0)
    kmap = lambda b, j: (b, 0, 0)
    return pl.pallas_call(
        kern,
        out_shape=jax.ShapeDtypeStruct((bn, tq_total, ATT_WIDTH), BF16),
        grid=(bn, tq_total // tq),
        in_specs=[pl.BlockSpec((1, tq, ATT_WIDTH), qmap),
                  pl.BlockSpec((1, tk_total, ATT_WIDTH), kmap),
                  pl.BlockSpec((1, tk_total, ATT_WIDTH), kmap),
                  _const_spec(tab.shape)],
        out_specs=pl.BlockSpec((1, tq, ATT_WIDTH), qmap),
        compiler_params=_params(("parallel", "arbitrary")),
        name="band",
    )(q, k, v, tab)


def _band_table(rel_bias, q_pos0, k_pos0, tq, n_real):
    tk = 3 * KEY_TILE
    span = tq + tk - 1
    diff = (q_pos0 - k_pos0) - (tk - 1) + jnp.arange(span)
    vec = jnp.take(rel_bias.astype(F32), jnp.clip(diff, -REL_CLIP, REL_CLIP) + REL_CLIP, axis=1)
    h = vec.shape[0]
    skew = jnp.tile(vec, (1, tq + 1))[:, :tq * (span + 1)].reshape(h, tq, span + 1)
    bias = jnp.flip(skew[:, :, :tk], axis=-1)
    qc = (q_pos0 + jnp.arange(tq)) // CHUNK
    kc = (k_pos0 + jnp.arange(tk)) // CHUNK
    valid = ((kc[None, :] <= qc[:, None]) & (kc[None, :] >= qc[:, None] - A_LEFT_CHUNKS)
             & (jnp.arange(tk) < n_real)[None, :])
    tab = jnp.where(valid[None], bias, NEG)
    return tab.reshape(h, tq, 3, KEY_TILE).transpose(0, 2, 1, 3)


def _sparse_kernel(qi_ref, wit_ref, ki2_ref, qb_ref, kb_ref, vt_ref, tri_ref, o_ref,
                   sc_ref, mb_ref, qim_ref, qbm_ref, acc_ref, out_ref, s_ref, p_ref,
                   *, tq, n_real, q_pos0, n_keys):
    j = pl.program_id(1)
    q_first = q_pos0 + j * n_real
    last_chunk = lax.shift_right_logical(q_first + (n_real - 1), CHUNK_SHIFT)
    k_end = jnp.minimum((last_chunk + 1) * CHUNK, n_keys)
    n_tiles = lax.shift_right_logical(k_end + (KEY_TILE - 1), KEY_TILE_SHIFT)

    lane = lax.broadcasted_iota(jnp.int32, (1, tq), 1)
    q_pos = q_first + lane
    k_lim = jnp.minimum((lax.shift_right_logical(q_pos, CHUNK_SHIFT) + 1) * CHUNK, n_keys)
    key_row = lax.broadcasted_iota(jnp.int32, (KEY_TILE, tq), 0)

    _, m0, m1 = _half_masks(tq)
    for h in range(IDX_HEADS):
        pair = slice(LANES * (h // 2), LANES * (h // 2 + 1))
        qim_ref[h] = qi_ref[0, :, pair] * (m0 if h % 2 == 0 else m1)
    for h in range(N_HEADS):
        pair = slice(LANES * (h // 2), LANES * (h // 2 + 1))
        qbm_ref[h] = qb_ref[0, :, pair] * (m0 if h % 2 == 0 else m1)
        acc_ref[h] = jnp.zeros((HEAD_DIM, tq), F32)
    w = wit_ref[...]

    def score_tile(t, carry):
        rmin8, rmax8 = carry
        start = pl.multiple_of(t * KEY_TILE, KEY_TILE)
        kt = ki2_ref[0, pl.ds(start, KEY_TILE), :].astype(BF16)
        acc = jnp.zeros((KEY_TILE, tq), F32)
        for h in range(IDX_HEADS):
            acc = acc + jnp.maximum(_dot_t(kt, qim_ref[h]), 0.0) * w[h:h + 1, :]
        sc = acc * IDX_SCALE
        adm = (start + key_row) < k_lim
        lowest = jnp.where(adm, sc, -jnp.inf)
        sc_ref[t] = lowest
        rmax8 = jnp.maximum(rmax8, _fold(lowest, jnp.maximum))
        rmin8 = jnp.minimum(rmin8, _fold(jnp.where(adm, sc, jnp.inf), jnp.minimum))
        return rmin8, rmax8

    rmin8, rmax8 = lax.fori_loop(
        0, n_tiles, score_tile,
        (jnp.full((SUBLANES, tq), jnp.inf, F32), jnp.full((SUBLANES, tq), -jnp.inf, F32)))
    rmin = jnp.min(rmin8, axis=0, keepdims=True)
    rmax = jnp.max(rmax8, axis=0, keepdims=True)

    def count_ge(thr):
        def body(t, c8):
            return c8 + _fold(jnp.where(sc_ref[t] >= thr, 1.0, 0.0), jnp.add)
        c8 = lax.fori_loop(0, n_tiles, body, jnp.zeros((SUBLANES, tq), F32))
        return jnp.sum(c8, axis=0, keepdims=True)

    def n_unsettled(c_lo):
        return jnp.sum(jnp.where(c_lo > TOPK, 1.0, 0.0))

    def search(carry, last_step):
        def cond(c):
            return jnp.logical_and(c[0] < last_step, c[1] > 0.0)

        def step(c):
            s, _, lo, hi, c_lo, c_hi = c
            for _ in range(STEP_GROUP):
                mid = lo + (hi - lo) * 0.5
                c_mid = count_ge(mid)
                unsettled = c_lo > TOPK
                enough = c_mid >= TOPK
                lo, c_lo = (jnp.where(unsettled, jnp.where(enough, mid, lo), lo),
                            jnp.where(unsettled, jnp.where(enough, c_mid, c_lo), c_lo))
                hi, c_hi = (jnp.where(unsettled, jnp.where(enough, hi, mid), hi),
                            jnp.where(unsettled, jnp.where(enough, c_hi, c_mid), c_hi))
            return s + STEP_GROUP, n_unsettled(c_lo), lo, hi, c_lo, c_hi

        return lax.while_loop(cond, step, carry)

    hi0 = rmax + jnp.maximum(jnp.abs(rmax), 1e-30) * (2.0 ** -10)
    c_lo0 = jnp.where(lane < n_real, k_lim, 0).astype(F32)
    steps, n_open, lo, hi, c_lo, c_hi = search(
        (jnp.int32(0), n_unsettled(c_lo0), rmin, hi0, c_lo0, jnp.zeros((1, tq), F32)),
        FIRST_STEPS)

    def n_untied():
        def body(t, carry):
            vmin8, vmax8 = carry
            x = sc_ref[t]
            vmax8 = jnp.maximum(vmax8, _fold(
                jnp.where(x >= lo, jnp.where(x < hi, x, -jnp.inf), -jnp.inf), jnp.maximum))
            vmin8 = jnp.minimum(vmin8, _fold(
                jnp.where(x >= lo, jnp.where(x < hi, x, jnp.inf), jnp.inf), jnp.minimum))
            return vmin8, vmax8
        vmin8, vmax8 = lax.fori_loop(
            0, n_tiles, body,
            (jnp.full((SUBLANES, tq), jnp.inf, F32), jnp.full((SUBLANES, tq), -jnp.inf, F32)))
        vmin = jnp.min(vmin8, axis=0, keepdims=True)
        vmax = jnp.max(vmax8, axis=0, keepdims=True)
        return jnp.sum(jnp.where(c_lo > TOPK, jnp.where(vmax == vmin, 0.0, 1.0), 0.0))

    n_open = lax.cond(n_open > 0.0, n_untied, lambda: jnp.float32(0.0))
    _, _, lo, hi, _, c_hi = search((steps, n_open, lo, hi, c_lo, c_hi), SEARCH_STEPS)

    need = TOPK - c_hi

    def mask_tile(t, run):
        x = sc_ref[t]
        inr = jnp.where(x >= lo, jnp.where(x < hi, 1.0, 0.0), 0.0)
        rank = _dot(tri_ref[...], inr.astype(BF16)) + run
        mb_ref[t] = jnp.where(
            x >= hi, 0.0, jnp.where(inr > 0.0, jnp.where(rank <= need, 0.0, NEG), NEG))
        return rank[KEY_TILE - 1:KEY_TILE, :]

    lax.fori_loop(0, n_tiles, mask_tile, jnp.zeros((1, tq), F32))

    def attend(t, carry):
        start = pl.multiple_of(t * KEY_TILE, KEY_TILE)
        m_new = []
        for pair in range(N_PAIRS):
            kt = kb_ref[0, pl.ds(start, KEY_TILE), LANES * pair:LANES * (pair + 1)].astype(BF16)
            for h in (2 * pair, 2 * pair + 1):
                s = _dot_t(kt, qbm_ref[h]) + mb_ref[t]
                s_ref[h] = s
                col_max = jnp.max(_fold(s, jnp.maximum), axis=0, keepdims=True)
                m_new.append(jnp.maximum(carry[h][0], col_max))
        new = []
        for h in range(N_HEADS):
            m_old, l_old = carry[h]
            alpha = jnp.exp(m_old - m_new[h])
            p = jnp.exp(s_ref[h] - m_new[h])
            p_ref[h] = p.astype(BF16)
            l_new = alpha * l_old + jnp.sum(_fold(p, jnp.add), axis=0, keepdims=True)
            acc_ref[h] = alpha * acc_ref[h]
            new.append((m_new[h], l_new))
        for h in range(N_HEADS):
            vt = vt_ref[0, t, HEAD_DIM * h:HEAD_DIM * (h + 1), :]
            acc_ref[h] = acc_ref[h] + _dot(vt, p_ref[h])
        return tuple(new)

    stats = lax.fori_loop(
        0, n_tiles, attend,
        tuple((jnp.full((1, tq), NEG, F32), jnp.zeros((1, tq), F32)) for _ in range(N_HEADS)))
    for h in range(N_HEADS):
        out_ref[HEAD_DIM * h:HEAD_DIM * (h + 1), :] = acc_ref[h] / stats[h][1]
    o_ref[0] = out_ref[...].T.astype(BF16)


def _sparse(qi, wit, ki2, qb, kb, vt, tri, *, tq, n_real, q_pos0, n_keys):
    bn, tq_total, _ = qb.shape
    tk_total = kb.shape[1]
    assert tk_total % KEY_TILE == 0 and n_keys <= tk_total and tq % LANES == 0
    max_tiles = tk_total // KEY_TILE
    blocks = tq_total // tq
    kern = functools.partial(_sparse_kernel, tq=tq, n_real=n_real, q_pos0=q_pos0, n_keys=n_keys)
    qmap = lambda b, j: (b, j, 0)
    kmap = lambda b, j: (b, 0, 0)
    return pl.pallas_call(
        kern,
        out_shape=jax.ShapeDtypeStruct((bn, tq_total, ATT_WIDTH), BF16),
        grid=(bn, blocks),
        in_specs=[pl.BlockSpec((1, tq, IDX_HEADS * IDX_DIM), qmap),
                  pl.BlockSpec((LANES, tq), lambda b, j: (0, b * blocks + j)),
                  pl.BlockSpec((1, tk_total, LANES), kmap),
                  pl.BlockSpec((1, tq, ATT_WIDTH), qmap),
                  pl.BlockSpec((1, tk_total, ATT_WIDTH), kmap),
                  pl.BlockSpec((1, max_tiles, ATT_WIDTH, KEY_TILE), lambda b, j: (b, 0, 0, 0)),
                  _const_spec((KEY_TILE, KEY_TILE))],
        out_specs=pl.BlockSpec((1, tq, ATT_WIDTH), qmap),
        scratch_shapes=[pltpu.VMEM((max_tiles, KEY_TILE, tq), F32),
                        pltpu.VMEM((max_tiles, KEY_TILE, tq), F32),
                        pltpu.VMEM((IDX_HEADS, tq, LANES), BF16),
                        pltpu.VMEM((N_HEADS, tq, LANES), BF16),
                        pltpu.VMEM((N_HEADS, HEAD_DIM, tq), F32),
                        pltpu.VMEM((ATT_WIDTH, tq), F32),
                        pltpu.VMEM((N_HEADS, KEY_TILE, tq), F32),
                        pltpu.VMEM((N_HEADS, KEY_TILE, tq), BF16)],
        compiler_params=_params(("parallel", "arbitrary")),
        name="sparse",
    )(qi, wit, ki2, qb, kb, vt, tri)


def _value_tiles(vt_all):
    bn, width, keys = vt_all.shape
    return vt_all.reshape(bn, width, keys // KEY_TILE, KEY_TILE).transpose(0, 2, 1, 3)


def _pool_kernel(u_ref, prev_ref, w_ref, scale_ref, o_ref, f_ref, a_ref, b_ref, *, t, pos0):
    n = t + POOL_PAD
    u = u_ref[0]
    f_ref[0:16, :] = jnp.zeros((16, C_WIDTH), F32)
    f_ref[16:32, :] = prev_ref[0]
    f_ref[pl.ds(POOL_PAD, t), :] = u
    s2 = f_ref[pl.ds(8, n - 8), :] + f_ref[pl.ds(7, n - 8), :]
    a_ref[pl.ds(8, n - 8), :] = s2
    s4 = a_ref[pl.ds(16, n - 16), :] + a_ref[pl.ds(14, n - 16), :]
    b_ref[pl.ds(16, n - 16), :] = s4
    s8 = b_ref[pl.ds(24, n - 24), :] + b_ref[pl.ds(20, n - 24), :]
    a_ref[pl.ds(24, n - 24), :] = s8
    s16 = a_ref[pl.ds(32, t), :] + a_ref[pl.ds(24, t), :]
    lane = lax.broadcasted_iota(jnp.int32, (t, C_WIDTH), 1)
    g0, g1, g2 = lane < POOL_GROUP, lane < 2 * POOL_GROUP, lane < 3 * POOL_GROUP
    total = jnp.where(g0, s2[24:], jnp.where(g1, s4[16:], jnp.where(g2, s8[8:], s16)))
    win = jnp.where(g0, POOL_WINDOWS[0],
                    jnp.where(g1, POOL_WINDOWS[1],
                              jnp.where(g2, POOL_WINDOWS[2], POOL_WINDOWS[3])))
    pos = pos0 + lax.broadcasted_iota(jnp.int32, (t, C_WIDTH), 0)
    cnt = jnp.minimum(pos + 1, win).astype(F32)
    pooled = total / cnt - u
    o_ref[0] = (_dot(pooled.astype(BF16), w_ref[...]) * scale_ref[...]).astype(BF16)


def _pool(u, prev16, w_blk, scale, *, pos0):
    bn, t, _ = u.shape
    kern = functools.partial(_pool_kernel, t=t, pos0=pos0)
    bmap = lambda b: (b, 0, 0)
    n = t + POOL_PAD
    return pl.pallas_call(
        kern,
        out_shape=jax.ShapeDtypeStruct((bn, t, C_WIDTH), BF16),
        grid=(bn,),
        in_specs=[pl.BlockSpec((1, t, C_WIDTH), bmap),
                  pl.BlockSpec((1, POOL_MAX, C_WIDTH), bmap),
                  _const_spec((C_WIDTH, C_WIDTH)),
                  _const_spec((1, C_WIDTH))],
        out_specs=pl.BlockSpec((1, t, C_WIDTH), bmap),
        scratch_shapes=[pltpu.VMEM((n, C_WIDTH), F32)] * 3,
        compiler_params=_params(("parallel",)),
        name="pool",
    )(u, prev16, w_blk, scale)


def _pool_weight(w_pool):
    w = jnp.zeros((C_WIDTH, C_WIDTH), w_pool.dtype)
    for g in range(len(POOL_WINDOWS)):
        sl = slice(g * POOL_GROUP, (g + 1) * POOL_GROUP)
        w = w.at[sl, sl].set(w_pool[g])
    return w.astype(BF16)


def _outmlp_kernel(ya_ref, yb_ref, yc_ref, x_ref, woa_ref, wob_ref, woc_ref,
                   g1_ref, g2_ref, g3_ref, w1_ref, w2_ref, o_ref):
    y = _dot(ya_ref[...], woa_ref[...]) + _dot(yb_ref[...], wob_ref[...])
    y = y + _dot(yc_ref[...], woc_ref[...])
    x1 = x_ref[...] + _rms(y, g1_ref[...])
    h = _rms(x1, g2_ref[...]).astype(BF16)
    m = jnp.zeros_like(x1)
    for f in range(D_FF // D_MODEL):
        sl = slice(f * D_MODEL, (f + 1) * D_MODEL)
        a = jnp.maximum(_dot(h, w1_ref[:, sl]), 0.0)
        m = m + _dot((a * a).astype(BF16), w2_ref[sl, :])
    o_ref[...] = x1 + _rms(m, g3_ref[...])


def _outmlp(ya, yb, yc, x, woa, wob, woc, g1, g2, g3, w1, w2):
    n = x.shape[0]
    assert n % ROW_TILE == 0
    row = lambda i: (i, 0)
    return pl.pallas_call(
        _outmlp_kernel,
        out_shape=jax.ShapeDtypeStruct((n, D_MODEL), F32),
        grid=(n // ROW_TILE,),
        in_specs=[pl.BlockSpec((ROW_TILE, ATT_WIDTH), row),
                  pl.BlockSpec((ROW_TILE, ATT_WIDTH), row),
                  pl.BlockSpec((ROW_TILE, C_WIDTH), row),
                  pl.BlockSpec((ROW_TILE, D_MODEL), row),
                  _const_spec((ATT_WIDTH, D_MODEL)),
                  _const_spec((ATT_WIDTH, D_MODEL)),
                  _const_spec((C_WIDTH, D_MODEL)),
                  _const_spec((1, D_MODEL)),
                  _const_spec((1, D_MODEL)),
                  _const_spec((1, D_MODEL)),
                  _const_spec((D_MODEL, D_FF)),
                  _const_spec((D_FF, D_MODEL))],
        out_specs=pl.BlockSpec((ROW_TILE, D_MODEL), row),
        compiler_params=_params(("parallel",)),
        name="outmlp",
    )(ya, yb, yc, x, woa, wob, woc, g1, g2, g3, w1, w2)


def _pad_axis(x, axis, size):
    pads = [(0, 0)] * x.ndim
    pads[axis] = (0, size - x.shape[axis])
    return jnp.pad(x, pads)


def _layer(x, weights, tri, *, seq, band_tab, band_tq, band_shift, q_pos0, past=None):
    (g_pre_mix, w_proj, w_proj_tail, w_proj_t, w_pool_blk, pool_scale, woa, wob, woc, g_post_mix,
     g_pre_mlp, w1, w2, g_post_mlp) = weights
    bn = x.shape[0] // seq
    qa, ka, va, qb, kb, vb, qi, ki2, u, wit, vbt = _proj(x, g_pre_mix, w_proj, w_proj_tail,
                                                         w_proj_t)
    per_batch = lambda t: t.reshape(bn, seq, t.shape[-1])
    qa, ka, va, qb, kb, vb, qi, ki2, u = map(per_batch, (qa, ka, va, qb, kb, vb, qi, ki2, u))

    if past is None:
        ka_all, va_all, kb_all, ki2_all = ka, va, kb, ki2
        vt_all = vbt.reshape(ATT_WIDTH, bn, seq).transpose(1, 0, 2)
        prev16 = jnp.zeros((bn, POOL_MAX, C_WIDTH), F32)
        n_keys = seq
        sparse_tq, qi_q, qb_q, wit_q = KEY_TILE, qi, qb, wit
    else:
        c_a_k, c_a_v, c_b_k, c_b_v, c_b_kidx, c_pool = past
        flat = lambda t: t.reshape(bn, t.shape[1], ATT_WIDTH)
        band_rows = 3 * KEY_TILE
        ka_all = _pad_axis(jnp.concatenate([flat(c_a_k), ka], axis=1), 1, band_rows)
        va_all = _pad_axis(jnp.concatenate([flat(c_a_v), va], axis=1), 1, band_rows)
        n_keys = c_b_k.shape[1] + seq
        key_rows = -(-n_keys // KEY_TILE) * KEY_TILE
        kb_all = _pad_axis(jnp.concatenate([flat(c_b_k), kb], axis=1), 1, key_rows)
        vb_all = _pad_axis(jnp.concatenate([flat(c_b_v), vb], axis=1), 1, key_rows)
        vt_all = vb_all.astype(BF16).transpose(0, 2, 1)
        c_ki2 = jnp.concatenate([c_b_kidx, c_b_kidx], axis=-1)
        ki2_all = _pad_axis(jnp.concatenate([c_ki2, ki2], axis=1), 1, key_rows)
        prev16 = jnp.pad(c_pool, ((0, 0), (1, 0), (0, 0)))
        sparse_tq = LANES
        qi_q, qb_q = _pad_axis(qi, 1, LANES), _pad_axis(qb, 1, LANES)
        wit_q = _pad_axis(wit.reshape(LANES, bn, seq), 2, LANES).reshape(LANES, bn * LANES)

    ya = _band(qa, ka_all, va_all, band_tab, tq=band_tq, shift=band_shift)
    yb = _sparse(qi_q, wit_q, ki2_all, qb_q, kb_all, _value_tiles(vt_all), tri,
                 tq=sparse_tq, n_real=min(seq, sparse_tq), q_pos0=q_pos0, n_keys=n_keys)[:, :seq]
    yc = _pool(u, prev16, w_pool_blk, pool_scale, pos0=q_pos0)
    flat2 = lambda t: t.reshape(bn * seq, t.shape[-1])
    x = _outmlp(flat2(ya), flat2(yb), flat2(yc), x, woa, wob, woc,
                g_post_mix, g_pre_mlp, g_post_mlp, w1, w2)
    heads = lambda t: t.reshape(bn, t.shape[1], N_HEADS, HEAD_DIM)
    return x, (heads(ka), heads(va), heads(kb), heads(vb), ki2[..., :IDX_DIM], u)


def kernel(x_prompt, x_sample, cache_a_k, cache_a_v, cache_b_k, cache_b_v, cache_b_kidx, state_pool, g_pre_mix, w_in, rel_bias, w_pool, pool_scale, w_out, g_post_mix, g_pre_mlp, w_ff1, w_ff2, g_post_mlp):
    batch, seq, _ = x_prompt.shape
    dec_batch, dec_seq, _ = x_sample.shape
    depth = w_in.shape[0]
    past_len = cache_b_k.shape[2]
    n_a = cache_a_k.shape[2]
    assert seq % KEY_TILE == 0 and (batch * seq) % ROW_TILE == 0
    assert (dec_batch * dec_seq) % ROW_TILE == 0 and n_a + dec_seq <= 3 * KEY_TILE
    assert dec_seq <= LANES

    tri = (jnp.arange(KEY_TILE)[:, None] >= jnp.arange(KEY_TILE)[None, :]).astype(BF16)
    xp = x_prompt.reshape(batch * seq, D_MODEL)
    xs = x_sample.reshape(dec_batch * dec_seq, D_MODEL)
    p_states, s_states = [], []
    row = lambda t: t.reshape(1, -1)
    for l in range(depth):
        weights = (row(g_pre_mix[l]), *_proj_weight(w_in[l]), _pool_weight(w_pool[l]),
                   row(pool_scale[l]),
                   w_out[l, :ATT_WIDTH].astype(BF16),
                   w_out[l, ATT_WIDTH:2 * ATT_WIDTH].astype(BF16),
                   w_out[l, 2 * ATT_WIDTH:].astype(BF16),
                   row(g_post_mix[l]), row(g_pre_mlp[l]),
                   w_ff1[l].astype(BF16), w_ff2[l].astype(BF16), row(g_post_mlp[l]))
        xp, sp = _layer(xp, weights, tri, seq=seq,
                        band_tab=_band_table(rel_bias[l], 2 * KEY_TILE, 0, KEY_TILE, 3 * KEY_TILE),
                        band_tq=KEY_TILE, band_shift=2, q_pos0=0)
        xs, ss = _layer(xs, weights, tri, seq=dec_seq,
                        band_tab=_band_table(rel_bias[l], past_len, past_len - n_a, dec_seq,
                                             n_a + dec_seq),
                        band_tq=dec_seq, band_shift=0, q_pos0=past_len,
                        past=(cache_a_k[l], cache_a_v[l], cache_b_k[l], cache_b_v[l],
                              cache_b_kidx[l], state_pool[l]))
        n_keep = min(A_WINDOW, seq)
        ka, va, kb, vb, ki, u = sp
        p_states.append((ka[:, seq - n_keep:], va[:, seq - n_keep:], kb, vb, ki,
                         u[:, seq - (POOL_MAX - 1):]))
        ka, va, kb, vb, ki, u = ss
        new_pool = jnp.concatenate([state_pool[l], u], axis=1)[:, dec_seq:]
        s_states.append((ka, va, kb, vb, ki, new_pool))
    stk = lambda states, i: jnp.stack([st[i] for st in states], axis=0)
    return ((xp.reshape(batch, seq, D_MODEL), xs.reshape(dec_batch, dec_seq, D_MODEL))
            + tuple(stk(p_states, i) for i in range(6))
            + tuple(stk(s_states, i) for i in range(6)))
```

```python
import functools

import jax
import jax.numpy as jnp
from jax import lax
from jax.experimental import pallas as pl
from jax.experimental.pallas import tpu as pltpu

D_MODEL = 1024
CHUNK = 64
CHUNK_SHIFT = 6
HEAD_DIM = 64
N_HEADS = 6
ATT_WIDTH = N_HEADS * HEAD_DIM
N_PAIRS = N_HEADS // 2
C_WIDTH = 256
A_LEFT_CHUNKS = 8
A_WINDOW = A_LEFT_CHUNKS * CHUNK
REL_CLIP = 128
IDX_HEADS = 8
IDX_DIM = 64
TOPK = 256
POOL_WINDOWS = (2, 4, 8, 16)
POOL_GROUP = 64
POOL_MAX = 16
POOL_PAD = 32
D_FF = 4 * D_MODEL
RMS_EPS = 1e-6
IDX_SCALE = IDX_HEADS ** -0.5 * IDX_DIM ** -0.5
ATT_SCALE = HEAD_DIM ** -0.5

LANES = 128
SUBLANES = 8
KEY_TILE = 256
KEY_TILE_SHIFT = 8
ROW_TILE = 512
NEG = -1e30
LOG2E = 1.4426950408889634
VALUE_ROWS = HEAD_DIM + 16
STEP_GROUP = 4
FIRST_STEPS = 24
SEARCH_STEPS = 40
VMEM_LIMIT = 48 * 1024 * 1024

_OFF_KB, _OFF_QI, _OFF_KI, _OFF_WI, _OFF_U, _OFF_END = 1536, 2304, 2816, 2880, 2888, 3144

F32 = jnp.float32
BF16 = jnp.bfloat16


def _const_spec(shape):
    zeros = (0,) * len(shape)
    return pl.BlockSpec(shape, lambda *_: zeros, pipeline_mode=pl.Buffered(1))


def _params(semantics):
    return pltpu.CompilerParams(dimension_semantics=semantics, vmem_limit_bytes=VMEM_LIMIT)


def _rms(x, g):
    ms = jnp.mean(x * x, axis=-1, keepdims=True)
    return x * lax.rsqrt(ms + RMS_EPS) * g


def _dot(a, b):
    return jnp.dot(a, b, preferred_element_type=F32)


def _dot_t(a, b):
    return lax.dot_general(a, b, (((1,), (1,)), ((), ())), preferred_element_type=F32)


def _half_masks(rows):
    lane = lax.broadcasted_iota(jnp.int32, (rows, LANES), 1)
    first = lane < HEAD_DIM
    m0 = jnp.where(first, 1.0, 0.0).astype(BF16)
    m1 = jnp.where(first, 0.0, 1.0).astype(BF16)
    return first, m0, m1


def _fold(x, op):
    parts = [x[i:i + SUBLANES] for i in range(0, x.shape[0], SUBLANES)]
    lanes = min(4, len(parts))
    acc = parts[:lanes]
    for i, part in enumerate(parts[lanes:]):
        acc[i % lanes] = op(acc[i % lanes], part)
    while len(acc) > 1:
        acc = [op(acc[i], acc[i + 1]) for i in range(0, len(acc), 2)]
    return acc[0]


def _proj_kernel(x_ref, g_ref, w_ref, wtail_ref, wt_ref, qa_ref, ka_ref, va_ref, qb_ref, kb_ref,
                 vb_ref, qi_ref, ki2_ref, u_ref, wit_ref, vbt_ref, vat_ref):
    h = _rms(x_ref[...], g_ref[...]).astype(BF16)
    z = _dot(h, w_ref[:, 0:768])
    qa_ref[...] = (z[:, 0:384] * (ATT_SCALE * LOG2E)).astype(BF16)
    ka_ref[...] = z[:, 384:768]
    z = _dot(h, w_ref[:, 768:1536])
    va_ref[...] = z[:, 0:384]
    qb_ref[...] = (z[:, 384:768] * (ATT_SCALE * LOG2E)).astype(BF16)
    z = _dot(h, w_ref[:, 1536:2304])
    kb_ref[...] = z[:, 0:384]
    vb_ref[...] = z[:, 384:768]
    qi_ref[...] = _dot(h, w_ref[:, 2304:2816]).astype(BF16)
    z = _dot(h, wtail_ref[...])
    ki2_ref[...] = z[:, 0:128]
    u_ref[...] = z[:, 128:384]
    zt = _dot_t(wt_ref[...], h)
    wit_ref[...] = zt[0:LANES]
    vbt_ref[...] = zt[LANES:LANES + ATT_WIDTH].astype(BF16)
    vat_ref[...] = zt[LANES + ATT_WIDTH:].astype(BF16)


def _proj(x, g, w, wtail, wt):
    n = x.shape[0]
    assert n % ROW_TILE == 0
    widths = (ATT_WIDTH,) * 6 + (IDX_HEADS * IDX_DIM, LANES, C_WIDTH)
    dtypes = (BF16, F32, F32, BF16, F32, F32, BF16, F32, F32)
    row = lambda i: (i, 0)
    col = lambda i: (0, i)
    out_shape = tuple(jax.ShapeDtypeStruct((n, wd), dt) for wd, dt in zip(widths, dtypes))
    out_shape += (jax.ShapeDtypeStruct((LANES, n), F32),)
    out_shape += (jax.ShapeDtypeStruct((ATT_WIDTH, n), BF16),) * 2
    out_specs = tuple(pl.BlockSpec((ROW_TILE, wd), row) for wd in widths)
    out_specs += (pl.BlockSpec((LANES, ROW_TILE), col),)
    out_specs += (pl.BlockSpec((ATT_WIDTH, ROW_TILE), col),) * 2
    return pl.pallas_call(
        _proj_kernel,
        out_shape=out_shape,
        grid=(n // ROW_TILE,),
        in_specs=[pl.BlockSpec((ROW_TILE, D_MODEL), row),
                  _const_spec((1, D_MODEL)),
                  _const_spec((D_MODEL, _OFF_KI)),
                  _const_spec((D_MODEL, 2 * IDX_DIM + C_WIDTH)),
                  _const_spec((LANES + 2 * ATT_WIDTH, D_MODEL))],
        out_specs=out_specs,
        compiler_params=_params(("parallel",)),
        name="proj",
    )(x, g, w, wtail, wt)


def _proj_weight(w_in):
    w_bf = w_in.astype(BF16)
    ki = w_bf[:, _OFF_KI:_OFF_WI]
    wtail = jnp.concatenate([ki, ki, w_bf[:, _OFF_U:_OFF_END]], axis=1)
    wi_t = jnp.pad(w_bf[:, _OFF_WI:_OFF_U].T, ((0, LANES - IDX_HEADS), (0, 0)))
    vb_t = w_bf[:, _OFF_KB + ATT_WIDTH:_OFF_QI].T
    va_t = w_bf[:, 2 * ATT_WIDTH:3 * ATT_WIDTH].T
    return w_bf[:, :_OFF_KI], wtail, jnp.concatenate([wi_t, vb_t, va_t], axis=0)


def _mask_heads(q_ref, qm_ref, heads, tq):
    _, m0, m1 = _half_masks(tq)
    for h in range(heads):
        pair = slice(LANES * (h // 2), LANES * (h // 2 + 1))
        qm_ref[h] = q_ref[0, :, pair] * (m0 if h % 2 == 0 else m1)


def _attend_scratch(tq):
    return [pltpu.VMEM((N_HEADS, VALUE_ROWS, tq), F32),
            pltpu.VMEM((ATT_WIDTH, tq), F32),
            pltpu.VMEM((N_HEADS, KEY_TILE, tq), F32),
            pltpu.VMEM((N_HEADS, KEY_TILE, tq), BF16)]


def _attend_init(acc_ref, tq):
    for h in range(N_HEADS):
        acc_ref[h] = jnp.zeros((VALUE_ROWS, tq), F32)
    return tuple(jnp.full((1, tq), NEG, F32) for _ in range(N_HEADS))


def _attend_tile(k_ref, vt_ref, tile, bias_of_head, qm_ref, acc_ref, s_ref, p_ref, m_run):
    start = pl.multiple_of(tile * KEY_TILE, KEY_TILE)
    m_new = []
    for pair in range(N_PAIRS):
        kt = k_ref[0, pl.ds(start, KEY_TILE), LANES * pair:LANES * (pair + 1)].astype(BF16)
        for h in (2 * pair, 2 * pair + 1):
            s = _dot_t(kt, qm_ref[h]) + bias_of_head(h)
            s_ref[h] = s
            col_max = jnp.max(_fold(s, jnp.maximum), axis=0, keepdims=True)
            m_new.append(jnp.maximum(m_run[h], col_max))
    for h in range(N_HEADS):
        p_ref[h] = jnp.exp2(s_ref[h] - m_new[h]).astype(BF16)
        acc_ref[h] = jnp.exp2(m_run[h] - m_new[h]) * acc_ref[h]
    for h in range(N_HEADS):
        vt = vt_ref[0, tile, VALUE_ROWS * h:VALUE_ROWS * (h + 1), :]
        acc_ref[h] = acc_ref[h] + _dot(vt, p_ref[h])
    return tuple(m_new)


def _attend_finish(acc_ref, out_ref, o_ref):
    for h in range(N_HEADS):
        acc = acc_ref[h]
        out_ref[HEAD_DIM * h:HEAD_DIM * (h + 1), :] = acc[:HEAD_DIM] / acc[HEAD_DIM:HEAD_DIM + 1]
    o_ref[0] = out_ref[...].T.astype(BF16)


def _band_kernel(q_ref, k_ref, vt_ref, tab_ref, o_ref, qm_ref, acc_ref, out_ref, s_ref, p_ref,
                 *, tq, shift):
    j = pl.program_id(1)
    _mask_heads(q_ref, qm_ref, N_HEADS, tq)
    m_run = _attend_init(acc_ref, tq)
    for t in range(3):
        blk = j + (t - shift)
        bias = lambda h, t=t, blk=blk: jnp.where(blk >= 0, tab_ref[h, t], NEG)
        m_run = _attend_tile(k_ref, vt_ref, jnp.maximum(blk, 0), bias, qm_ref, acc_ref, s_ref,
                             p_ref, m_run)
    _attend_finish(acc_ref, out_ref, o_ref)


def _band(q, k, vt, tab, *, tq, shift):
    bn, tq_total, _ = q.shape
    tk_total = k.shape[1]
    max_tiles = tk_total // KEY_TILE
    kern = functools.partial(_band_kernel, tq=tq, shift=shift)
    qmap = lambda b, j: (b, j, 0)
    return pl.pallas_call(
        kern,
        out_shape=jax.ShapeDtypeStruct((bn, tq_total, ATT_WIDTH), BF16),
        grid=(bn, tq_total // tq),
        in_specs=[pl.BlockSpec((1, tq, ATT_WIDTH), qmap),
                  pl.BlockSpec((1, tk_total, ATT_WIDTH), lambda b, j: (b, 0, 0)),
                  pl.BlockSpec((1, max_tiles, N_HEADS * VALUE_ROWS, KEY_TILE),
                               lambda b, j: (b, 0, 0, 0)),
                  _const_spec(tab.shape)],
        out_specs=pl.BlockSpec((1, tq, ATT_WIDTH), qmap),
        scratch_shapes=[pltpu.VMEM((N_HEADS, tq, LANES), BF16),
                        *_attend_scratch(tq)],
        compiler_params=_params(("parallel", "arbitrary")),
        name="band",
    )(q, k, vt, tab)


def _band_table(rel_bias, q_pos0, k_pos0, tq, n_real):
    tk = 3 * KEY_TILE
    span = tq + tk - 1
    diff = (q_pos0 - k_pos0) - (tk - 1) + jnp.arange(span)
    vec = jnp.take(rel_bias.astype(F32), jnp.clip(diff, -REL_CLIP, REL_CLIP) + REL_CLIP, axis=1)
    h = vec.shape[0]
    skew = jnp.tile(vec, (1, tq + 1))[:, :tq * (span + 1)].reshape(h, tq, span + 1)
    bias = jnp.flip(skew[:, :, :tk], axis=-1)
    qc = (q_pos0 + jnp.arange(tq)) // CHUNK
    kc = (k_pos0 + jnp.arange(tk)) // CHUNK
    valid = ((kc[None, :] <= qc[:, None]) & (kc[None, :] >= qc[:, None] - A_LEFT_CHUNKS)
             & (jnp.arange(tk) < n_real)[None, :])
    tab = jnp.where(valid[None], bias * LOG2E, NEG)
    return tab.reshape(h, tq, 3, KEY_TILE).transpose(0, 2, 3, 1)


def _sparse_kernel(qi_ref, wit_ref, ki2_ref, qb_ref, kb_ref, vt_ref, tri_ref, o_ref,
                   sc_ref, mb_ref, qim_ref, qbm_ref, acc_ref, out_ref, s_ref, p_ref,
                   *, tq, n_real, q_pos0, n_keys):
    j = pl.program_id(1)
    q_first = q_pos0 + j * n_real
    last_chunk = lax.shift_right_logical(q_first + (n_real - 1), CHUNK_SHIFT)
    k_end = jnp.minimum((last_chunk + 1) * CHUNK, n_keys)
    n_tiles = lax.shift_right_logical(k_end + (KEY_TILE - 1), KEY_TILE_SHIFT)

    lane = lax.broadcasted_iota(jnp.int32, (1, tq), 1)
    q_pos = q_first + lane
    k_lim = jnp.minimum((lax.shift_right_logical(q_pos, CHUNK_SHIFT) + 1) * CHUNK, n_keys)
    key_row = lax.broadcasted_iota(jnp.int32, (KEY_TILE, tq), 0)

    _mask_heads(qi_ref, qim_ref, IDX_HEADS, tq)
    _mask_heads(qb_ref, qbm_ref, N_HEADS, tq)
    w = wit_ref[0:SUBLANES, :] * IDX_SCALE

    def score_tile(t, carry):
        rmin8, rmax8 = carry
        start = pl.multiple_of(t * KEY_TILE, KEY_TILE)
        kt = ki2_ref[0, pl.ds(start, KEY_TILE), :].astype(BF16)
        sc = jnp.zeros((KEY_TILE, tq), F32)
        for h in range(IDX_HEADS):
            sc = sc + jnp.maximum(_dot_t(kt, qim_ref[h]), 0.0) * w[h:h + 1, :]
        adm = (start + key_row) < k_lim
        lowest = jnp.where(adm, sc, -jnp.inf)
        sc_ref[t] = lowest
        rmax8 = jnp.maximum(rmax8, _fold(lowest, jnp.maximum))
        rmin8 = jnp.minimum(rmin8, _fold(jnp.where(adm, sc, jnp.inf), jnp.minimum))
        return rmin8, rmax8

    rmin8, rmax8 = lax.fori_loop(
        0, n_tiles, score_tile,
        (jnp.full((SUBLANES, tq), jnp.inf, F32), jnp.full((SUBLANES, tq), -jnp.inf, F32)))
    rmin = jnp.min(rmin8, axis=0, keepdims=True)
    rmax = jnp.max(rmax8, axis=0, keepdims=True)

    def count_ge(thr):
        def body(t, c8):
            return c8 + _fold(jnp.where(sc_ref[t] >= thr, 1.0, 0.0), jnp.add)
        c8 = lax.fori_loop(0, n_tiles, body, jnp.zeros((SUBLANES, tq), F32))
        return jnp.sum(c8, axis=0, keepdims=True)

    def n_unsettled(c_lo):
        return jnp.sum(jnp.where(c_lo > TOPK, 1.0, 0.0))

    def search(carry, last_step):
        def cond(c):
            return jnp.logical_and(c[0] < last_step, c[1] > 0.0)

        def step(c):
            s, _, lo, hi, c_lo, c_hi = c
            for _ in range(STEP_GROUP):
                mid = lo + (hi - lo) * 0.5
                c_mid = count_ge(mid)
                unsettled = c_lo > TOPK
                enough = c_mid >= TOPK
                lo, c_lo = (jnp.where(unsettled, jnp.where(enough, mid, lo), lo),
                            jnp.where(unsettled, jnp.where(enough, c_mid, c_lo), c_lo))
                hi, c_hi = (jnp.where(unsettled, jnp.where(enough, hi, mid), hi),
                            jnp.where(unsettled, jnp.where(enough, c_hi, c_mid), c_hi))
            return s + STEP_GROUP, n_unsettled(c_lo), lo, hi, c_lo, c_hi

        return lax.while_loop(cond, step, carry)

    hi0 = rmax + jnp.maximum(jnp.abs(rmax), 1e-30) * (2.0 ** -10)
    c_lo0 = jnp.where(lane < n_real, k_lim, 0).astype(F32)
    steps, n_open, lo, hi, c_lo, c_hi = search(
        (jnp.int32(0), n_unsettled(c_lo0), rmin, hi0, c_lo0, jnp.zeros((1, tq), F32)),
        FIRST_STEPS)

    def n_untied():
        def body(t, carry):
            vmin8, vmax8 = carry
            x = sc_ref[t]
            vmax8 = jnp.maximum(vmax8, _fold(
                jnp.where(x >= lo, jnp.where(x < hi, x, -jnp.inf), -jnp.inf), jnp.maximum))
            vmin8 = jnp.minimum(vmin8, _fold(
                jnp.where(x >= lo, jnp.where(x < hi, x, jnp.inf), jnp.inf), jnp.minimum))
            return vmin8, vmax8
        vmin8, vmax8 = lax.fori_loop(
            0, n_tiles, body,
            (jnp.full((SUBLANES, tq), jnp.inf, F32), jnp.full((SUBLANES, tq), -jnp.inf, F32)))
        vmin = jnp.min(vmin8, axis=0, keepdims=True)
        vmax = jnp.max(vmax8, axis=0, keepdims=True)
        return jnp.sum(jnp.where(c_lo > TOPK, jnp.where(vmax == vmin, 0.0, 1.0), 0.0))

    n_open = lax.cond(n_open > 0.0, n_untied, lambda: jnp.float32(0.0))
    _, _, lo, hi, _, c_hi = search((steps, n_open, lo, hi, c_lo, c_hi), SEARCH_STEPS)

    need = TOPK - c_hi

    def mask_tile(t, run):
        x = sc_ref[t]
        inr = jnp.where(x >= lo, jnp.where(x < hi, 1.0, 0.0), 0.0)
        rank = _dot(tri_ref[...], inr.astype(BF16)) + run
        mb_ref[t] = jnp.where(
            x >= hi, 0.0, jnp.where(inr > 0.0, jnp.where(rank <= need, 0.0, NEG), NEG))
        return rank[KEY_TILE - 1:KEY_TILE, :]

    lax.fori_loop(0, n_tiles, mask_tile, jnp.zeros((1, tq), F32))

    def attend(t, m_run):
        return _attend_tile(kb_ref, vt_ref, t, lambda h: mb_ref[t], qbm_ref, acc_ref, s_ref, p_ref,
                            m_run)

    lax.fori_loop(0, n_tiles, attend, _attend_init(acc_ref, tq))
    _attend_finish(acc_ref, out_ref, o_ref)


def _sparse(qi, wit, ki2, qb, kb, vt, tri, *, tq, n_real, q_pos0, n_keys):
    bn, tq_total, _ = qb.shape
    tk_total = kb.shape[1]
    assert tk_total % KEY_TILE == 0 and n_keys <= tk_total and tq % LANES == 0
    max_tiles = tk_total // KEY_TILE
    blocks = tq_total // tq
    kern = functools.partial(_sparse_kernel, tq=tq, n_real=n_real, q_pos0=q_pos0, n_keys=n_keys)
    qmap = lambda b, j: (b, j, 0)
    kmap = lambda b, j: (b, 0, 0)
    return pl.pallas_call(
        kern,
        out_shape=jax.ShapeDtypeStruct((bn, tq_total, ATT_WIDTH), BF16),
        grid=(bn, blocks),
        in_specs=[pl.BlockSpec((1, tq, IDX_HEADS * IDX_DIM), qmap),
                  pl.BlockSpec((LANES, tq), lambda b, j: (0, b * blocks + j)),
                  pl.BlockSpec((1, tk_total, LANES), kmap),
                  pl.BlockSpec((1, tq, ATT_WIDTH), qmap),
                  pl.BlockSpec((1, tk_total, ATT_WIDTH), kmap),
                  pl.BlockSpec((1, max_tiles, N_HEADS * VALUE_ROWS, KEY_TILE),
                               lambda b, j: (b, 0, 0, 0)),
                  _const_spec((KEY_TILE, KEY_TILE))],
        out_specs=pl.BlockSpec((1, tq, ATT_WIDTH), qmap),
        scratch_shapes=[pltpu.VMEM((max_tiles, KEY_TILE, tq), F32),
                        pltpu.VMEM((max_tiles, KEY_TILE, tq), F32),
                        pltpu.VMEM((IDX_HEADS, tq, LANES), BF16),
                        pltpu.VMEM((N_HEADS, tq, LANES), BF16),
                        *_attend_scratch(tq)],
        compiler_params=_params(("parallel", "arbitrary")),
        name="sparse",
    )(qi, wit, ki2, qb, kb, vt, tri)


def _value_tiles(vt_all):
    bn, _, keys = vt_all.shape
    v = vt_all.reshape(bn, N_HEADS, HEAD_DIM, keys)
    ones = jnp.ones((bn, N_HEADS, VALUE_ROWS - HEAD_DIM, keys), v.dtype)
    v = jnp.concatenate([v, ones], axis=2).reshape(bn, N_HEADS * VALUE_ROWS, keys // KEY_TILE, KEY_TILE)
    return v.transpose(0, 2, 1, 3)


def _pool_kernel(u_ref, prev_ref, w_ref, scale_ref, o_ref, f_ref, a_ref, b_ref, *, t, pos0):
    n = t + POOL_PAD
    u = u_ref[0]
    f_ref[0:16, :] = jnp.zeros((16, C_WIDTH), F32)
    f_ref[16:32, :] = prev_ref[0]
    f_ref[pl.ds(POOL_PAD, t), :] = u
    s2 = f_ref[pl.ds(8, n - 8), :] + f_ref[pl.ds(7, n - 8), :]
    a_ref[pl.ds(8, n - 8), :] = s2
    s4 = a_ref[pl.ds(16, n - 16), :] + a_ref[pl.ds(14, n - 16), :]
    b_ref[pl.ds(16, n - 16), :] = s4
    s8 = b_ref[pl.ds(24, n - 24), :] + b_ref[pl.ds(20, n - 24), :]
    a_ref[pl.ds(24, n - 24), :] = s8
    s16 = a_ref[pl.ds(32, t), :] + a_ref[pl.ds(24, t), :]
    lane = lax.broadcasted_iota(jnp.int32, (t, C_WIDTH), 1)
    g0, g1, g2 = lane < POOL_GROUP, lane < 2 * POOL_GROUP, lane < 3 * POOL_GROUP
    total = jnp.where(g0, s2[24:], jnp.where(g1, s4[16:], jnp.where(g2, s8[8:], s16)))
    win = jnp.where(g0, POOL_WINDOWS[0],
                    jnp.where(g1, POOL_WINDOWS[1],
                              jnp.where(g2, POOL_WINDOWS[2], POOL_WINDOWS[3])))
    pos = pos0 + lax.broadcasted_iota(jnp.int32, (t, C_WIDTH), 0)
    cnt = jnp.minimum(pos + 1, win).astype(F32)
    pooled = total / cnt - u
    o_ref[0] = (_dot(pooled.astype(BF16), w_ref[...]) * scale_ref[...]).astype(BF16)


def _pool(u, prev16, w_blk, scale, *, pos0):
    bn, t, _ = u.shape
    kern = functools.partial(_pool_kernel, t=t, pos0=pos0)
    bmap = lambda b: (b, 0, 0)
    n = t + POOL_PAD
    return pl.pallas_call(
        kern,
        out_shape=jax.ShapeDtypeStruct((bn, t, C_WIDTH), BF16),
        grid=(bn,),
        in_specs=[pl.BlockSpec((1, t, C_WIDTH), bmap),
                  pl.BlockSpec((1, POOL_MAX, C_WIDTH), bmap),
                  _const_spec((C_WIDTH, C_WIDTH)),
                  _const_spec((1, C_WIDTH))],
        out_specs=pl.BlockSpec((1, t, C_WIDTH), bmap),
        scratch_shapes=[pltpu.VMEM((n, C_WIDTH), F32)] * 3,
        compiler_params=_params(("parallel",)),
        name="pool",
    )(u, prev16, w_blk, scale)


def _pool_weight(w_pool):
    w = jnp.zeros((C_WIDTH, C_WIDTH), w_pool.dtype)
    for g in range(len(POOL_WINDOWS)):
        sl = slice(g * POOL_GROUP, (g + 1) * POOL_GROUP)
        w = w.at[sl, sl].set(w_pool[g])
    return w.astype(BF16)


def _outmlp_kernel(ya_ref, yb_ref, yc_ref, x_ref, woa_ref, wob_ref, woc_ref,
                   g1_ref, g2_ref, g3_ref, w1_ref, w2_ref, o_ref):
    y = _dot(ya_ref[...], woa_ref[...]) + _dot(yb_ref[...], wob_ref[...])
    y = y + _dot(yc_ref[...], woc_ref[...])
    x1 = x_ref[...] + _rms(y, g1_ref[...])
    h = _rms(x1, g2_ref[...]).astype(BF16)
    m = jnp.zeros_like(x1)
    for f in range(D_FF // D_MODEL):
        sl = slice(f * D_MODEL, (f + 1) * D_MODEL)
        a = jnp.maximum(_dot(h, w1_ref[:, sl]), 0.0)
        m = m + _dot((a * a).astype(BF16), w2_ref[sl, :])
    o_ref[...] = x1 + _rms(m, g3_ref[...])


def _outmlp(ya, yb, yc, x, woa, wob, woc, g1, g2, g3, w1, w2):
    n = x.shape[0]
    assert n % ROW_TILE == 0
    row = lambda i: (i, 0)
    return pl.pallas_call(
        _outmlp_kernel,
        out_shape=jax.ShapeDtypeStruct((n, D_MODEL), F32),
        grid=(n // ROW_TILE,),
        in_specs=[pl.BlockSpec((ROW_TILE, ATT_WIDTH), row),
                  pl.BlockSpec((ROW_TILE, ATT_WIDTH), row),
                  pl.BlockSpec((ROW_TILE, C_WIDTH), row),
                  pl.BlockSpec((ROW_TILE, D_MODEL), row),
                  _const_spec((ATT_WIDTH, D_MODEL)),
                  _const_spec((ATT_WIDTH, D_MODEL)),
                  _const_spec((C_WIDTH, D_MODEL)),
                  _const_spec((1, D_MODEL)),
                  _const_spec((1, D_MODEL)),
                  _const_spec((1, D_MODEL)),
                  _const_spec((D_MODEL, D_FF)),
                  _const_spec((D_FF, D_MODEL))],
        out_specs=pl.BlockSpec((ROW_TILE, D_MODEL), row),
        compiler_params=_params(("parallel",)),
        name="outmlp",
    )(ya, yb, yc, x, woa, wob, woc, g1, g2, g3, w1, w2)


def _pad_axis(x, axis, size):
    pads = [(0, 0)] * x.ndim
    pads[axis] = (0, size - x.shape[axis])
    return jnp.pad(x, pads)


def _layer(x, weights, tri, *, seq, band_tab, band_shift, q_pos0, past=None):
    (g_pre_mix, w_proj, w_proj_tail, w_proj_t, w_pool_blk, pool_scale, woa, wob, woc, g_post_mix,
     g_pre_mlp, w1, w2, g_post_mlp) = weights
    bn = x.shape[0] // seq
    qa, ka, va, qb, kb, vb, qi, ki2, u, wit, vbt, vat = _proj(x, g_pre_mix, w_proj, w_proj_tail,
                                                              w_proj_t)
    per_batch = lambda t: t.reshape(bn, seq, t.shape[-1])
    qa, ka, va, qb, kb, vb, qi, ki2, u = map(per_batch, (qa, ka, va, qb, kb, vb, qi, ki2, u))

    if past is None:
        ka_all, kb_all, ki2_all = ka, kb, ki2
        keys_on_lanes = lambda t: t.reshape(ATT_WIDTH, bn, seq).transpose(1, 0, 2)
        vat_all, vbt_all = keys_on_lanes(vat), keys_on_lanes(vbt)
        prev16 = jnp.zeros((bn, POOL_MAX, C_WIDTH), F32)
        n_keys = seq
        tq, qa_q, qi_q, qb_q, wit_q = KEY_TILE, qa, qi, qb, wit
    else:
        c_a_k, c_a_v, c_b_k, c_b_v, c_b_kidx, c_pool = past
        flat = lambda t: t.reshape(bn, t.shape[1], ATT_WIDTH)
        keys_on_lanes = lambda t: t.astype(BF16).transpose(0, 2, 1)
        band_rows = 3 * KEY_TILE
        ka_all = _pad_axis(jnp.concatenate([flat(c_a_k), ka], axis=1), 1, band_rows)
        vat_all = keys_on_lanes(_pad_axis(jnp.concatenate([flat(c_a_v), va], axis=1), 1, band_rows))
        n_keys = c_b_k.shape[1] + seq
        key_rows = -(-n_keys // KEY_TILE) * KEY_TILE
        kb_all = _pad_axis(jnp.concatenate([flat(c_b_k), kb], axis=1), 1, key_rows)
        vbt_all = keys_on_lanes(_pad_axis(jnp.concatenate([flat(c_b_v), vb], axis=1), 1, key_rows))
        c_ki2 = jnp.concatenate([c_b_kidx, c_b_kidx], axis=-1)
        ki2_all = _pad_axis(jnp.concatenate([c_ki2, ki2], axis=1), 1, key_rows)
        prev16 = jnp.pad(c_pool, ((0, 0), (1, 0), (0, 0)))
        tq = LANES
        qa_q, qi_q, qb_q = (_pad_axis(t, 1, LANES) for t in (qa, qi, qb))
        wit_q = _pad_axis(wit.reshape(LANES, bn, seq), 2, LANES).reshape(LANES, bn * LANES)

    ya = _band(qa_q, ka_all, _value_tiles(vat_all), band_tab, tq=tq, shift=band_shift)[:, :seq]
    yb = _sparse(qi_q, wit_q, ki2_all, qb_q, kb_all, _value_tiles(vbt_all), tri,
                 tq=tq, n_real=min(seq, tq), q_pos0=q_pos0, n_keys=n_keys)[:, :seq]
    yc = _pool(u, prev16, w_pool_blk, pool_scale, pos0=q_pos0)
    flat2 = lambda t: t.reshape(bn * seq, t.shape[-1])
    x = _outmlp(flat2(ya), flat2(yb), flat2(yc), x, woa, wob, woc,
                g_post_mix, g_pre_mlp, g_post_mlp, w1, w2)
    heads = lambda t: t.reshape(bn, t.shape[1], N_HEADS, HEAD_DIM)
    return x, (heads(ka), heads(va), heads(kb), heads(vb), ki2[..., :IDX_DIM], u)


def kernel(x_prompt, x_sample, cache_a_k, cache_a_v, cache_b_k, cache_b_v, cache_b_kidx, state_pool, g_pre_mix, w_in, rel_bias, w_pool, pool_scale, w_out, g_post_mix, g_pre_mlp, w_ff1, w_ff2, g_post_mlp):
    batch, seq, _ = x_prompt.shape
    dec_batch, dec_seq, _ = x_sample.shape
    depth = w_in.shape[0]
    past_len = cache_b_k.shape[2]
    n_a = cache_a_k.shape[2]
    assert seq % KEY_TILE == 0 and (batch * seq) % ROW_TILE == 0
    assert (dec_batch * dec_seq) % ROW_TILE == 0 and n_a + dec_seq <= 3 * KEY_TILE
    assert dec_seq <= LANES

    tri = (jnp.arange(KEY_TILE)[:, None] >= jnp.arange(KEY_TILE)[None, :]).astype(BF16)
    xp = x_prompt.reshape(batch * seq, D_MODEL)
    xs = x_sample.reshape(dec_batch * dec_seq, D_MODEL)
    p_states, s_states = [], []
    row = lambda t: t.reshape(1, -1)
    for l in range(depth):
        weights = (row(g_pre_mix[l]), *_proj_weight(w_in[l]), _pool_weight(w_pool[l]),
                   row(pool_scale[l]),
                   w_out[l, :ATT_WIDTH].astype(BF16),
                   w_out[l, ATT_WIDTH:2 * ATT_WIDTH].astype(BF16),
                   w_out[l, 2 * ATT_WIDTH:].astype(BF16),
                   row(g_post_mix[l]), row(g_pre_mlp[l]),
                   w_ff1[l].astype(BF16), w_ff2[l].astype(BF16), row(g_post_mlp[l]))
        xp, sp = _layer(xp, weights, tri, seq=seq,
                        band_tab=_band_table(rel_bias[l], 2 * KEY_TILE, 0, KEY_TILE, 3 * KEY_TILE),
                        band_shift=2, q_pos0=0)
        xs, ss = _layer(xs, weights, tri, seq=dec_seq,
                        band_tab=_band_table(rel_bias[l], past_len, past_len - n_a, LANES,
                                             n_a + dec_seq),
                        band_shift=0, q_pos0=past_len,
                        past=(cache_a_k[l], cache_a_v[l], cache_b_k[l], cache_b_v[l],
                              cache_b_kidx[l], state_pool[l]))
        n_keep = min(A_WINDOW, seq)
        ka, va, kb, vb, ki, u = sp
        p_states.append((ka[:, seq - n_keep:], va[:, seq - n_keep:], kb, vb, ki,
                         u[:, seq - (POOL_MAX - 1):]))
        ka, va, kb, vb, ki, u = ss
        new_pool = jnp.concatenate([state_pool[l], u], axis=1)[:, dec_seq:]
        s_states.append((ka, va, kb, vb, ki, new_pool))
    stk = lambda states, i: jnp.stack([st[i] for st in states], axis=0)
    return ((xp.reshape(batch, seq, D_MODEL), xs.reshape(dec_batch, dec_seq, D_MODEL))
            + tuple(stk(p_states, i) for i in range(6))
            + tuple(stk(s_states, i) for i in range(6)))
```

```python
import functools

import jax
import jax.numpy as jnp
from jax import lax
from jax.experimental import pallas as pl
from jax.experimental.pallas import tpu as pltpu

D_MODEL = 1024
CHUNK = 64
CHUNK_SHIFT = 6
HEAD_DIM = 64
N_HEADS = 6
ATT_WIDTH = N_HEADS * HEAD_DIM
N_PAIRS = N_HEADS // 2
C_WIDTH = 256
A_LEFT_CHUNKS = 8
A_WINDOW = A_LEFT_CHUNKS * CHUNK
REL_CLIP = 128
IDX_HEADS = 8
IDX_DIM = 64
TOPK = 256
POOL_WINDOWS = (2, 4, 8, 16)
POOL_GROUP = 64
POOL_MAX = 16
POOL_PAD = 32
D_FF = 4 * D_MODEL
RMS_EPS = 1e-6
IDX_SCALE = IDX_HEADS ** -0.5 * IDX_DIM ** -0.5
ATT_SCALE = HEAD_DIM ** -0.5

LANES = 128
SUBLANES = 8
KEY_TILE = 256
KEY_TILE_SHIFT = 8
ROW_TILE = 512
NEG = -1e30
LOG2E = 1.4426950408889634
VALUE_ROWS = HEAD_DIM + 16
STEP_GROUP = 4
FIRST_STEPS = 24
SEARCH_STEPS = 40
VMEM_LIMIT = 48 * 1024 * 1024

_OFF_KB, _OFF_QI, _OFF_KI, _OFF_WI, _OFF_U, _OFF_END = 1536, 2304, 2816, 2880, 2888, 3144

F32 = jnp.float32
BF16 = jnp.bfloat16


def _const_spec(shape):
    zeros = (0,) * len(shape)
    return pl.BlockSpec(shape, lambda *_: zeros, pipeline_mode=pl.Buffered(1))


def _params(semantics):
    return pltpu.CompilerParams(dimension_semantics=semantics, vmem_limit_bytes=VMEM_LIMIT)


def _rms(x, g):
    ms = jnp.mean(x * x, axis=-1, keepdims=True)
    return x * lax.rsqrt(ms + RMS_EPS) * g


def _dot(a, b):
    return jnp.dot(a, b, preferred_element_type=F32)


def _dot_t(a, b):
    return lax.dot_general(a, b, (((1,), (1,)), ((), ())), preferred_element_type=F32)


def _half_masks(rows):
    lane = lax.broadcasted_iota(jnp.int32, (rows, LANES), 1)
    first = lane < HEAD_DIM
    m0 = jnp.where(first, 1.0, 0.0).astype(BF16)
    m1 = jnp.where(first, 0.0, 1.0).astype(BF16)
    return first, m0, m1


def _fold(x, op):
    parts = [x[i:i + SUBLANES] for i in range(0, x.shape[0], SUBLANES)]
    lanes = min(4, len(parts))
    acc = parts[:lanes]
    for i, part in enumerate(parts[lanes:]):
        acc[i % lanes] = op(acc[i % lanes], part)
    while len(acc) > 1:
        acc = [op(acc[i], acc[i + 1]) for i in range(0, len(acc), 2)]
    return acc[0]


def _proj_kernel(x_ref, g_ref, w_ref, wtail_ref, wt_ref, qa_ref, ka_ref, va_ref, qb_ref, kb_ref,
                 vb_ref, qi_ref, ki2_ref, u_ref, wit_ref, vbt_ref, vat_ref):
    h = _rms(x_ref[...], g_ref[...]).astype(BF16)
    z = _dot(h, w_ref[:, 0:768])
    qa_ref[...] = (z[:, 0:384] * (ATT_SCALE * LOG2E)).astype(BF16)
    ka_ref[...] = z[:, 384:768]
    z = _dot(h, w_ref[:, 768:1536])
    va_ref[...] = z[:, 0:384]
    qb_ref[...] = (z[:, 384:768] * (ATT_SCALE * LOG2E)).astype(BF16)
    z = _dot(h, w_ref[:, 1536:2304])
    kb_ref[...] = z[:, 0:384]
    vb_ref[...] = z[:, 384:768]
    qi_ref[...] = _dot(h, w_ref[:, 2304:2816]).astype(BF16)
    z = _dot(h, wtail_ref[...])
    ki2_ref[...] = z[:, 0:128]
    u_ref[...] = z[:, 128:384]
    zt = _dot_t(wt_ref[...], h)
    wit_ref[...] = zt[0:LANES]
    ones = jnp.ones((VALUE_ROWS - HEAD_DIM, KEY_TILE), BF16)
    for base, vt_ref in ((LANES, vbt_ref), (LANES + ATT_WIDTH, vat_ref)):
        for tile in range(ROW_TILE // KEY_TILE):
            lanes = slice(KEY_TILE * tile, KEY_TILE * (tile + 1))
            for hd in range(N_HEADS):
                rows = slice(base + HEAD_DIM * hd, base + HEAD_DIM * (hd + 1))
                vt_ref[0, tile, VALUE_ROWS * hd:VALUE_ROWS * hd + HEAD_DIM, :] = (
                    zt[rows, lanes].astype(BF16))
                vt_ref[0, tile, VALUE_ROWS * hd + HEAD_DIM:VALUE_ROWS * (hd + 1), :] = ones


def _proj(x, g, w, wtail, wt, *, seq_tiles):
    n = x.shape[0]
    tiles_per_step = ROW_TILE // KEY_TILE
    assert n % ROW_TILE == 0 and seq_tiles % tiles_per_step == 0
    steps_per_seq = seq_tiles // tiles_per_step
    widths = (ATT_WIDTH,) * 6 + (IDX_HEADS * IDX_DIM, LANES, C_WIDTH)
    dtypes = (BF16, F32, F32, BF16, F32, F32, BF16, F32, F32)
    row = lambda i: (i, 0)
    vt_shape = (n // (seq_tiles * KEY_TILE), seq_tiles, N_HEADS * VALUE_ROWS, KEY_TILE)
    vt_spec = pl.BlockSpec((1, tiles_per_step, N_HEADS * VALUE_ROWS, KEY_TILE),
                           lambda i: (i // steps_per_seq, i % steps_per_seq, 0, 0))
    out_shape = tuple(jax.ShapeDtypeStruct((n, wd), dt) for wd, dt in zip(widths, dtypes))
    out_shape += (jax.ShapeDtypeStruct((LANES, n), F32),)
    out_shape += (jax.ShapeDtypeStruct(vt_shape, BF16),) * 2
    out_specs = tuple(pl.BlockSpec((ROW_TILE, wd), row) for wd in widths)
    out_specs += (pl.BlockSpec((LANES, ROW_TILE), lambda i: (0, i)), vt_spec, vt_spec)
    return pl.pallas_call(
        _proj_kernel,
        out_shape=out_shape,
        grid=(n // ROW_TILE,),
        in_specs=[pl.BlockSpec((ROW_TILE, D_MODEL), row),
                  _const_spec((1, D_MODEL)),
                  _const_spec((D_MODEL, _OFF_KI)),
                  _const_spec((D_MODEL, 2 * IDX_DIM + C_WIDTH)),
                  _const_spec((LANES + 2 * ATT_WIDTH, D_MODEL))],
        out_specs=out_specs,
        compiler_params=_params(("parallel",)),
        name="proj",
    )(x, g, w, wtail, wt)


def _proj_weight(w_in):
    w_bf = w_in.astype(BF16)
    ki = w_bf[:, _OFF_KI:_OFF_WI]
    wtail = jnp.concatenate([ki, ki, w_bf[:, _OFF_U:_OFF_END]], axis=1)
    wi_t = jnp.pad(w_bf[:, _OFF_WI:_OFF_U].T, ((0, LANES - IDX_HEADS), (0, 0)))
    vb_t = w_bf[:, _OFF_KB + ATT_WIDTH:_OFF_QI].T
    va_t = w_bf[:, 2 * ATT_WIDTH:3 * ATT_WIDTH].T
    return w_bf[:, :_OFF_KI], wtail, jnp.concatenate([wi_t, vb_t, va_t], axis=0)


def _mask_heads(q_ref, qm_ref, heads, tq):
    _, m0, m1 = _half_masks(tq)
    for h in range(heads):
        pair = slice(LANES * (h // 2), LANES * (h // 2 + 1))
        qm_ref[h] = q_ref[0, :, pair] * (m0 if h % 2 == 0 else m1)


def _attend_scratch(tq):
    return [pltpu.VMEM((N_HEADS, VALUE_ROWS, tq), F32),
            pltpu.VMEM((ATT_WIDTH, tq), F32),
            pltpu.VMEM((N_HEADS, KEY_TILE, tq), F32),
            pltpu.VMEM((N_HEADS, KEY_TILE, tq), BF16)]


def _attend_init(acc_ref, tq):
    for h in range(N_HEADS):
        acc_ref[h] = jnp.zeros((VALUE_ROWS, tq), F32)
    return tuple(jnp.full((1, tq), NEG, F32) for _ in range(N_HEADS))


def _attend_tile(k_ref, vt_ref, tile, bias_of_head, qm_ref, acc_ref, s_ref, p_ref, m_run):
    start = pl.multiple_of(tile * KEY_TILE, KEY_TILE)
    m_new = []
    for pair in range(N_PAIRS):
        kt = k_ref[0, pl.ds(start, KEY_TILE), LANES * pair:LANES * (pair + 1)].astype(BF16)
        for h in (2 * pair, 2 * pair + 1):
            s = _dot_t(kt, qm_ref[h]) + bias_of_head(h)
            s_ref[h] = s
            col_max = jnp.max(_fold(s, jnp.maximum), axis=0, keepdims=True)
            m_new.append(jnp.maximum(m_run[h], col_max))
    for h in range(N_HEADS):
        p_ref[h] = jnp.exp2(s_ref[h] - m_new[h]).astype(BF16)
        acc_ref[h] = jnp.exp2(m_run[h] - m_new[h]) * acc_ref[h]
    for h in range(N_HEADS):
        vt = vt_ref[0, tile, VALUE_ROWS * h:VALUE_ROWS * (h + 1), :]
        acc_ref[h] = acc_ref[h] + _dot(vt, p_ref[h])
    return tuple(m_new)


def _attend_finish(acc_ref, out_ref, o_ref):
    for h in range(N_HEADS):
        acc = acc_ref[h]
        out_ref[HEAD_DIM * h:HEAD_DIM * (h + 1), :] = acc[:HEAD_DIM] / acc[HEAD_DIM:HEAD_DIM + 1]
    o_ref[0] = out_ref[...].T.astype(BF16)


def _band_kernel(q_ref, k_ref, vt_ref, tab_ref, o_ref, qm_ref, acc_ref, out_ref, s_ref, p_ref,
                 *, tq, shift):
    j = pl.program_id(1)
    _mask_heads(q_ref, qm_ref, N_HEADS, tq)
    m_run = _attend_init(acc_ref, tq)
    for t in range(3):
        blk = j + (t - shift)
        bias = lambda h, t=t, blk=blk: jnp.where(blk >= 0, tab_ref[h, t], NEG)
        m_run = _attend_tile(k_ref, vt_ref, jnp.maximum(blk, 0), bias, qm_ref, acc_ref, s_ref,
                             p_ref, m_run)
    _attend_finish(acc_ref, out_ref, o_ref)


def _band(q, k, vt, tab, *, tq, shift):
    bn, tq_total, _ = q.shape
    tk_total = k.shape[1]
    max_tiles = tk_total // KEY_TILE
    kern = functools.partial(_band_kernel, tq=tq, shift=shift)
    qmap = lambda b, j: (b, j, 0)
    return pl.pallas_call(
        kern,
        out_shape=jax.ShapeDtypeStruct((bn, tq_total, ATT_WIDTH), BF16),
        grid=(bn, tq_total // tq),
        in_specs=[pl.BlockSpec((1, tq, ATT_WIDTH), qmap),
                  pl.BlockSpec((1, tk_total, ATT_WIDTH), lambda b, j: (b, 0, 0)),
                  pl.BlockSpec((1, max_tiles, N_HEADS * VALUE_ROWS, KEY_TILE),
                               lambda b, j: (b, 0, 0, 0)),
                  _const_spec(tab.shape)],
        out_specs=pl.BlockSpec((1, tq, ATT_WIDTH), qmap),
        scratch_shapes=[pltpu.VMEM((N_HEADS, tq, LANES), BF16),
                        *_attend_scratch(tq)],
        compiler_params=_params(("parallel", "arbitrary")),
        name="band",
    )(q, k, vt, tab)


def _band_table(rel_bias, q_pos0, k_pos0, tq, n_real):
    tk = 3 * KEY_TILE
    span = tq + tk - 1
    diff = (q_pos0 - k_pos0) - (tk - 1) + jnp.arange(span)
    vec = jnp.take(rel_bias.astype(F32), jnp.clip(diff, -REL_CLIP, REL_CLIP) + REL_CLIP, axis=1)
    h = vec.shape[0]
    skew = jnp.tile(vec, (1, tq + 1))[:, :tq * (span + 1)].reshape(h, tq, span + 1)
    bias = jnp.flip(skew[:, :, :tk], axis=-1)
    qc = (q_pos0 + jnp.arange(tq)) // CHUNK
    kc = (k_pos0 + jnp.arange(tk)) // CHUNK
    valid = ((kc[None, :] <= qc[:, None]) & (kc[None, :] >= qc[:, None] - A_LEFT_CHUNKS)
             & (jnp.arange(tk) < n_real)[None, :])
    tab = jnp.where(valid[None], bias * LOG2E, NEG)
    return tab.reshape(h, tq, 3, KEY_TILE).transpose(0, 2, 3, 1)


def _sparse_kernel(qi_ref, wit_ref, ki2_ref, qb_ref, kb_ref, vt_ref, tri_ref, o_ref,
                   sc_ref, mb_ref, qim_ref, qbm_ref, acc_ref, out_ref, s_ref, p_ref,
                   *, tq, n_real, q_pos0, n_keys):
    j = pl.program_id(1)
    q_first = q_pos0 + j * n_real
    last_chunk = lax.shift_right_logical(q_first + (n_real - 1), CHUNK_SHIFT)
    k_end = jnp.minimum((last_chunk + 1) * CHUNK, n_keys)
    n_tiles = lax.shift_right_logical(k_end + (KEY_TILE - 1), KEY_TILE_SHIFT)

    lane = lax.broadcasted_iota(jnp.int32, (1, tq), 1)
    q_pos = q_first + lane
    k_lim = jnp.minimum((lax.shift_right_logical(q_pos, CHUNK_SHIFT) + 1) * CHUNK, n_keys)
    key_row = lax.broadcasted_iota(jnp.int32, (KEY_TILE, tq), 0)

    _mask_heads(qi_ref, qim_ref, IDX_HEADS, tq)
    _mask_heads(qb_ref, qbm_ref, N_HEADS, tq)
    w = wit_ref[0:SUBLANES, :] * IDX_SCALE

    def score_tile(t, carry):
        rmin8, rmax8 = carry
        start = pl.multiple_of(t * KEY_TILE, KEY_TILE)
        kt = ki2_ref[0, pl.ds(start, KEY_TILE), :].astype(BF16)
        sc = jnp.zeros((KEY_TILE, tq), F32)
        for h in range(IDX_HEADS):
            sc = sc + jnp.maximum(_dot_t(kt, qim_ref[h]), 0.0) * w[h:h + 1, :]
        adm = (start + key_row) < k_lim
        lowest = jnp.where(adm, sc, -jnp.inf)
        sc_ref[t] = lowest
        rmax8 = jnp.maximum(rmax8, _fold(lowest, jnp.maximum))
        rmin8 = jnp.minimum(rmin8, _fold(jnp.where(adm, sc, jnp.inf), jnp.minimum))
        return rmin8, rmax8

    rmin8, rmax8 = lax.fori_loop(
        0, n_tiles, score_tile,
        (jnp.full((SUBLANES, tq), jnp.inf, F32), jnp.full((SUBLANES, tq), -jnp.inf, F32)))
    rmin = jnp.min(rmin8, axis=0, keepdims=True)
    rmax = jnp.max(rmax8, axis=0, keepdims=True)

    def count_ge(thr):
        def body(t, c8):
            return c8 + _fold(jnp.where(sc_ref[t] >= thr, 1.0, 0.0), jnp.add)
        c8 = lax.fori_loop(0, n_tiles, body, jnp.zeros((SUBLANES, tq), F32))
        return jnp.sum(c8, axis=0, keepdims=True)

    def n_unsettled(c_lo):
        return jnp.sum(jnp.where(c_lo > TOPK, 1.0, 0.0))

    def search(carry, last_step):
        def cond(c):
            return jnp.logical_and(c[0] < last_step, c[1] > 0.0)

        def step(c):
            s, _, lo, hi, c_lo, c_hi = c
            for _ in range(STEP_GROUP):
                mid = lo + (hi - lo) * 0.5
                c_mid = count_ge(mid)
                unsettled = c_lo > TOPK
                enough = c_mid >= TOPK
                lo, c_lo = (jnp.where(unsettled, jnp.where(enough, mid, lo), lo),
                            jnp.where(unsettled, jnp.where(enough, c_mid, c_lo), c_lo))
                hi, c_hi = (jnp.where(unsettled, jnp.where(enough, hi, mid), hi),
                            jnp.where(unsettled, jnp.where(enough, c_hi, c_mid), c_hi))
            return s + STEP_GROUP, n_unsettled(c_lo), lo, hi, c_lo, c_hi

        return lax.while_loop(cond, step, carry)

    hi0 = rmax + jnp.maximum(jnp.abs(rmax), 1e-30) * (2.0 ** -10)
    c_lo0 = jnp.where(lane < n_real, k_lim, 0).astype(F32)
    steps, n_open, lo, hi, c_lo, c_hi = search(
        (jnp.int32(0), n_unsettled(c_lo0), rmin, hi0, c_lo0, jnp.zeros((1, tq), F32)),
        FIRST_STEPS)

    def n_untied():
        def body(t, carry):
            vmin8, vmax8 = carry
            x = sc_ref[t]
            vmax8 = jnp.maximum(vmax8, _fold(
                jnp.where(x >= lo, jnp.where(x < hi, x, -jnp.inf), -jnp.inf), jnp.maximum))
            vmin8 = jnp.minimum(vmin8, _fold(
                jnp.where(x >= lo, jnp.where(x < hi, x, jnp.inf), jnp.inf), jnp.minimum))
            return vmin8, vmax8
        vmin8, vmax8 = lax.fori_loop(
            0, n_tiles, body,
            (jnp.full((SUBLANES, tq), jnp.inf, F32), jnp.full((SUBLANES, tq), -jnp.inf, F32)))
        vmin = jnp.min(vmin8, axis=0, keepdims=True)
        vmax = jnp.max(vmax8, axis=0, keepdims=True)
        return jnp.sum(jnp.where(c_lo > TOPK, jnp.where(vmax == vmin, 0.0, 1.0), 0.0))

    n_open = lax.cond(n_open > 0.0, n_untied, lambda: jnp.float32(0.0))
    _, _, lo, hi, _, c_hi = search((steps, n_open, lo, hi, c_lo, c_hi), SEARCH_STEPS)

    need = TOPK - c_hi

    def mask_tile(t, run):
        x = sc_ref[t]
        inr = jnp.where(x >= lo, jnp.where(x < hi, 1.0, 0.0), 0.0)
        rank = _dot(tri_ref[...], inr.astype(BF16)) + run
        mb_ref[t] = jnp.where(
            x >= hi, 0.0, jnp.where(inr > 0.0, jnp.where(rank <= need, 0.0, NEG), NEG))
        return rank[KEY_TILE - 1:KEY_TILE, :]

    lax.fori_loop(0, n_tiles, mask_tile, jnp.zeros((1, tq), F32))

    def attend(t, m_run):
        return _attend_tile(kb_ref, vt_ref, t, lambda h: mb_ref[t], qbm_ref, acc_ref, s_ref, p_ref,
                            m_run)

    lax.fori_loop(0, n_tiles, attend, _attend_init(acc_ref, tq))
    _attend_finish(acc_ref, out_ref, o_ref)


def _sparse(qi, wit, ki2, qb, kb, vt, tri, *, tq, n_real, q_pos0, n_keys):
    bn, tq_total, _ = qb.shape
    tk_total = kb.shape[1]
    assert tk_total % KEY_TILE == 0 and n_keys <= tk_total and tq % LANES == 0
    max_tiles = tk_total // KEY_TILE
    blocks = tq_total // tq
    kern = functools.partial(_sparse_kernel, tq=tq, n_real=n_real, q_pos0=q_pos0, n_keys=n_keys)
    qmap = lambda b, j: (b, j, 0)
    kmap = lambda b, j: (b, 0, 0)
    return pl.pallas_call(
        kern,
        out_shape=jax.ShapeDtypeStruct((bn, tq_total, ATT_WIDTH), BF16),
        grid=(bn, blocks),
        in_specs=[pl.BlockSpec((1, tq, IDX_HEADS * IDX_DIM), qmap),
                  pl.BlockSpec((LANES, tq), lambda b, j: (0, b * blocks + j)),
                  pl.BlockSpec((1, tk_total, LANES), kmap),
                  pl.BlockSpec((1, tq, ATT_WIDTH), qmap),
                  pl.BlockSpec((1, tk_total, ATT_WIDTH), kmap),
                  pl.BlockSpec((1, max_tiles, N_HEADS * VALUE_ROWS, KEY_TILE),
                               lambda b, j: (b, 0, 0, 0)),
                  _const_spec((KEY_TILE, KEY_TILE))],
        out_specs=pl.BlockSpec((1, tq, ATT_WIDTH), qmap),
        scratch_shapes=[pltpu.VMEM((max_tiles, KEY_TILE, tq), F32),
                        pltpu.VMEM((max_tiles, KEY_TILE, tq), F32),
                        pltpu.VMEM((IDX_HEADS, tq, LANES), BF16),
                        pltpu.VMEM((N_HEADS, tq, LANES), BF16),
                        *_attend_scratch(tq)],
        compiler_params=_params(("parallel", "arbitrary")),
        name="sparse",
    )(qi, wit, ki2, qb, kb, vt, tri)


def _value_tiles(vt_all):
    bn, _, keys = vt_all.shape
    v = vt_all.reshape(bn, N_HEADS, HEAD_DIM, keys)
    ones = jnp.ones((bn, N_HEADS, VALUE_ROWS - HEAD_DIM, keys), v.dtype)
    v = jnp.concatenate([v, ones], axis=2).reshape(bn, N_HEADS * VALUE_ROWS, keys // KEY_TILE, KEY_TILE)
    return v.transpose(0, 2, 1, 3)


def _pool_kernel(u_ref, prev_ref, w_ref, scale_ref, o_ref, f_ref, a_ref, b_ref, *, t, pos0):
    n = t + POOL_PAD
    u = u_ref[0]
    f_ref[0:16, :] = jnp.zeros((16, C_WIDTH), F32)
    f_ref[16:32, :] = prev_ref[0]
    f_ref[pl.ds(POOL_PAD, t), :] = u
    s2 = f_ref[pl.ds(8, n - 8), :] + f_ref[pl.ds(7, n - 8), :]
    a_ref[pl.ds(8, n - 8), :] = s2
    s4 = a_ref[pl.ds(16, n - 16), :] + a_ref[pl.ds(14, n - 16), :]
    b_ref[pl.ds(16, n - 16), :] = s4
    s8 = b_ref[pl.ds(24, n - 24), :] + b_ref[pl.ds(20, n - 24), :]
    a_ref[pl.ds(24, n - 24), :] = s8
    s16 = a_ref[pl.ds(32, t), :] + a_ref[pl.ds(24, t), :]
    lane = lax.broadcasted_iota(jnp.int32, (t, C_WIDTH), 1)
    g0, g1, g2 = lane < POOL_GROUP, lane < 2 * POOL_GROUP, lane < 3 * POOL_GROUP
    total = jnp.where(g0, s2[24:], jnp.where(g1, s4[16:], jnp.where(g2, s8[8:], s16)))
    win = jnp.where(g0, POOL_WINDOWS[0],
                    jnp.where(g1, POOL_WINDOWS[1],
                              jnp.where(g2, POOL_WINDOWS[2], POOL_WINDOWS[3])))
    pos = pos0 + lax.broadcasted_iota(jnp.int32, (t, C_WIDTH), 0)
    cnt = jnp.minimum(pos + 1, win).astype(F32)
    pooled = total / cnt - u
    o_ref[0] = (_dot(pooled.astype(BF16), w_ref[...]) * scale_ref[...]).astype(BF16)


def _pool(u, prev16, w_blk, scale, *, pos0):
    bn, t, _ = u.shape
    kern = functools.partial(_pool_kernel, t=t, pos0=pos0)
    bmap = lambda b: (b, 0, 0)
    n = t + POOL_PAD
    return pl.pallas_call(
        kern,
        out_shape=jax.ShapeDtypeStruct((bn, t, C_WIDTH), BF16),
        grid=(bn,),
        in_specs=[pl.BlockSpec((1, t, C_WIDTH), bmap),
                  pl.BlockSpec((1, POOL_MAX, C_WIDTH), bmap),
                  _const_spec((C_WIDTH, C_WIDTH)),
                  _const_spec((1, C_WIDTH))],
        out_specs=pl.BlockSpec((1, t, C_WIDTH), bmap),
        scratch_shapes=[pltpu.VMEM((n, C_WIDTH), F32)] * 3,
        compiler_params=_params(("parallel",)),
        name="pool",
    )(u, prev16, w_blk, scale)


def _pool_weight(w_pool):
    w = jnp.zeros((C_WIDTH, C_WIDTH), w_pool.dtype)
    for g in range(len(POOL_WINDOWS)):
        sl = slice(g * POOL_GROUP, (g + 1) * POOL_GROUP)
        w = w.at[sl, sl].set(w_pool[g])
    return w.astype(BF16)


def _outmlp_kernel(ya_ref, yb_ref, yc_ref, x_ref, woa_ref, wob_ref, woc_ref,
                   g1_ref, g2_ref, g3_ref, w1_ref, w2_ref, o_ref):
    y = _dot(ya_ref[...], woa_ref[...]) + _dot(yb_ref[...], wob_ref[...])
    y = y + _dot(yc_ref[...], woc_ref[...])
    x1 = x_ref[...] + _rms(y, g1_ref[...])
    h = _rms(x1, g2_ref[...]).astype(BF16)
    m = jnp.zeros_like(x1)
    for f in range(D_FF // D_MODEL):
        sl = slice(f * D_MODEL, (f + 1) * D_MODEL)
        a = jnp.maximum(_dot(h, w1_ref[:, sl]), 0.0)
        m = m + _dot((a * a).astype(BF16), w2_ref[sl, :])
    o_ref[...] = x1 + _rms(m, g3_ref[...])


def _outmlp(ya, yb, yc, x, woa, wob, woc, g1, g2, g3, w1, w2):
    n = x.shape[0]
    assert n % ROW_TILE == 0
    row = lambda i: (i, 0)
    return pl.pallas_call(
        _outmlp_kernel,
        out_shape=jax.ShapeDtypeStruct((n, D_MODEL), F32),
        grid=(n // ROW_TILE,),
        in_specs=[pl.BlockSpec((ROW_TILE, ATT_WIDTH), row),
                  pl.BlockSpec((ROW_TILE, ATT_WIDTH), row),
                  pl.BlockSpec((ROW_TILE, C_WIDTH), row),
                  pl.BlockSpec((ROW_TILE, D_MODEL), row),
                  _const_spec((ATT_WIDTH, D_MODEL)),
                  _const_spec((ATT_WIDTH, D_MODEL)),
                  _const_spec((C_WIDTH, D_MODEL)),
                  _const_spec((1, D_MODEL)),
                  _const_spec((1, D_MODEL)),
                  _const_spec((1, D_MODEL)),
                  _const_spec((D_MODEL, D_FF)),
                  _const_spec((D_FF, D_MODEL))],
        out_specs=pl.BlockSpec((ROW_TILE, D_MODEL), row),
        compiler_params=_params(("parallel",)),
        name="outmlp",
    )(ya, yb, yc, x, woa, wob, woc, g1, g2, g3, w1, w2)


def _pad_axis(x, axis, size):
    pads = [(0, 0)] * x.ndim
    pads[axis] = (0, size - x.shape[axis])
    return jnp.pad(x, pads)


def _layer(x, weights, tri, *, seq, band_tab, band_shift, q_pos0, past=None):
    (g_pre_mix, w_proj, w_proj_tail, w_proj_t, w_pool_blk, pool_scale, woa, wob, woc, g_post_mix,
     g_pre_mlp, w1, w2, g_post_mlp) = weights
    bn = x.shape[0] // seq
    seq_tiles = seq // KEY_TILE if past is None else ROW_TILE // KEY_TILE
    qa, ka, va, qb, kb, vb, qi, ki2, u, wit, vbt, vat = _proj(
        x, g_pre_mix, w_proj, w_proj_tail, w_proj_t, seq_tiles=seq_tiles)
    per_batch = lambda t: t.reshape(bn, seq, t.shape[-1])
    qa, ka, va, qb, kb, vb, qi, ki2, u = map(per_batch, (qa, ka, va, qb, kb, vb, qi, ki2, u))

    if past is None:
        ka_all, kb_all, ki2_all = ka, kb, ki2
        va_tiles, vb_tiles = vat, vbt
        prev16 = jnp.zeros((bn, POOL_MAX, C_WIDTH), F32)
        n_keys = seq
        tq, qa_q, qi_q, qb_q, wit_q = KEY_TILE, qa, qi, qb, wit
    else:
        c_a_k, c_a_v, c_b_k, c_b_v, c_b_kidx, c_pool = past
        flat = lambda t: t.reshape(bn, t.shape[1], ATT_WIDTH)
        tiles = lambda t: _value_tiles(t.astype(BF16).transpose(0, 2, 1))
        band_rows = 3 * KEY_TILE
        ka_all = _pad_axis(jnp.concatenate([flat(c_a_k), ka], axis=1), 1, band_rows)
        va_tiles = tiles(_pad_axis(jnp.concatenate([flat(c_a_v), va], axis=1), 1, band_rows))
        n_keys = c_b_k.shape[1] + seq
        key_rows = -(-n_keys // KEY_TILE) * KEY_TILE
        kb_all = _pad_axis(jnp.concatenate([flat(c_b_k), kb], axis=1), 1, key_rows)
        vb_tiles = tiles(_pad_axis(jnp.concatenate([flat(c_b_v), vb], axis=1), 1, key_rows))
        c_ki2 = jnp.concatenate([c_b_kidx, c_b_kidx], axis=-1)
        ki2_all = _pad_axis(jnp.concatenate([c_ki2, ki2], axis=1), 1, key_rows)
        prev16 = jnp.pad(c_pool, ((0, 0), (1, 0), (0, 0)))
        tq = LANES
        qa_q, qi_q, qb_q = (_pad_axis(t, 1, LANES) for t in (qa, qi, qb))
        wit_q = _pad_axis(wit.reshape(LANES, bn, seq), 2, LANES).reshape(LANES, bn * LANES)

    ya = _band(qa_q, ka_all, va_tiles, band_tab, tq=tq, shift=band_shift)[:, :seq]
    yb = _sparse(qi_q, wit_q, ki2_all, qb_q, kb_all, vb_tiles, tri,
                 tq=tq, n_real=min(seq, tq), q_pos0=q_pos0, n_keys=n_keys)[:, :seq]
    yc = _pool(u, prev16, w_pool_blk, pool_scale, pos0=q_pos0)
    flat2 = lambda t: t.reshape(bn * seq, t.shape[-1])
    x = _outmlp(flat2(ya), flat2(yb), flat2(yc), x, woa, wob, woc,
                g_post_mix, g_pre_mlp, g_post_mlp, w1, w2)
    heads = lambda t: t.reshape(bn, t.shape[1], N_HEADS, HEAD_DIM)
    return x, (heads(ka), heads(va), heads(kb), heads(vb), ki2[..., :IDX_DIM], u)


def kernel(x_prompt, x_sample, cache_a_k, cache_a_v, cache_b_k, cache_b_v, cache_b_kidx, state_pool, g_pre_mix, w_in, rel_bias, w_pool, pool_scale, w_out, g_post_mix, g_pre_mlp, w_ff1, w_ff2, g_post_mlp):
    batch, seq, _ = x_prompt.shape
    dec_batch, dec_seq, _ = x_sample.shape
    depth = w_in.shape[0]
    past_len = cache_b_k.shape[2]
    n_a = cache_a_k.shape[2]
    assert seq % KEY_TILE == 0 and (batch * seq) % ROW_TILE == 0
    assert (dec_batch * dec_seq) % ROW_TILE == 0 and n_a + dec_seq <= 3 * KEY_TILE
    assert dec_seq <= LANES

    tri = (jnp.arange(KEY_TILE)[:, None] >= jnp.arange(KEY_TILE)[None, :]).astype(BF16)
    xp = x_prompt.reshape(batch * seq, D_MODEL)
    xs = x_sample.reshape(dec_batch * dec_seq, D_MODEL)
    p_states, s_states = [], []
    row = lambda t: t.reshape(1, -1)
    for l in range(depth):
        weights = (row(g_pre_mix[l]), *_proj_weight(w_in[l]), _pool_weight(w_pool[l]),
                   row(pool_scale[l]),
                   w_out[l, :ATT_WIDTH].astype(BF16),
                   w_out[l, ATT_WIDTH:2 * ATT_WIDTH].astype(BF16),
                   w_out[l, 2 * ATT_WIDTH:].astype(BF16),
                   row(g_post_mix[l]), row(g_pre_mlp[l]),
                   w_ff1[l].astype(BF16), w_ff2[l].astype(BF16), row(g_post_mlp[l]))
        xp, sp = _layer(xp, weights, tri, seq=seq,
                        band_tab=_band_table(rel_bias[l], 2 * KEY_TILE, 0, KEY_TILE, 3 * KEY_TILE),
                        band_shift=2, q_pos0=0)
        xs, ss = _layer(xs, weights, tri, seq=dec_seq,
                        band_tab=_band_table(rel_bias[l], past_len, past_len - n_a, LANES,
                                             n_a + dec_seq),
                        band_shift=0, q_pos0=past_len,
                        past=(cache_a_k[l], cache_a_v[l], cache_b_k[l], cache_b_v[l],
                              cache_b_kidx[l], state_pool[l]))
        n_keep = min(A_WINDOW, seq)
        ka, va, kb, vb, ki, u = sp
        p_states.append((ka[:, seq - n_keep:], va[:, seq - n_keep:], kb, vb, ki,
                         u[:, seq - (POOL_MAX - 1):]))
        ka, va, kb, vb, ki, u = ss
        new_pool = jnp.concatenate([state_pool[l], u], axis=1)[:, dec_seq:]
        s_states.append((ka, va, kb, vb, ki, new_pool))
    stk = lambda states, i: jnp.stack([st[i] for st in states], axis=0)
    return ((xp.reshape(batch, seq, D_MODEL), xs.reshape(dec_batch, dec_seq, D_MODEL))
            + tuple(stk(p_states, i) for i in range(6))
            + tuple(stk(s_states, i) for i in range(6)))
```

```python
import functools

import jax
import jax.numpy as jnp
from jax import lax
from jax.experimental import pallas as pl
from jax.experimental.pallas import tpu as pltpu

D_MODEL = 1024
CHUNK = 64
CHUNK_SHIFT = 6
HEAD_DIM = 64
N_HEADS = 6
ATT_WIDTH = N_HEADS * HEAD_DIM
N_PAIRS = N_HEADS // 2
C_WIDTH = 256
A_LEFT_CHUNKS = 8
A_WINDOW = A_LEFT_CHUNKS * CHUNK
REL_CLIP = 128
IDX_HEADS = 8
IDX_DIM = 64
TOPK = 256
POOL_WINDOWS = (2, 4, 8, 16)
POOL_GROUP = 64
POOL_MAX = 16
POOL_PAD = 32
D_FF = 4 * D_MODEL
RMS_EPS = 1e-6
IDX_SCALE = IDX_HEADS ** -0.5 * IDX_DIM ** -0.5
ATT_SCALE = HEAD_DIM ** -0.5

LANES = 128
SUBLANES = 8
KEY_TILE = 256
KEY_TILE_SHIFT = 8
ROW_TILE = 512
NEG = -1e30
LOG2E = 1.4426950408889634
VALUE_ROWS = HEAD_DIM + 16
STEP_GROUP = 4
FIRST_STEPS = 24
SEARCH_STEPS = 40
VMEM_LIMIT = 48 * 1024 * 1024

_OFF_KB, _OFF_QI, _OFF_KI, _OFF_WI, _OFF_U, _OFF_END = 1536, 2304, 2816, 2880, 2888, 3144

F32 = jnp.float32
BF16 = jnp.bfloat16


def _const_spec(shape):
    zeros = (0,) * len(shape)
    return pl.BlockSpec(shape, lambda *_: zeros, pipeline_mode=pl.Buffered(1))


def _params(semantics):
    return pltpu.CompilerParams(dimension_semantics=semantics, vmem_limit_bytes=VMEM_LIMIT)


def _rms(x, g):
    ms = jnp.mean(x * x, axis=-1, keepdims=True)
    return x * lax.rsqrt(ms + RMS_EPS) * g


def _dot(a, b):
    return jnp.dot(a, b, preferred_element_type=F32)


def _dot_t(a, b):
    return lax.dot_general(a, b, (((1,), (1,)), ((), ())), preferred_element_type=F32)


def _half_masks(rows):
    lane = lax.broadcasted_iota(jnp.int32, (rows, LANES), 1)
    first = lane < HEAD_DIM
    m0 = jnp.where(first, 1.0, 0.0).astype(BF16)
    m1 = jnp.where(first, 0.0, 1.0).astype(BF16)
    return first, m0, m1


def _fold(x, op):
    parts = [x[i:i + SUBLANES] for i in range(0, x.shape[0], SUBLANES)]
    lanes = min(4, len(parts))
    acc = parts[:lanes]
    for i, part in enumerate(parts[lanes:]):
        acc[i % lanes] = op(acc[i % lanes], part)
    while len(acc) > 1:
        acc = [op(acc[i], acc[i + 1]) for i in range(0, len(acc), 2)]
    return acc[0]


def _proj_kernel(x_ref, g_ref, w_ref, wtail_ref, wt_ref, qa_ref, qb_ref, qi_ref, u_ref, wit_ref,
                 vbt_ref, vat_ref, ka_ref, va_ref, kb_ref, vb_ref, ki2_ref):
    def put(ref, val):
        ref[0] = val
        if ref.shape[0] > 1:
            ref[1:] = jnp.zeros((ref.shape[0] - 1,) + val.shape, val.dtype)

    h = _rms(x_ref[...], g_ref[...]).astype(BF16)
    z = _dot(h, w_ref[:, 0:768])
    qa_ref[...] = (z[:, 0:384] * (ATT_SCALE * LOG2E)).astype(BF16)
    put(ka_ref, z[:, 384:768])
    z = _dot(h, w_ref[:, 768:1536])
    put(va_ref, z[:, 0:384])
    qb_ref[...] = (z[:, 384:768] * (ATT_SCALE * LOG2E)).astype(BF16)
    z = _dot(h, w_ref[:, 1536:2304])
    put(kb_ref, z[:, 0:384])
    put(vb_ref, z[:, 384:768])
    qi_ref[...] = _dot(h, w_ref[:, 2304:2816]).astype(BF16)
    z = _dot(h, wtail_ref[...])
    put(ki2_ref, z[:, 0:128])
    u_ref[...] = z[:, 128:384]
    zt = _dot_t(wt_ref[...], h)
    wit_ref[...] = zt[0:LANES]
    ones = jnp.ones((VALUE_ROWS - HEAD_DIM, KEY_TILE), BF16)
    for base, vt_ref in ((LANES, vbt_ref), (LANES + ATT_WIDTH, vat_ref)):
        for tile in range(ROW_TILE // KEY_TILE):
            lanes = slice(KEY_TILE * tile, KEY_TILE * (tile + 1))
            for hd in range(N_HEADS):
                rows = slice(base + HEAD_DIM * hd, base + HEAD_DIM * (hd + 1))
                vt_ref[0, tile, VALUE_ROWS * hd:VALUE_ROWS * hd + HEAD_DIM, :] = (
                    zt[rows, lanes].astype(BF16))
                vt_ref[0, tile, VALUE_ROWS * hd + HEAD_DIM:VALUE_ROWS * (hd + 1), :] = ones


N_PROJ_IN = 5
N_PROJ_PLAIN = 7
STACKED_WIDTHS = (ATT_WIDTH,) * 4 + (LANES,)


def _proj(x, g, w, wtail, wt, stacked, *, seq_tiles, layer, depth):
    n = x.shape[0]
    tiles_per_step = ROW_TILE // KEY_TILE
    assert n % ROW_TILE == 0 and seq_tiles % tiles_per_step == 0
    steps_per_seq = seq_tiles // tiles_per_step
    row = lambda i: (i, 0)
    vt_shape = (n // (seq_tiles * KEY_TILE), seq_tiles, N_HEADS * VALUE_ROWS, KEY_TILE)
    vt_spec = pl.BlockSpec((1, tiles_per_step, N_HEADS * VALUE_ROWS, KEY_TILE),
                           lambda i: (i // steps_per_seq, i % steps_per_seq, 0, 0))
    plain = ((ATT_WIDTH, BF16), (ATT_WIDTH, BF16), (IDX_HEADS * IDX_DIM, BF16), (C_WIDTH, F32))
    out_shape = [jax.ShapeDtypeStruct((n, wd), dt) for wd, dt in plain]
    out_specs = [pl.BlockSpec((ROW_TILE, wd), row) for wd, _ in plain]
    out_shape += [jax.ShapeDtypeStruct((LANES, n), F32)] + [jax.ShapeDtypeStruct(vt_shape, BF16)] * 2
    out_specs += [pl.BlockSpec((LANES, ROW_TILE), lambda i: (0, i)), vt_spec, vt_spec]
    assert len(out_shape) == N_PROJ_PLAIN
    out_shape += [jax.ShapeDtypeStruct((depth, n, wd), F32) for wd in STACKED_WIDTHS]
    in_specs = [pl.BlockSpec((ROW_TILE, D_MODEL), row),
                _const_spec((1, D_MODEL)),
                _const_spec((D_MODEL, _OFF_KI)),
                _const_spec((D_MODEL, 2 * IDX_DIM + C_WIDTH)),
                _const_spec((LANES + 2 * ATT_WIDTH, D_MODEL))]
    if stacked is None:
        assert layer == 0
        operands, aliases, kern = (), {}, _proj_kernel
        out_specs += [pl.BlockSpec((depth, ROW_TILE, wd), lambda i: (0, i, 0))
                      for wd in STACKED_WIDTHS]
    else:
        operands = tuple(stacked)
        in_specs += [pl.BlockSpec(memory_space=pl.ANY)] * len(operands)
        aliases = {N_PROJ_IN + k: N_PROJ_PLAIN + k for k in range(len(operands))}
        out_specs += [pl.BlockSpec((1, ROW_TILE, wd), lambda i: (layer, i, 0))
                      for wd in STACKED_WIDTHS]

        def kern(*refs):
            _proj_kernel(*refs[:N_PROJ_IN], *refs[N_PROJ_IN + len(operands):])

    outs = pl.pallas_call(
        kern,
        out_shape=tuple(out_shape),
        grid=(n // ROW_TILE,),
        in_specs=in_specs,
        out_specs=tuple(out_specs),
        input_output_aliases=aliases,
        compiler_params=_params(("parallel",)),
        name="proj",
    )(x, g, w, wtail, wt, *operands)
    return outs[:N_PROJ_PLAIN], outs[N_PROJ_PLAIN:]


def _proj_weight(w_in):
    w_bf = w_in.astype(BF16)
    ki = w_bf[:, _OFF_KI:_OFF_WI]
    wtail = jnp.concatenate([ki, ki, w_bf[:, _OFF_U:_OFF_END]], axis=1)
    wi_t = jnp.pad(w_bf[:, _OFF_WI:_OFF_U].T, ((0, LANES - IDX_HEADS), (0, 0)))
    vb_t = w_bf[:, _OFF_KB + ATT_WIDTH:_OFF_QI].T
    va_t = w_bf[:, 2 * ATT_WIDTH:3 * ATT_WIDTH].T
    return w_bf[:, :_OFF_KI], wtail, jnp.concatenate([wi_t, vb_t, va_t], axis=0)


def _mask_heads(q_ref, qm_ref, heads, tq):
    _, m0, m1 = _half_masks(tq)
    for h in range(heads):
        pair = slice(LANES * (h // 2), LANES * (h // 2 + 1))
        qm_ref[h] = q_ref[0, :, pair] * (m0 if h % 2 == 0 else m1)


def _attend_scratch(tq):
    return [pltpu.VMEM((N_HEADS, VALUE_ROWS, tq), F32),
            pltpu.VMEM((ATT_WIDTH, tq), F32),
            pltpu.VMEM((N_HEADS, KEY_TILE, tq), F32),
            pltpu.VMEM((N_HEADS, KEY_TILE, tq), BF16)]


def _attend_init(acc_ref, tq):
    for h in range(N_HEADS):
        acc_ref[h] = jnp.zeros((VALUE_ROWS, tq), F32)
    return tuple(jnp.full((1, tq), NEG, F32) for _ in range(N_HEADS))


def _attend_tile(k_ref, vt_ref, tile, bias_of_head, qm_ref, acc_ref, s_ref, p_ref, m_run):
    start = pl.multiple_of(tile * KEY_TILE, KEY_TILE)
    m_new = []
    for pair in range(N_PAIRS):
        kt = k_ref[0, pl.ds(start, KEY_TILE), LANES * pair:LANES * (pair + 1)].astype(BF16)
        for h in (2 * pair, 2 * pair + 1):
            s = _dot_t(kt, qm_ref[h]) + bias_of_head(h)
            s_ref[h] = s
            col_max = jnp.max(_fold(s, jnp.maximum), axis=0, keepdims=True)
            m_new.append(jnp.maximum(m_run[h], col_max))
    for h in range(N_HEADS):
        p_ref[h] = jnp.exp2(s_ref[h] - m_new[h]).astype(BF16)
        acc_ref[h] = jnp.exp2(m_run[h] - m_new[h]) * acc_ref[h]
    for h in range(N_HEADS):
        vt = vt_ref[0, tile, VALUE_ROWS * h:VALUE_ROWS * (h + 1), :]
        acc_ref[h] = acc_ref[h] + _dot(vt, p_ref[h])
    return tuple(m_new)


def _attend_finish(acc_ref, out_ref, o_ref):
    for h in range(N_HEADS):
        acc = acc_ref[h]
        out_ref[HEAD_DIM * h:HEAD_DIM * (h + 1), :] = acc[:HEAD_DIM] / acc[HEAD_DIM:HEAD_DIM + 1]
    o_ref[0] = out_ref[...].T.astype(BF16)


def _band_kernel(q_ref, k_ref, vt_ref, tab_ref, o_ref, qm_ref, acc_ref, out_ref, s_ref, p_ref,
                 *, tq, shift):
    j = pl.program_id(1)
    _mask_heads(q_ref, qm_ref, N_HEADS, tq)
    m_run = _attend_init(acc_ref, tq)
    for t in range(3):
        blk = j + (t - shift)
        bias = lambda h, t=t, blk=blk: jnp.where(blk >= 0, tab_ref[h, t], NEG)
        m_run = _attend_tile(k_ref, vt_ref, jnp.maximum(blk, 0), bias, qm_ref, acc_ref, s_ref,
                             p_ref, m_run)
    _attend_finish(acc_ref, out_ref, o_ref)


def _band(q, k, vt, tab, *, tq, shift, k_batch0=0):
    bn, tq_total, _ = q.shape
    tk_total = k.shape[1]
    max_tiles = tk_total // KEY_TILE
    kern = functools.partial(_band_kernel, tq=tq, shift=shift)
    qmap = lambda b, j: (b, j, 0)
    return pl.pallas_call(
        kern,
        out_shape=jax.ShapeDtypeStruct((bn, tq_total, ATT_WIDTH), BF16),
        grid=(bn, tq_total // tq),
        in_specs=[pl.BlockSpec((1, tq, ATT_WIDTH), qmap),
                  pl.BlockSpec((1, tk_total, ATT_WIDTH), lambda b, j: (k_batch0 + b, 0, 0)),
                  pl.BlockSpec((1, max_tiles, N_HEADS * VALUE_ROWS, KEY_TILE),
                               lambda b, j: (b, 0, 0, 0)),
                  _const_spec(tab.shape)],
        out_specs=pl.BlockSpec((1, tq, ATT_WIDTH), qmap),
        scratch_shapes=[pltpu.VMEM((N_HEADS, tq, LANES), BF16),
                        *_attend_scratch(tq)],
        compiler_params=_params(("parallel", "arbitrary")),
        name="band",
    )(q, k, vt, tab)


def _band_table(rel_bias, q_pos0, k_pos0, tq, n_real):
    tk = 3 * KEY_TILE
    span = tq + tk
    diff = (q_pos0 - k_pos0) - (tk - 1) + jnp.arange(span)
    vec = jnp.take(rel_bias.astype(F32), jnp.clip(diff, -REL_CLIP, REL_CLIP) + REL_CLIP, axis=1)
    h = vec.shape[0]
    skew = jnp.tile(vec, (1, tk))[:, :tk * (span - 1)].reshape(h, tk, span - 1)
    bias = skew[:, :, tk - 1:tk - 1 + tq]
    qc = (q_pos0 + jnp.arange(tq)) // CHUNK
    kc = (k_pos0 + jnp.arange(tk)) // CHUNK
    valid = ((kc[:, None] <= qc[None, :]) & (kc[:, None] >= qc[None, :] - A_LEFT_CHUNKS)
             & (jnp.arange(tk) < n_real)[:, None])
    tab = jnp.where(valid[None], bias * LOG2E, NEG)
    return tab.reshape(h, 3, KEY_TILE, tq)


def _sparse_kernel(qi_ref, wit_ref, ki2_ref, qb_ref, kb_ref, vt_ref, tri_ref, o_ref,
                   sc_ref, mb_ref, qim_ref, qbm_ref, acc_ref, out_ref, s_ref, p_ref,
                   *, tq, n_real, q_pos0, n_keys):
    j = pl.program_id(1)
    q_first = q_pos0 + j * n_real
    last_chunk = lax.shift_right_logical(q_first + (n_real - 1), CHUNK_SHIFT)
    k_end = jnp.minimum((last_chunk + 1) * CHUNK, n_keys)
    n_tiles = lax.shift_right_logical(k_end + (KEY_TILE - 1), KEY_TILE_SHIFT)

    lane = lax.broadcasted_iota(jnp.int32, (1, tq), 1)
    q_pos = q_first + lane
    k_lim = jnp.minimum((lax.shift_right_logical(q_pos, CHUNK_SHIFT) + 1) * CHUNK, n_keys)
    key_row = lax.broadcasted_iota(jnp.int32, (KEY_TILE, tq), 0)

    _mask_heads(qi_ref, qim_ref, IDX_HEADS, tq)
    _mask_heads(qb_ref, qbm_ref, N_HEADS, tq)
    w = wit_ref[0:SUBLANES, :] * IDX_SCALE

    def score_tile(t, carry):
        rmin8, rmax8 = carry
        start = pl.multiple_of(t * KEY_TILE, KEY_TILE)
        kt = ki2_ref[0, pl.ds(start, KEY_TILE), :].astype(BF16)
        sc = jnp.zeros((KEY_TILE, tq), F32)
        for h in range(IDX_HEADS):
            sc = sc + jnp.maximum(_dot_t(kt, qim_ref[h]), 0.0) * w[h:h + 1, :]
        adm = (start + key_row) < k_lim
        lowest = jnp.where(adm, sc, -jnp.inf)
        sc_ref[t] = lowest
        rmax8 = jnp.maximum(rmax8, _fold(lowest, jnp.maximum))
        rmin8 = jnp.minimum(rmin8, _fold(jnp.where(adm, sc, jnp.inf), jnp.minimum))
        return rmin8, rmax8

    rmin8, rmax8 = lax.fori_loop(
        0, n_tiles, score_tile,
        (jnp.full((SUBLANES, tq), jnp.inf, F32), jnp.full((SUBLANES, tq), -jnp.inf, F32)))
    rmin = jnp.min(rmin8, axis=0, keepdims=True)
    rmax = jnp.max(rmax8, axis=0, keepdims=True)

    def count_ge(thr):
        def body(t, c8):
            return c8 + _fold(jnp.where(sc_ref[t] >= thr, 1.0, 0.0), jnp.add)
        c8 = lax.fori_loop(0, n_tiles, body, jnp.zeros((SUBLANES, tq), F32))
        return jnp.sum(c8, axis=0, keepdims=True)

    def n_unsettled(c_lo):
        return jnp.sum(jnp.where(c_lo > TOPK, 1.0, 0.0))

    def search(carry, last_step):
        def cond(c):
            return jnp.logical_and(c[0] < last_step, c[1] > 0.0)

        def step(c):
            s, _, lo, hi, c_lo, c_hi = c
            for _ in range(STEP_GROUP):
                mid = lo + (hi - lo) * 0.5
                c_mid = count_ge(mid)
                unsettled = c_lo > TOPK
                enough = c_mid >= TOPK
                lo, c_lo = (jnp.where(unsettled, jnp.where(enough, mid, lo), lo),
                            jnp.where(unsettled, jnp.where(enough, c_mid, c_lo), c_lo))
                hi, c_hi = (jnp.where(unsettled, jnp.where(enough, hi, mid), hi),
                            jnp.where(unsettled, jnp.where(enough, c_hi, c_mid), c_hi))
            return s + STEP_GROUP, n_unsettled(c_lo), lo, hi, c_lo, c_hi

        return lax.while_loop(cond, step, carry)

    hi0 = rmax + jnp.maximum(jnp.abs(rmax), 1e-30) * (2.0 ** -10)
    c_lo0 = jnp.where(lane < n_real, k_lim, 0).astype(F32)
    steps, n_open, lo, hi, c_lo, c_hi = search(
        (jnp.int32(0), n_unsettled(c_lo0), rmin, hi0, c_lo0, jnp.zeros((1, tq), F32)),
        FIRST_STEPS)

    def n_untied():
        def body(t, carry):
            vmin8, vmax8 = carry
            x = sc_ref[t]
            vmax8 = jnp.maximum(vmax8, _fold(
                jnp.where(x >= lo, jnp.where(x < hi, x, -jnp.inf), -jnp.inf), jnp.maximum))
            vmin8 = jnp.minimum(vmin8, _fold(
                jnp.where(x >= lo, jnp.where(x < hi, x, jnp.inf), jnp.inf), jnp.minimum))
            return vmin8, vmax8
        vmin8, vmax8 = lax.fori_loop(
            0, n_tiles, body,
            (jnp.full((SUBLANES, tq), jnp.inf, F32), jnp.full((SUBLANES, tq), -jnp.inf, F32)))
        vmin = jnp.min(vmin8, axis=0, keepdims=True)
        vmax = jnp.max(vmax8, axis=0, keepdims=True)
        return jnp.sum(jnp.where(c_lo > TOPK, jnp.where(vmax == vmin, 0.0, 1.0), 0.0))

    n_open = lax.cond(n_open > 0.0, n_untied, lambda: jnp.float32(0.0))
    _, _, lo, hi, _, c_hi = search((steps, n_open, lo, hi, c_lo, c_hi), SEARCH_STEPS)

    need = TOPK - c_hi

    def mask_tile(t, run):
        x = sc_ref[t]
        inr = jnp.where(x >= lo, jnp.where(x < hi, 1.0, 0.0), 0.0)
        rank = _dot(tri_ref[...], inr.astype(BF16)) + run
        mb_ref[t] = jnp.where(
            x >= hi, 0.0, jnp.where(inr > 0.0, jnp.where(rank <= need, 0.0, NEG), NEG))
        return rank[KEY_TILE - 1:KEY_TILE, :]

    lax.fori_loop(0, n_tiles, mask_tile, jnp.zeros((1, tq), F32))

    def attend(t, m_run):
        return _attend_tile(kb_ref, vt_ref, t, lambda h: mb_ref[t], qbm_ref, acc_ref, s_ref, p_ref,
                            m_run)

    lax.fori_loop(0, n_tiles, attend, _attend_init(acc_ref, tq))
    _attend_finish(acc_ref, out_ref, o_ref)


def _sparse(qi, wit, ki2, qb, kb, vt, tri, *, tq, n_real, q_pos0, n_keys, k_batch0=0):
    bn, tq_total, _ = qb.shape
    tk_total = kb.shape[1]
    assert tk_total % KEY_TILE == 0 and n_keys <= tk_total and tq % LANES == 0
    max_tiles = tk_total // KEY_TILE
    blocks = tq_total // tq
    kern = functools.partial(_sparse_kernel, tq=tq, n_real=n_real, q_pos0=q_pos0, n_keys=n_keys)
    qmap = lambda b, j: (b, j, 0)
    kmap = lambda b, j: (k_batch0 + b, 0, 0)
    return pl.pallas_call(
        kern,
        out_shape=jax.ShapeDtypeStruct((bn, tq_total, ATT_WIDTH), BF16),
        grid=(bn, blocks),
        in_specs=[pl.BlockSpec((1, tq, IDX_HEADS * IDX_DIM), qmap),
                  pl.BlockSpec((LANES, tq), lambda b, j: (0, b * blocks + j)),
                  pl.BlockSpec((1, tk_total, LANES), kmap),
                  pl.BlockSpec((1, tq, ATT_WIDTH), qmap),
                  pl.BlockSpec((1, tk_total, ATT_WIDTH), kmap),
                  pl.BlockSpec((1, max_tiles, N_HEADS * VALUE_ROWS, KEY_TILE),
                               lambda b, j: (b, 0, 0, 0)),
                  _const_spec((KEY_TILE, KEY_TILE))],
        out_specs=pl.BlockSpec((1, tq, ATT_WIDTH), qmap),
        scratch_shapes=[pltpu.VMEM((max_tiles, KEY_TILE, tq), F32),
                        pltpu.VMEM((max_tiles, KEY_TILE, tq), F32),
                        pltpu.VMEM((IDX_HEADS, tq, LANES), BF16),
                        pltpu.VMEM((N_HEADS, tq, LANES), BF16),
                        *_attend_scratch(tq)],
        compiler_params=_params(("parallel", "arbitrary")),
        name="sparse",
    )(qi, wit, ki2, qb, kb, vt, tri)


def _value_tiles(v):
    bn, keys, _ = v.shape
    v = v.astype(BF16).reshape(bn, keys // KEY_TILE, KEY_TILE, N_HEADS, HEAD_DIM)
    v = jnp.pad(v.transpose(0, 1, 3, 4, 2),
                ((0, 0), (0, 0), (0, 0), (0, VALUE_ROWS - HEAD_DIM), (0, 0)), constant_values=1)
    return v.reshape(bn, keys // KEY_TILE, N_HEADS * VALUE_ROWS, KEY_TILE)


def _pool_kernel(u_ref, prev_ref, w_ref, scale_ref, o_ref, f_ref, a_ref, b_ref, *, t, pos0):
    n = t + POOL_PAD
    u = u_ref[0]
    f_ref[0:16, :] = jnp.zeros((16, C_WIDTH), F32)
    f_ref[16:32, :] = prev_ref[0]
    f_ref[pl.ds(POOL_PAD, t), :] = u
    s2 = f_ref[pl.ds(8, n - 8), :] + f_ref[pl.ds(7, n - 8), :]
    a_ref[pl.ds(8, n - 8), :] = s2
    s4 = a_ref[pl.ds(16, n - 16), :] + a_ref[pl.ds(14, n - 16), :]
    b_ref[pl.ds(16, n - 16), :] = s4
    s8 = b_ref[pl.ds(24, n - 24), :] + b_ref[pl.ds(20, n - 24), :]
    a_ref[pl.ds(24, n - 24), :] = s8
    s16 = a_ref[pl.ds(32, t), :] + a_ref[pl.ds(24, t), :]
    lane = lax.broadcasted_iota(jnp.int32, (t, C_WIDTH), 1)
    g0, g1, g2 = lane < POOL_GROUP, lane < 2 * POOL_GROUP, lane < 3 * POOL_GROUP
    total = jnp.where(g0, s2[24:], jnp.where(g1, s4[16:], jnp.where(g2, s8[8:], s16)))
    win = jnp.where(g0, POOL_WINDOWS[0],
                    jnp.where(g1, POOL_WINDOWS[1],
                              jnp.where(g2, POOL_WINDOWS[2], POOL_WINDOWS[3])))
    pos = pos0 + lax.broadcasted_iota(jnp.int32, (t, C_WIDTH), 0)
    cnt = jnp.minimum(pos + 1, win).astype(F32)
    pooled = total / cnt - u
    o_ref[0] = (_dot(pooled.astype(BF16), w_ref[...]) * scale_ref[...]).astype(BF16)


def _pool(u, prev16, w_blk, scale, *, pos0):
    bn, t, _ = u.shape
    kern = functools.partial(_pool_kernel, t=t, pos0=pos0)
    bmap = lambda b: (b, 0, 0)
    n = t + POOL_PAD
    return pl.pallas_call(
        kern,
        out_shape=jax.ShapeDtypeStruct((bn, t, C_WIDTH), BF16),
        grid=(bn,),
        in_specs=[pl.BlockSpec((1, t, C_WIDTH), bmap),
                  pl.BlockSpec((1, POOL_MAX, C_WIDTH), bmap),
                  _const_spec((C_WIDTH, C_WIDTH)),
                  _const_spec((1, C_WIDTH))],
        out_specs=pl.BlockSpec((1, t, C_WIDTH), bmap),
        scratch_shapes=[pltpu.VMEM((n, C_WIDTH), F32)] * 3,
        compiler_params=_params(("parallel",)),
        name="pool",
    )(u, prev16, w_blk, scale)


def _pool_weight(w_pool):
    w = jnp.zeros((C_WIDTH, C_WIDTH), w_pool.dtype)
    for g in range(len(POOL_WINDOWS)):
        sl = slice(g * POOL_GROUP, (g + 1) * POOL_GROUP)
        w = w.at[sl, sl].set(w_pool[g])
    return w.astype(BF16)


def _outmlp_kernel(ya_ref, yb_ref, yc_ref, x_ref, woa_ref, wob_ref, woc_ref,
                   g1_ref, g2_ref, g3_ref, w1_ref, w2_ref, o_ref):
    y = _dot(ya_ref[...], woa_ref[...]) + _dot(yb_ref[...], wob_ref[...])
    y = y + _dot(yc_ref[...], woc_ref[...])
    x1 = x_ref[...] + _rms(y, g1_ref[...])
    h = _rms(x1, g2_ref[...]).astype(BF16)
    m = jnp.zeros_like(x1)
    for f in range(D_FF // D_MODEL):
        sl = slice(f * D_MODEL, (f + 1) * D_MODEL)
        a = jnp.maximum(_dot(h, w1_ref[:, sl]), 0.0)
        m = m + _dot((a * a).astype(BF16), w2_ref[sl, :])
    o_ref[...] = x1 + _rms(m, g3_ref[...])


def _outmlp(ya, yb, yc, x, woa, wob, woc, g1, g2, g3, w1, w2):
    n = x.shape[0]
    assert n % ROW_TILE == 0
    row = lambda i: (i, 0)
    return pl.pallas_call(
        _outmlp_kernel,
        out_shape=jax.ShapeDtypeStruct((n, D_MODEL), F32),
        grid=(n // ROW_TILE,),
        in_specs=[pl.BlockSpec((ROW_TILE, ATT_WIDTH), row),
                  pl.BlockSpec((ROW_TILE, ATT_WIDTH), row),
                  pl.BlockSpec((ROW_TILE, C_WIDTH), row),
                  pl.BlockSpec((ROW_TILE, D_MODEL), row),
                  _const_spec((ATT_WIDTH, D_MODEL)),
                  _const_spec((ATT_WIDTH, D_MODEL)),
                  _const_spec((C_WIDTH, D_MODEL)),
                  _const_spec((1, D_MODEL)),
                  _const_spec((1, D_MODEL)),
                  _const_spec((1, D_MODEL)),
                  _const_spec((D_MODEL, D_FF)),
                  _const_spec((D_FF, D_MODEL))],
        out_specs=pl.BlockSpec((ROW_TILE, D_MODEL), row),
        compiler_params=_params(("parallel",)),
        name="outmlp",
    )(ya, yb, yc, x, woa, wob, woc, g1, g2, g3, w1, w2)


def _pad_axis(x, axis, size):
    pads = [(0, 0)] * x.ndim
    pads[axis] = (0, size - x.shape[axis])
    return jnp.pad(x, pads)


def _layer(x, weights, tri, stacked, *, layer, depth, seq, band_tab, band_shift, q_pos0,
           past=None):
    (g_pre_mix, w_proj, w_proj_tail, w_proj_t, w_pool_blk, pool_scale, woa, wob, woc, g_post_mix,
     g_pre_mlp, w1, w2, g_post_mlp) = weights
    bn = x.shape[0] // seq
    seq_tiles = seq // KEY_TILE if past is None else ROW_TILE // KEY_TILE
    (qa, qb, qi, u, wit, vb_tiles, va_tiles), stacked = _proj(
        x, g_pre_mix, w_proj, w_proj_tail, w_proj_t, stacked,
        seq_tiles=seq_tiles, layer=layer, depth=depth)
    per_batch = lambda t: t.reshape(bn, seq, t.shape[-1])
    qa, qb, qi, u = map(per_batch, (qa, qb, qi, u))
    ka_all, _, kb_all, _, ki2_all = (t.reshape(depth * bn, seq, t.shape[-1]) for t in stacked)

    if past is None:
        k_batch0 = layer * bn
        prev16 = jnp.zeros((bn, POOL_MAX, C_WIDTH), F32)
        n_keys = seq
        tq, qa_q, qi_q, qb_q, wit_q = KEY_TILE, qa, qi, qb, wit
    else:
        c_a_k, c_a_v, c_b_k, c_b_v, c_b_kidx, c_pool = past
        k_batch0 = 0
        ka, va, kb, vb, ki2 = (t[layer].reshape(bn, seq, t.shape[-1]) for t in stacked)
        flat = lambda t: t.reshape(bn, t.shape[1], ATT_WIDTH)
        band_rows = 3 * KEY_TILE
        ka_all = _pad_axis(jnp.concatenate([flat(c_a_k), ka], axis=1), 1, band_rows)
        va_tiles = _value_tiles(_pad_axis(jnp.concatenate([flat(c_a_v), va], axis=1), 1, band_rows))
        n_keys = c_b_k.shape[1] + seq
        key_rows = -(-n_keys // KEY_TILE) * KEY_TILE
        kb_all = _pad_axis(jnp.concatenate([flat(c_b_k), kb], axis=1), 1, key_rows)
        vb_tiles = _value_tiles(_pad_axis(jnp.concatenate([flat(c_b_v), vb], axis=1), 1, key_rows))
        c_ki2 = jnp.concatenate([c_b_kidx, c_b_kidx], axis=-1)
        ki2_all = _pad_axis(jnp.concatenate([c_ki2, ki2], axis=1), 1, key_rows)
        prev16 = jnp.pad(c_pool, ((0, 0), (1, 0), (0, 0)))
        tq = LANES
        qa_q, qi_q, qb_q = (_pad_axis(t, 1, LANES) for t in (qa, qi, qb))
        wit_q = _pad_axis(wit.reshape(LANES, bn, seq), 2, LANES).reshape(LANES, bn * LANES)

    ya = _band(qa_q, ka_all, va_tiles, band_tab, tq=tq, shift=band_shift,
               k_batch0=k_batch0)[:, :seq]
    yb = _sparse(qi_q, wit_q, ki2_all, qb_q, kb_all, vb_tiles, tri, tq=tq, n_real=min(seq, tq),
                 q_pos0=q_pos0, n_keys=n_keys, k_batch0=k_batch0)[:, :seq]
    yc = _pool(u, prev16, w_pool_blk, pool_scale, pos0=q_pos0)
    flat2 = lambda t: t.reshape(bn * seq, t.shape[-1])
    x = _outmlp(flat2(ya), flat2(yb), flat2(yc), x, woa, wob, woc,
                g_post_mix, g_pre_mlp, g_post_mlp, w1, w2)
    return x, stacked, u


def _state_leaves(stacked, depth, bn, seq, n_keep):
    ka, va, kb, vb, ki2 = (t.reshape(depth, bn, seq, t.shape[-1]) for t in stacked)
    heads = lambda t: t.reshape(depth, bn, t.shape[2], N_HEADS, HEAD_DIM)
    return (heads(ka[:, :, seq - n_keep:]), heads(va[:, :, seq - n_keep:]), heads(kb), heads(vb),
            ki2[..., :IDX_DIM])


def kernel(x_prompt, x_sample, cache_a_k, cache_a_v, cache_b_k, cache_b_v, cache_b_kidx, state_pool, g_pre_mix, w_in, rel_bias, w_pool, pool_scale, w_out, g_post_mix, g_pre_mlp, w_ff1, w_ff2, g_post_mlp):
    batch, seq, _ = x_prompt.shape
    dec_batch, dec_seq, _ = x_sample.shape
    depth = w_in.shape[0]
    past_len = cache_b_k.shape[2]
    n_a = cache_a_k.shape[2]
    assert seq % KEY_TILE == 0 and (batch * seq) % ROW_TILE == 0
    assert (dec_batch * dec_seq) % ROW_TILE == 0 and n_a + dec_seq <= 3 * KEY_TILE
    assert dec_seq <= LANES

    tri = (jnp.arange(KEY_TILE)[:, None] >= jnp.arange(KEY_TILE)[None, :]).astype(BF16)
    xp = x_prompt.reshape(batch * seq, D_MODEL)
    xs = x_sample.reshape(dec_batch * dec_seq, D_MODEL)
    p_stacked = s_stacked = None
    p_pools, s_pools = [], []
    row = lambda t: t.reshape(1, -1)
    for l in range(depth):
        weights = (row(g_pre_mix[l]), *_proj_weight(w_in[l]), _pool_weight(w_pool[l]),
                   row(pool_scale[l]),
                   w_out[l, :ATT_WIDTH].astype(BF16),
                   w_out[l, ATT_WIDTH:2 * ATT_WIDTH].astype(BF16),
                   w_out[l, 2 * ATT_WIDTH:].astype(BF16),
                   row(g_post_mix[l]), row(g_pre_mlp[l]),
                   w_ff1[l].astype(BF16), w_ff2[l].astype(BF16), row(g_post_mlp[l]))
        xp, p_stacked, u = _layer(
            xp, weights, tri, p_stacked, layer=l, depth=depth, seq=seq,
            band_tab=_band_table(rel_bias[l], 2 * KEY_TILE, 0, KEY_TILE, 3 * KEY_TILE),
            band_shift=2, q_pos0=0)
        p_pools.append(u[:, seq - (POOL_MAX - 1):])
        xs, s_stacked, u = _layer(
            xs, weights, tri, s_stacked, layer=l, depth=depth, seq=dec_seq,
            band_tab=_band_table(rel_bias[l], past_len, past_len - n_a, LANES, n_a + dec_seq),
            band_shift=0, q_pos0=past_len,
            past=(cache_a_k[l], cache_a_v[l], cache_b_k[l], cache_b_v[l], cache_b_kidx[l],
                  state_pool[l]))
        s_pools.append(jnp.concatenate([state_pool[l], u], axis=1)[:, dec_seq:])
    return ((xp.reshape(batch, seq, D_MODEL), xs.reshape(dec_batch, dec_seq, D_MODEL))
            + _state_leaves(p_stacked, depth, batch, seq, min(A_WINDOW, seq))
            + (jnp.stack(p_pools, axis=0),)
            + _state_leaves(s_stacked, depth, dec_batch, dec_seq, dec_seq)
            + (jnp.stack(s_pools, axis=0),))
```

```python
import functools

import jax
import jax.numpy as jnp
from jax import lax
from jax.experimental import pallas as pl
from jax.experimental.pallas import tpu as pltpu

D_MODEL = 1024
CHUNK = 64
CHUNK_SHIFT = 6
HEAD_DIM = 64
N_HEADS = 6
ATT_WIDTH = N_HEADS * HEAD_DIM
N_PAIRS = N_HEADS // 2
C_WIDTH = 256
A_LEFT_CHUNKS = 8
A_WINDOW = A_LEFT_CHUNKS * CHUNK
REL_CLIP = 128
IDX_HEADS = 8
IDX_DIM = 64
TOPK = 256
POOL_WINDOWS = (2, 4, 8, 16)
POOL_GROUP = 64
POOL_MAX = 16
POOL_PAD = 32
D_FF = 4 * D_MODEL
RMS_EPS = 1e-6
IDX_SCALE = IDX_HEADS ** -0.5 * IDX_DIM ** -0.5
ATT_SCALE = HEAD_DIM ** -0.5

LANES = 128
SUBLANES = 8
KEY_TILE = 256
KEY_TILE_SHIFT = 8
ROW_TILE = 512
NEG = -1e30
LOG2E = 1.4426950408889634
VALUE_ROWS = HEAD_DIM + 16
STEP_GROUP = 4
FIRST_STEPS = 20
SEARCH_STEPS = 40
VMEM_LIMIT = 48 * 1024 * 1024

_OFF_KB, _OFF_QI, _OFF_KI, _OFF_WI, _OFF_U, _OFF_END = 1536, 2304, 2816, 2880, 2888, 3144

F32 = jnp.float32
BF16 = jnp.bfloat16


def _const_spec(shape):
    zeros = (0,) * len(shape)
    return pl.BlockSpec(shape, lambda *_: zeros, pipeline_mode=pl.Buffered(1))


def _params(semantics):
    return pltpu.CompilerParams(dimension_semantics=semantics, vmem_limit_bytes=VMEM_LIMIT)


def _rms(x, g):
    ms = jnp.mean(x * x, axis=-1, keepdims=True)
    return x * lax.rsqrt(ms + RMS_EPS) * g


def _dot(a, b):
    return jnp.dot(a, b, preferred_element_type=F32)


def _dot_t(a, b):
    return lax.dot_general(a, b, (((1,), (1,)), ((), ())), preferred_element_type=F32)


def _half_masks(rows):
    lane = lax.broadcasted_iota(jnp.int32, (rows, LANES), 1)
    first = lane < HEAD_DIM
    m0 = jnp.where(first, 1.0, 0.0).astype(BF16)
    m1 = jnp.where(first, 0.0, 1.0).astype(BF16)
    return first, m0, m1


def _fold(x, op):
    parts = [x[i:i + SUBLANES] for i in range(0, x.shape[0], SUBLANES)]
    lanes = min(4, len(parts))
    acc = parts[:lanes]
    for i, part in enumerate(parts[lanes:]):
        acc[i % lanes] = op(acc[i % lanes], part)
    while len(acc) > 1:
        acc = [op(acc[i], acc[i + 1]) for i in range(0, len(acc), 2)]
    return acc[0]


def _pair_loop(n, body, init):
    def pair(i, carry):
        return body(2 * i + 1, body(2 * i, carry, 0), 1)
    carry = lax.fori_loop(0, lax.shift_right_logical(n, 1), pair, init)
    return lax.cond(lax.rem(n, 2) == 1, lambda c: body(n - 1, c, 0), lambda c: c, carry)


def _proj_kernel(x_ref, g_ref, w_ref, wtail_ref, wt_ref, qa_ref, qb_ref, qi_ref, u_ref, wit_ref,
                 vbt_ref, vat_ref, ka_ref, va_ref, kb_ref, vb_ref, ki2_ref):
    def put(ref, val):
        ref[0] = val
        if ref.shape[0] > 1:
            ref[1:] = jnp.zeros((ref.shape[0] - 1,) + val.shape, val.dtype)

    h = _rms(x_ref[...], g_ref[...]).astype(BF16)
    z = _dot(h, w_ref[:, 0:768])
    qa_ref[...] = (z[:, 0:384] * (ATT_SCALE * LOG2E)).astype(BF16)
    put(ka_ref, z[:, 384:768])
    z = _dot(h, w_ref[:, 768:1536])
    put(va_ref, z[:, 0:384])
    qb_ref[...] = (z[:, 384:768] * (ATT_SCALE * LOG2E)).astype(BF16)
    z = _dot(h, w_ref[:, 1536:2304])
    put(kb_ref, z[:, 0:384])
    put(vb_ref, z[:, 384:768])
    qi_ref[...] = _dot(h, w_ref[:, 2304:2816]).astype(BF16)
    z = _dot(h, wtail_ref[...])
    put(ki2_ref, z[:, 0:128])
    u_ref[...] = z[:, 128:384]
    zt = _dot_t(wt_ref[...], h)
    wit_ref[...] = zt[0:LANES]
    ones = jnp.ones((VALUE_ROWS - HEAD_DIM, KEY_TILE), BF16)
    for base, vt_ref in ((LANES, vbt_ref), (LANES + ATT_WIDTH, vat_ref)):
        for tile in range(ROW_TILE // KEY_TILE):
            lanes = slice(KEY_TILE * tile, KEY_TILE * (tile + 1))
            for hd in range(N_HEADS):
                rows = slice(base + HEAD_DIM * hd, base + HEAD_DIM * (hd + 1))
                vt_ref[0, tile, VALUE_ROWS * hd:VALUE_ROWS * hd + HEAD_DIM, :] = (
                    zt[rows, lanes].astype(BF16))
                vt_ref[0, tile, VALUE_ROWS * hd + HEAD_DIM:VALUE_ROWS * (hd + 1), :] = ones


N_PROJ_IN = 5
N_PROJ_PLAIN = 7
STACKED_WIDTHS = (ATT_WIDTH,) * 4 + (LANES,)


def _proj(x, g, w, wtail, wt, stacked, *, seq_tiles, layer, depth):
    n = x.shape[0]
    tiles_per_step = ROW_TILE // KEY_TILE
    assert n % ROW_TILE == 0 and seq_tiles % tiles_per_step == 0
    steps_per_seq = seq_tiles // tiles_per_step
    row = lambda i: (i, 0)
    vt_shape = (n // (seq_tiles * KEY_TILE), seq_tiles, N_HEADS * VALUE_ROWS, KEY_TILE)
    vt_spec = pl.BlockSpec((1, tiles_per_step, N_HEADS * VALUE_ROWS, KEY_TILE),
                           lambda i: (i // steps_per_seq, i % steps_per_seq, 0, 0))
    plain = ((ATT_WIDTH, BF16), (ATT_WIDTH, BF16), (IDX_HEADS * IDX_DIM, BF16), (C_WIDTH, F32))
    out_shape = [jax.ShapeDtypeStruct((n, wd), dt) for wd, dt in plain]
    out_specs = [pl.BlockSpec((ROW_TILE, wd), row) for wd, _ in plain]
    out_shape += [jax.ShapeDtypeStruct((LANES, n), F32)] + [jax.ShapeDtypeStruct(vt_shape, BF16)] * 2
    out_specs += [pl.BlockSpec((LANES, ROW_TILE), lambda i: (0, i)), vt_spec, vt_spec]
    assert len(out_shape) == N_PROJ_PLAIN
    out_shape += [jax.ShapeDtypeStruct((depth, n, wd), F32) for wd in STACKED_WIDTHS]
    in_specs = [pl.BlockSpec((ROW_TILE, D_MODEL), row),
                _const_spec((1, D_MODEL)),
                _const_spec((D_MODEL, _OFF_KI)),
                _const_spec((D_MODEL, 2 * IDX_DIM + C_WIDTH)),
                _const_spec((LANES + 2 * ATT_WIDTH, D_MODEL))]
    if stacked is None:
        assert layer == 0
        operands, aliases, kern = (), {}, _proj_kernel
        out_specs += [pl.BlockSpec((depth, ROW_TILE, wd), lambda i: (0, i, 0))
                      for wd in STACKED_WIDTHS]
    else:
        operands = tuple(stacked)
        in_specs += [pl.BlockSpec(memory_space=pl.ANY)] * len(operands)
        aliases = {N_PROJ_IN + k: N_PROJ_PLAIN + k for k in range(len(operands))}
        out_specs += [pl.BlockSpec((1, ROW_TILE, wd), lambda i: (layer, i, 0))
                      for wd in STACKED_WIDTHS]

        def kern(*refs):
            _proj_kernel(*refs[:N_PROJ_IN], *refs[N_PROJ_IN + len(operands):])

    outs = pl.pallas_call(
        kern,
        out_shape=tuple(out_shape),
        grid=(n // ROW_TILE,),
        in_specs=in_specs,
        out_specs=tuple(out_specs),
        input_output_aliases=aliases,
        compiler_params=_params(("parallel",)),
        name="proj",
    )(x, g, w, wtail, wt, *operands)
    return outs[:N_PROJ_PLAIN], outs[N_PROJ_PLAIN:]


def _proj_weight(w_in):
    w_bf = w_in.astype(BF16)
    ki = w_bf[:, _OFF_KI:_OFF_WI]
    wtail = jnp.concatenate([ki, ki, w_bf[:, _OFF_U:_OFF_END]], axis=1)
    wi_t = jnp.pad(w_bf[:, _OFF_WI:_OFF_U].T, ((0, LANES - IDX_HEADS), (0, 0)))
    vb_t = w_bf[:, _OFF_KB + ATT_WIDTH:_OFF_QI].T
    va_t = w_bf[:, 2 * ATT_WIDTH:3 * ATT_WIDTH].T
    return w_bf[:, :_OFF_KI], wtail, jnp.concatenate([wi_t, vb_t, va_t], axis=0)


def _mask_heads(q_ref, qm_ref, heads, tq):
    _, m0, m1 = _half_masks(tq)
    for h in range(heads):
        pair = slice(LANES * (h // 2), LANES * (h // 2 + 1))
        qm_ref[h] = q_ref[0, :, pair] * (m0 if h % 2 == 0 else m1)


def _attend_scratch(tq):
    return [pltpu.VMEM((N_HEADS, VALUE_ROWS, tq), F32),
            pltpu.VMEM((ATT_WIDTH, tq), F32),
            pltpu.VMEM((2, N_HEADS, KEY_TILE, tq), F32),
            pltpu.VMEM((2, N_HEADS, KEY_TILE, tq), BF16)]


def _attend_init(acc_ref, tq):
    for h in range(N_HEADS):
        acc_ref[h] = jnp.zeros((VALUE_ROWS, tq), F32)
    return tuple(jnp.full((1, tq), NEG, F32) for _ in range(N_HEADS))


def _attend_tile(k_ref, vt_ref, tile, bias_of_head, qm_ref, acc_ref, s_ref, p_ref, m_run, slot):
    start = pl.multiple_of(tile * KEY_TILE, KEY_TILE)
    m_new = []
    for pair in range(N_PAIRS):
        kt = k_ref[0, pl.ds(start, KEY_TILE), LANES * pair:LANES * (pair + 1)].astype(BF16)
        for h in (2 * pair, 2 * pair + 1):
            s = _dot_t(kt, qm_ref[h]) + bias_of_head(h)
            s_ref[slot, h] = s
            col_max = jnp.max(_fold(s, jnp.maximum), axis=0, keepdims=True)
            m_new.append(jnp.maximum(m_run[h], col_max))
    for h in range(N_HEADS):
        p_ref[slot, h] = jnp.exp2(s_ref[slot, h] - m_new[h]).astype(BF16)
        acc_ref[h] = jnp.exp2(m_run[h] - m_new[h]) * acc_ref[h]
    for h in range(N_HEADS):
        vt = vt_ref[0, tile, VALUE_ROWS * h:VALUE_ROWS * (h + 1), :]
        acc_ref[h] = acc_ref[h] + _dot(vt, p_ref[slot, h])
    return tuple(m_new)


def _attend_finish(acc_ref, out_ref, o_ref):
    for h in range(N_HEADS):
        acc = acc_ref[h]
        out_ref[HEAD_DIM * h:HEAD_DIM * (h + 1), :] = acc[:HEAD_DIM] / acc[HEAD_DIM:HEAD_DIM + 1]
    o_ref[0] = out_ref[...].T.astype(BF16)


def _band_kernel(q_ref, k_ref, vt_ref, tab_ref, o_ref, qm_ref, acc_ref, out_ref, s_ref, p_ref,
                 *, tq, shift):
    j = pl.program_id(1)
    _mask_heads(q_ref, qm_ref, N_HEADS, tq)
    m_run = _attend_init(acc_ref, tq)
    for t in range(3):
        blk = j + (t - shift)
        bias = lambda h, t=t, blk=blk: jnp.where(blk >= 0, tab_ref[h, t], NEG)
        m_run = _attend_tile(k_ref, vt_ref, jnp.maximum(blk, 0), bias, qm_ref, acc_ref, s_ref,
                             p_ref, m_run, t % 2)
    _attend_finish(acc_ref, out_ref, o_ref)


def _band(q, k, vt, tab, *, tq, shift, k_batch0=0):
    bn, tq_total, _ = q.shape
    tk_total = k.shape[1]
    max_tiles = tk_total // KEY_TILE
    kern = functools.partial(_band_kernel, tq=tq, shift=shift)
    qmap = lambda b, j: (b, j, 0)
    return pl.pallas_call(
        kern,
        out_shape=jax.ShapeDtypeStruct((bn, tq_total, ATT_WIDTH), BF16),
        grid=(bn, tq_total // tq),
        in_specs=[pl.BlockSpec((1, tq, ATT_WIDTH), qmap),
                  pl.BlockSpec((1, tk_total, ATT_WIDTH), lambda b, j: (k_batch0 + b, 0, 0)),
                  pl.BlockSpec((1, max_tiles, N_HEADS * VALUE_ROWS, KEY_TILE),
                               lambda b, j: (b, 0, 0, 0)),
                  _const_spec(tab.shape)],
        out_specs=pl.BlockSpec((1, tq, ATT_WIDTH), qmap),
        scratch_shapes=[pltpu.VMEM((N_HEADS, tq, LANES), BF16),
                        *_attend_scratch(tq)],
        compiler_params=_params(("parallel", "arbitrary")),
        name="band",
    )(q, k, vt, tab)


def _band_table(rel_bias, q_pos0, k_pos0, tq, n_real):
    tk = 3 * KEY_TILE
    span = tq + tk
    diff = (q_pos0 - k_pos0) - (tk - 1) + jnp.arange(span)
    vec = jnp.take(rel_bias.astype(F32), jnp.clip(diff, -REL_CLIP, REL_CLIP) + REL_CLIP, axis=1)
    h = vec.shape[0]
    skew = jnp.tile(vec, (1, tk))[:, :tk * (span - 1)].reshape(h, tk, span - 1)
    bias = skew[:, :, tk - 1:tk - 1 + tq]
    qc = (q_pos0 + jnp.arange(tq)) // CHUNK
    kc = (k_pos0 + jnp.arange(tk)) // CHUNK
    valid = ((kc[:, None] <= qc[None, :]) & (kc[:, None] >= qc[None, :] - A_LEFT_CHUNKS)
             & (jnp.arange(tk) < n_real)[:, None])
    tab = jnp.where(valid[None], bias * LOG2E, NEG)
    return tab.reshape(h, 3, KEY_TILE, tq)


def _sparse_kernel(qi_ref, wit_ref, ki2_ref, qb_ref, kb_ref, vt_ref, tri_ref, o_ref,
                   sc_ref, mb_ref, qim_ref, qbm_ref, acc_ref, out_ref, s_ref, p_ref,
                   *, tq, n_real, q_pos0, n_keys):
    j = pl.program_id(1)
    q_first = q_pos0 + j * n_real
    last_chunk = lax.shift_right_logical(q_first + (n_real - 1), CHUNK_SHIFT)
    k_end = jnp.minimum((last_chunk + 1) * CHUNK, n_keys)
    n_tiles = lax.shift_right_logical(k_end + (KEY_TILE - 1), KEY_TILE_SHIFT)

    lane = lax.broadcasted_iota(jnp.int32, (1, tq), 1)
    q_pos = q_first + lane
    k_lim = jnp.minimum((lax.shift_right_logical(q_pos, CHUNK_SHIFT) + 1) * CHUNK, n_keys)
    key_row = lax.broadcasted_iota(jnp.int32, (KEY_TILE, tq), 0)

    _mask_heads(qi_ref, qim_ref, IDX_HEADS, tq)
    _mask_heads(qb_ref, qbm_ref, N_HEADS, tq)
    w = wit_ref[0:SUBLANES, :] * IDX_SCALE

    def score_tile(t, carry, _):
        rmin8, rmax8 = carry
        start = pl.multiple_of(t * KEY_TILE, KEY_TILE)
        kt = ki2_ref[0, pl.ds(start, KEY_TILE), :].astype(BF16)
        sc = jnp.zeros((KEY_TILE, tq), F32)
        for h in range(IDX_HEADS):
            sc = sc + jnp.maximum(_dot_t(kt, qim_ref[h]), 0.0) * w[h:h + 1, :]
        adm = (start + key_row) < k_lim
        lowest = jnp.where(adm, sc, -jnp.inf)
        sc_ref[t] = lowest
        rmax8 = jnp.maximum(rmax8, _fold(lowest, jnp.maximum))
        rmin8 = jnp.minimum(rmin8, _fold(jnp.where(adm, sc, jnp.inf), jnp.minimum))
        return rmin8, rmax8

    rmin8, rmax8 = _pair_loop(
        n_tiles, score_tile,
        (jnp.full((SUBLANES, tq), jnp.inf, F32), jnp.full((SUBLANES, tq), -jnp.inf, F32)))
    rmin = jnp.min(rmin8, axis=0, keepdims=True)
    rmax = jnp.max(rmax8, axis=0, keepdims=True)

    def count_ge(thr):
        def body(t, c8):
            return c8 + _fold(jnp.where(sc_ref[t] >= thr, 1.0, 0.0), jnp.add)
        c8 = lax.fori_loop(0, n_tiles, body, jnp.zeros((SUBLANES, tq), F32))
        return jnp.sum(c8, axis=0, keepdims=True)

    def n_unsettled(c_lo):
        return jnp.sum(jnp.where(c_lo > TOPK, 1.0, 0.0))

    def search(carry, last_step):
        def cond(c):
            return jnp.logical_and(c[0] < last_step, c[1] > 0.0)

        def step(c):
            s, _, lo, hi, c_lo, c_hi = c
            for _ in range(STEP_GROUP):
                mid = lo + (hi - lo) * 0.5
                c_mid = count_ge(mid)
                unsettled = c_lo > TOPK
                enough = c_mid >= TOPK
                lo, c_lo = (jnp.where(unsettled, jnp.where(enough, mid, lo), lo),
                            jnp.where(unsettled, jnp.where(enough, c_mid, c_lo), c_lo))
                hi, c_hi = (jnp.where(unsettled, jnp.where(enough, hi, mid), hi),
                            jnp.where(unsettled, jnp.where(enough, c_hi, c_mid), c_hi))
            return s + STEP_GROUP, n_unsettled(c_lo), lo, hi, c_lo, c_hi

        return lax.while_loop(cond, step, carry)

    hi0 = rmax + jnp.maximum(jnp.abs(rmax), 1e-30) * (2.0 ** -10)
    c_lo0 = jnp.where(lane < n_real, k_lim, 0).astype(F32)
    steps, n_open, lo, hi, c_lo, c_hi = search(
        (jnp.int32(0), n_unsettled(c_lo0), rmin, hi0, c_lo0, jnp.zeros((1, tq), F32)),
        FIRST_STEPS)

    def n_untied():
        def body(t, carry):
            vmin8, vmax8 = carry
            x = sc_ref[t]
            vmax8 = jnp.maximum(vmax8, _fold(
                jnp.where(x >= lo, jnp.where(x < hi, x, -jnp.inf), -jnp.inf), jnp.maximum))
            vmin8 = jnp.minimum(vmin8, _fold(
                jnp.where(x >= lo, jnp.where(x < hi, x, jnp.inf), jnp.inf), jnp.minimum))
            return vmin8, vmax8
        vmin8, vmax8 = lax.fori_loop(
            0, n_tiles, body,
            (jnp.full((SUBLANES, tq), jnp.inf, F32), jnp.full((SUBLANES, tq), -jnp.inf, F32)))
        vmin = jnp.min(vmin8, axis=0, keepdims=True)
        vmax = jnp.max(vmax8, axis=0, keepdims=True)
        return jnp.sum(jnp.where(c_lo > TOPK, jnp.where(vmax == vmin, 0.0, 1.0), 0.0))

    n_open = lax.cond(n_open > 0.0, n_untied, lambda: jnp.float32(0.0))
    _, _, lo, hi, _, c_hi = search((steps, n_open, lo, hi, c_lo, c_hi), SEARCH_STEPS)

    need = TOPK - c_hi

    def mask_tile(t, run, _):
        x = sc_ref[t]
        inr = jnp.where(x >= lo, jnp.where(x < hi, 1.0, 0.0), 0.0)
        rank = _dot(tri_ref[...], inr.astype(BF16)) + run
        mb_ref[t] = jnp.where(
            x >= hi, 0.0, jnp.where(inr > 0.0, jnp.where(rank <= need, 0.0, NEG), NEG))
        return rank[KEY_TILE - 1:KEY_TILE, :]

    _pair_loop(n_tiles, mask_tile, jnp.zeros((1, tq), F32))

    def attend(t, m_run, slot):
        return _attend_tile(kb_ref, vt_ref, t, lambda h: mb_ref[t], qbm_ref, acc_ref, s_ref, p_ref,
                            m_run, slot)

    _pair_loop(n_tiles, attend, _attend_init(acc_ref, tq))
    _attend_finish(acc_ref, out_ref, o_ref)


def _sparse(qi, wit, ki2, qb, kb, vt, tri, *, tq, n_real, q_pos0, n_keys, k_batch0=0):
    bn, tq_total, _ = qb.shape
    tk_total = kb.shape[1]
    assert tk_total % KEY_TILE == 0 and n_keys <= tk_total and tq % LANES == 0
    max_tiles = tk_total // KEY_TILE
    blocks = tq_total // tq
    kern = functools.partial(_sparse_kernel, tq=tq, n_real=n_real, q_pos0=q_pos0, n_keys=n_keys)
    qmap = lambda b, j: (b, j, 0)
    kmap = lambda b, j: (k_batch0 + b, 0, 0)
    return pl.pallas_call(
        kern,
        out_shape=jax.ShapeDtypeStruct((bn, tq_total, ATT_WIDTH), BF16),
        grid=(bn, blocks),
        in_specs=[pl.BlockSpec((1, tq, IDX_HEADS * IDX_DIM), qmap),
                  pl.BlockSpec((LANES, tq), lambda b, j: (0, b * blocks + j)),
                  pl.BlockSpec((1, tk_total, LANES), kmap),
                  pl.BlockSpec((1, tq, ATT_WIDTH), qmap),
                  pl.BlockSpec((1, tk_total, ATT_WIDTH), kmap),
                  pl.BlockSpec((1, max_tiles, N_HEADS * VALUE_ROWS, KEY_TILE),
                               lambda b, j: (b, 0, 0, 0)),
                  _const_spec((KEY_TILE, KEY_TILE))],
        out_specs=pl.BlockSpec((1, tq, ATT_WIDTH), qmap),
        scratch_shapes=[pltpu.VMEM((max_tiles, KEY_TILE, tq), F32),
                        pltpu.VMEM((max_tiles, KEY_TILE, tq), F32),
                        pltpu.VMEM((IDX_HEADS, tq, LANES), BF16),
                        pltpu.VMEM((N_HEADS, tq, LANES), BF16),
                        *_attend_scratch(tq)],
        compiler_params=_params(("parallel", "arbitrary")),
        name="sparse",
    )(qi, wit, ki2, qb, kb, vt, tri)


def _value_tiles(v):
    bn, keys, _ = v.shape
    v = v.astype(BF16).reshape(bn, keys // KEY_TILE, KEY_TILE, N_HEADS, HEAD_DIM)
    v = jnp.pad(v.transpose(0, 1, 3, 4, 2),
                ((0, 0), (0, 0), (0, 0), (0, VALUE_ROWS - HEAD_DIM), (0, 0)), constant_values=1)
    return v.reshape(bn, keys // KEY_TILE, N_HEADS * VALUE_ROWS, KEY_TILE)


def _pool_kernel(u_ref, prev_ref, w_ref, scale_ref, o_ref, f_ref, a_ref, b_ref, *, t, pos0):
    n = t + POOL_PAD
    u = u_ref[0]
    f_ref[0:16, :] = jnp.zeros((16, C_WIDTH), F32)
    f_ref[16:32, :] = prev_ref[0]
    f_ref[pl.ds(POOL_PAD, t), :] = u
    s2 = f_ref[pl.ds(8, n - 8), :] + f_ref[pl.ds(7, n - 8), :]
    a_ref[pl.ds(8, n - 8), :] = s2
    s4 = a_ref[pl.ds(16, n - 16), :] + a_ref[pl.ds(14, n - 16), :]
    b_ref[pl.ds(16, n - 16), :] = s4
    s8 = b_ref[pl.ds(24, n - 24), :] + b_ref[pl.ds(20, n - 24), :]
    a_ref[pl.ds(24, n - 24), :] = s8
    s16 = a_ref[pl.ds(32, t), :] + a_ref[pl.ds(24, t), :]
    lane = lax.broadcasted_iota(jnp.int32, (t, C_WIDTH), 1)
    g0, g1, g2 = lane < POOL_GROUP, lane < 2 * POOL_GROUP, lane < 3 * POOL_GROUP
    total = jnp.where(g0, s2[24:], jnp.where(g1, s4[16:], jnp.where(g2, s8[8:], s16)))
    win = jnp.where(g0, POOL_WINDOWS[0],
                    jnp.where(g1, POOL_WINDOWS[1],
                              jnp.where(g2, POOL_WINDOWS[2], POOL_WINDOWS[3])))
    pos = pos0 + lax.broadcasted_iota(jnp.int32, (t, C_WIDTH), 0)
    cnt = jnp.minimum(pos + 1, win).astype(F32)
    pooled = total / cnt - u
    o_ref[0] = (_dot(pooled.astype(BF16), w_ref[...]) * scale_ref[...]).astype(BF16)


def _pool(u, prev16, w_blk, scale, *, pos0):
    bn, t, _ = u.shape
    kern = functools.partial(_pool_kernel, t=t, pos0=pos0)
    bmap = lambda b: (b, 0, 0)
    n = t + POOL_PAD
    return pl.pallas_call(
        kern,
        out_shape=jax.ShapeDtypeStruct((bn, t, C_WIDTH), BF16),
        grid=(bn,),
        in_specs=[pl.BlockSpec((1, t, C_WIDTH), bmap),
                  pl.BlockSpec((1, POOL_MAX, C_WIDTH), bmap),
                  _const_spec((C_WIDTH, C_WIDTH)),
                  _const_spec((1, C_WIDTH))],
        out_specs=pl.BlockSpec((1, t, C_WIDTH), bmap),
        scratch_shapes=[pltpu.VMEM((n, C_WIDTH), F32)] * 3,
        compiler_params=_params(("parallel",)),
        name="pool",
    )(u, prev16, w_blk, scale)


def _pool_weight(w_pool):
    w = jnp.zeros((C_WIDTH, C_WIDTH), w_pool.dtype)
    for g in range(len(POOL_WINDOWS)):
        sl = slice(g * POOL_GROUP, (g + 1) * POOL_GROUP)
        w = w.at[sl, sl].set(w_pool[g])
    return w.astype(BF16)


def _outmlp_kernel(ya_ref, yb_ref, yc_ref, x_ref, woa_ref, wob_ref, woc_ref,
                   g1_ref, g2_ref, g3_ref, w1_ref, w2_ref, o_ref):
    y = _dot(ya_ref[...], woa_ref[...]) + _dot(yb_ref[...], wob_ref[...])
    y = y + _dot(yc_ref[...], woc_ref[...])
    x1 = x_ref[...] + _rms(y, g1_ref[...])
    h = _rms(x1, g2_ref[...]).astype(BF16)
    m = jnp.zeros_like(x1)
    for f in range(D_FF // D_MODEL):
        sl = slice(f * D_MODEL, (f + 1) * D_MODEL)
        a = jnp.maximum(_dot(h, w1_ref[:, sl]), 0.0)
        m = m + _dot((a * a).astype(BF16), w2_ref[sl, :])
    o_ref[...] = x1 + _rms(m, g3_ref[...])


def _outmlp(ya, yb, yc, x, woa, wob, woc, g1, g2, g3, w1, w2):
    n = x.shape[0]
    assert n % ROW_TILE == 0
    row = lambda i: (i, 0)
    return pl.pallas_call(
        _outmlp_kernel,
        out_shape=jax.ShapeDtypeStruct((n, D_MODEL), F32),
        grid=(n // ROW_TILE,),
        in_specs=[pl.BlockSpec((ROW_TILE, ATT_WIDTH), row),
                  pl.BlockSpec((ROW_TILE, ATT_WIDTH), row),
                  pl.BlockSpec((ROW_TILE, C_WIDTH), row),
                  pl.BlockSpec((ROW_TILE, D_MODEL), row),
                  _const_spec((ATT_WIDTH, D_MODEL)),
                  _const_spec((ATT_WIDTH, D_MODEL)),
                  _const_spec((C_WIDTH, D_MODEL)),
                  _const_spec((1, D_MODEL)),
                  _const_spec((1, D_MODEL)),
                  _const_spec((1, D_MODEL)),
                  _const_spec((D_MODEL, D_FF)),
                  _const_spec((D_FF, D_MODEL))],
        out_specs=pl.BlockSpec((ROW_TILE, D_MODEL), row),
        compiler_params=_params(("parallel",)),
        name="outmlp",
    )(ya, yb, yc, x, woa, wob, woc, g1, g2, g3, w1, w2)


def _pad_axis(x, axis, size):
    pads = [(0, 0)] * x.ndim
    pads[axis] = (0, size - x.shape[axis])
    return jnp.pad(x, pads)


def _layer(x, weights, tri, stacked, *, layer, depth, seq, band_tab, band_shift, q_pos0,
           past=None):
    (g_pre_mix, w_proj, w_proj_tail, w_proj_t, w_pool_blk, pool_scale, woa, wob, woc, g_post_mix,
     g_pre_mlp, w1, w2, g_post_mlp) = weights
    bn = x.shape[0] // seq
    seq_tiles = seq // KEY_TILE if past is None else ROW_TILE // KEY_TILE
    (qa, qb, qi, u, wit, vb_tiles, va_tiles), stacked = _proj(
        x, g_pre_mix, w_proj, w_proj_tail, w_proj_t, stacked,
        seq_tiles=seq_tiles, layer=layer, depth=depth)
    per_batch = lambda t: t.reshape(bn, seq, t.shape[-1])
    qa, qb, qi, u = map(per_batch, (qa, qb, qi, u))
    ka_all, _, kb_all, _, ki2_all = (t.reshape(depth * bn, seq, t.shape[-1]) for t in stacked)

    if past is None:
        k_batch0 = layer * bn
        prev16 = jnp.zeros((bn, POOL_MAX, C_WIDTH), F32)
        n_keys = seq
        tq, qa_q, qi_q, qb_q, wit_q = KEY_TILE, qa, qi, qb, wit
    else:
        c_a_k, c_a_v, c_b_k, c_b_v, c_b_kidx, c_pool = past
        k_batch0 = 0
        ka, va, kb, vb, ki2 = (t[layer].reshape(bn, seq, t.shape[-1]) for t in stacked)
        flat = lambda t: t.reshape(bn, t.shape[1], ATT_WIDTH)
        band_rows = 3 * KEY_TILE
        ka_all = _pad_axis(jnp.concatenate([flat(c_a_k), ka], axis=1), 1, band_rows)
        va_tiles = _value_tiles(_pad_axis(jnp.concatenate([flat(c_a_v), va], axis=1), 1, band_rows))
        n_keys = c_b_k.shape[1] + seq
        key_rows = -(-n_keys // KEY_TILE) * KEY_TILE
        kb_all = _pad_axis(jnp.concatenate([flat(c_b_k), kb], axis=1), 1, key_rows)
        vb_tiles = _value_tiles(_pad_axis(jnp.concatenate([flat(c_b_v), vb], axis=1), 1, key_rows))
        c_ki2 = jnp.concatenate([c_b_kidx, c_b_kidx], axis=-1)
        ki2_all = _pad_axis(jnp.concatenate([c_ki2, ki2], axis=1), 1, key_rows)
        prev16 = jnp.pad(c_pool, ((0, 0), (1, 0), (0, 0)))
        tq = LANES
        qa_q, qi_q, qb_q = (_pad_axis(t, 1, LANES) for t in (qa, qi, qb))
        wit_q = _pad_axis(wit.reshape(LANES, bn, seq), 2, LANES).reshape(LANES, bn * LANES)

    ya = _band(qa_q, ka_all, va_tiles, band_tab, tq=tq, shift=band_shift,
               k_batch0=k_batch0)[:, :seq]
    yb = _sparse(qi_q, wit_q, ki2_all, qb_q, kb_all, vb_tiles, tri, tq=tq, n_real=min(seq, tq),
                 q_pos0=q_pos0, n_keys=n_keys, k_batch0=k_batch0)[:, :seq]
    yc = _pool(u, prev16, w_pool_blk, pool_scale, pos0=q_pos0)
    flat2 = lambda t: t.reshape(bn * seq, t.shape[-1])
    x = _outmlp(flat2(ya), flat2(yb), flat2(yc), x, woa, wob, woc,
                g_post_mix, g_pre_mlp, g_post_mlp, w1, w2)
    return x, stacked, u


def _state_leaves(stacked, depth, bn, seq, n_keep):
    ka, va, kb, vb, ki2 = (t.reshape(depth, bn, seq, t.shape[-1]) for t in stacked)
    heads = lambda t: t.reshape(depth, bn, t.shape[2], N_HEADS, HEAD_DIM)
    return (heads(ka[:, :, seq - n_keep:]), heads(va[:, :, seq - n_keep:]), heads(kb), heads(vb),
            ki2[..., :IDX_DIM])


def kernel(x_prompt, x_sample, cache_a_k, cache_a_v, cache_b_k, cache_b_v, cache_b_kidx, state_pool, g_pre_mix, w_in, rel_bias, w_pool, pool_scale, w_out, g_post_mix, g_pre_mlp, w_ff1, w_ff2, g_post_mlp):
    batch, seq, _ = x_prompt.shape
    dec_batch, dec_seq, _ = x_sample.shape
    depth = w_in.shape[0]
    past_len = cache_b_k.shape[2]
    n_a = cache_a_k.shape[2]
    assert seq % KEY_TILE == 0 and (batch * seq) % ROW_TILE == 0
    assert (dec_batch * dec_seq) % ROW_TILE == 0 and n_a + dec_seq <= 3 * KEY_TILE
    assert dec_seq <= LANES

    tri = (jnp.arange(KEY_TILE)[:, None] >= jnp.arange(KEY_TILE)[None, :]).astype(BF16)
    xp = x_prompt.reshape(batch * seq, D_MODEL)
    xs = x_sample.reshape(dec_batch * dec_seq, D_MODEL)
    p_stacked = s_stacked = None
    p_pools, s_pools = [], []
    row = lambda t: t.reshape(1, -1)
    for l in range(depth):
        weights = (row(g_pre_mix[l]), *_proj_weight(w_in[l]), _pool_weight(w_pool[l]),
                   row(pool_scale[l]),
                   w_out[l, :ATT_WIDTH].astype(BF16),
                   w_out[l, ATT_WIDTH:2 * ATT_WIDTH].astype(BF16),
                   w_out[l, 2 * ATT_WIDTH:].astype(BF16),
                   row(g_post_mix[l]), row(g_pre_mlp[l]),
                   w_ff1[l].astype(BF16), w_ff2[l].astype(BF16), row(g_post_mlp[l]))
        xp, p_stacked, u = _layer(
            xp, weights, tri, p_stacked, layer=l, depth=depth, seq=seq,
            band_tab=_band_table(rel_bias[l], 2 * KEY_TILE, 0, KEY_TILE, 3 * KEY_TILE),
            band_shift=2, q_pos0=0)
        p_pools.append(u[:, seq - (POOL_MAX - 1):])
        xs, s_stacked, u = _layer(
            xs, weights, tri, s_stacked, layer=l, depth=depth, seq=dec_seq,
            band_tab=_band_table(rel_bias[l], past_len, past_len - n_a, LANES, n_a + dec_seq),
            band_shift=0, q_pos0=past_len,
            past=(cache_a_k[l], cache_a_v[l], cache_b_k[l], cache_b_v[l], cache_b_kidx[l],
                  state_pool[l]))
        s_pools.append(jnp.concatenate([state_pool[l], u], axis=1)[:, dec_seq:])
    return ((xp.reshape(batch, seq, D_MODEL), xs.reshape(dec_batch, dec_seq, D_MODEL))
            + _state_leaves(p_stacked, depth, batch, seq, min(A_WINDOW, seq))
            + (jnp.stack(p_pools, axis=0),)
            + _state_leaves(s_stacked, depth, dec_batch, dec_seq, dec_seq)
            + (jnp.stack(s_pools, axis=0),))
```

```python
import functools

import jax
import jax.numpy as jnp
from jax import lax
from jax.experimental import pallas as pl
from jax.experimental.pallas import tpu as pltpu

D_MODEL = 1024
CHUNK = 64
CHUNK_SHIFT = 6
HEAD_DIM = 64
N_HEADS = 6
ATT_WIDTH = N_HEADS * HEAD_DIM
N_PAIRS = N_HEADS // 2
C_WIDTH = 256
A_LEFT_CHUNKS = 8
A_WINDOW = A_LEFT_CHUNKS * CHUNK
REL_CLIP = 128
IDX_HEADS = 8
IDX_DIM = 64
TOPK = 256
POOL_WINDOWS = (2, 4, 8, 16)
POOL_GROUP = 64
POOL_MAX = 16
POOL_PAD = 32
D_FF = 4 * D_MODEL
RMS_EPS = 1e-6
IDX_SCALE = IDX_HEADS ** -0.5 * IDX_DIM ** -0.5
ATT_SCALE = HEAD_DIM ** -0.5

LANES = 128
SUBLANES = 8
KEY_TILE = 256
KEY_TILE_SHIFT = 8
ROW_TILE = 512
NEG = -1e30
LOG2E = 1.4426950408889634
VALUE_ROWS = HEAD_DIM + 16
STEP_GROUP = 4
FIRST_STEPS = 20
SEARCH_STEPS = 40
VMEM_LIMIT = 48 * 1024 * 1024

_OFF_KB, _OFF_QI, _OFF_KI, _OFF_WI, _OFF_U, _OFF_END = 1536, 2304, 2816, 2880, 2888, 3144

F32 = jnp.float32
BF16 = jnp.bfloat16


def _const_spec(shape):
    zeros = (0,) * len(shape)
    return pl.BlockSpec(shape, lambda *_: zeros, pipeline_mode=pl.Buffered(1))


def _params(semantics):
    return pltpu.CompilerParams(dimension_semantics=semantics, vmem_limit_bytes=VMEM_LIMIT)


def _rms(x, g):
    ms = jnp.mean(x * x, axis=-1, keepdims=True)
    return x * lax.rsqrt(ms + RMS_EPS) * g


def _dot(a, b):
    return jnp.dot(a, b, preferred_element_type=F32)


def _dot_t(a, b):
    return lax.dot_general(a, b, (((1,), (1,)), ((), ())), preferred_element_type=F32)


def _half_masks(rows):
    lane = lax.broadcasted_iota(jnp.int32, (rows, LANES), 1)
    first = lane < HEAD_DIM
    m0 = jnp.where(first, 1.0, 0.0).astype(BF16)
    m1 = jnp.where(first, 0.0, 1.0).astype(BF16)
    return first, m0, m1


def _fold(x, op):
    parts = [x[i:i + SUBLANES] for i in range(0, x.shape[0], SUBLANES)]
    lanes = min(4, len(parts))
    acc = parts[:lanes]
    for i, part in enumerate(parts[lanes:]):
        acc[i % lanes] = op(acc[i % lanes], part)
    while len(acc) > 1:
        acc = [op(acc[i], acc[i + 1]) for i in range(0, len(acc), 2)]
    return acc[0]


def _pair_loop(n, body, init):
    def pair(i, carry):
        return body(2 * i + 1, body(2 * i, carry, 0), 1)
    carry = lax.fori_loop(0, lax.shift_right_logical(n, 1), pair, init)
    return lax.cond(lax.rem(n, 2) == 1, lambda c: body(n - 1, c, 0), lambda c: c, carry)


def _proj_kernel(x_ref, g_ref, w_ref, wtail_ref, wt_ref, qa_ref, qb_ref, qi_ref, u_ref, wit_ref,
                 ka_ref, va_ref, kb_ref, vb_ref, ki2_ref):
    def put(ref, val):
        ref[0] = val
        if ref.shape[0] > 1:
            ref[1:] = jnp.zeros((ref.shape[0] - 1,) + val.shape, val.dtype)

    h = _rms(x_ref[...], g_ref[...]).astype(BF16)
    z = _dot(h, w_ref[:, 0:768])
    qa_ref[...] = (z[:, 0:384] * (ATT_SCALE * LOG2E)).astype(BF16)
    put(ka_ref, z[:, 384:768])
    z = _dot(h, w_ref[:, 768:1536])
    put(va_ref, z[:, 0:384])
    qb_ref[...] = (z[:, 384:768] * (ATT_SCALE * LOG2E)).astype(BF16)
    z = _dot(h, w_ref[:, 1536:2304])
    put(kb_ref, z[:, 0:384])
    put(vb_ref, z[:, 384:768])
    qi_ref[...] = _dot(h, w_ref[:, 2304:2816]).astype(BF16)
    z = _dot(h, wtail_ref[...])
    put(ki2_ref, z[:, 0:128])
    u_ref[...] = z[:, 128:384]
    wit_ref[...] = _dot_t(wt_ref[...], h)


N_PROJ_IN = 5
N_PROJ_PLAIN = 5
STACKED_WIDTHS = (ATT_WIDTH,) * 4 + (LANES,)


def _proj(x, g, w, wtail, wt, stacked, *, layer, depth):
    n = x.shape[0]
    assert n % ROW_TILE == 0
    row = lambda i: (i, 0)
    plain = ((ATT_WIDTH, BF16), (ATT_WIDTH, BF16), (IDX_HEADS * IDX_DIM, BF16), (C_WIDTH, F32))
    out_shape = [jax.ShapeDtypeStruct((n, wd), dt) for wd, dt in plain]
    out_specs = [pl.BlockSpec((ROW_TILE, wd), row) for wd, _ in plain]
    out_shape += [jax.ShapeDtypeStruct((LANES, n), F32)]
    out_specs += [pl.BlockSpec((LANES, ROW_TILE), lambda i: (0, i))]
    assert len(out_shape) == N_PROJ_PLAIN
    out_shape += [jax.ShapeDtypeStruct((depth, n, wd), F32) for wd in STACKED_WIDTHS]
    in_specs = [pl.BlockSpec((ROW_TILE, D_MODEL), row),
                _const_spec((1, D_MODEL)),
                _const_spec((D_MODEL, _OFF_KI)),
                _const_spec((D_MODEL, 2 * IDX_DIM + C_WIDTH)),
                _const_spec((LANES, D_MODEL))]
    if stacked is None:
        assert layer == 0
        operands, aliases, kern = (), {}, _proj_kernel
        out_specs += [pl.BlockSpec((depth, ROW_TILE, wd), lambda i: (0, i, 0))
                      for wd in STACKED_WIDTHS]
    else:
        operands = tuple(stacked)
        in_specs += [pl.BlockSpec(memory_space=pl.ANY)] * len(operands)
        aliases = {N_PROJ_IN + k: N_PROJ_PLAIN + k for k in range(len(operands))}
        out_specs += [pl.BlockSpec((1, ROW_TILE, wd), lambda i: (layer, i, 0))
                      for wd in STACKED_WIDTHS]

        def kern(*refs):
            _proj_kernel(*refs[:N_PROJ_IN], *refs[N_PROJ_IN + len(operands):])

    outs = pl.pallas_call(
        kern,
        out_shape=tuple(out_shape),
        grid=(n // ROW_TILE,),
        in_specs=in_specs,
        out_specs=tuple(out_specs),
        input_output_aliases=aliases,
        compiler_params=_params(("parallel",)),
        name="proj",
    )(x, g, w, wtail, wt, *operands)
    return outs[:N_PROJ_PLAIN], outs[N_PROJ_PLAIN:]


def _proj_weight(w_in):
    w_bf = w_in.astype(BF16)
    ki = w_bf[:, _OFF_KI:_OFF_WI]
    wtail = jnp.concatenate([ki, ki, w_bf[:, _OFF_U:_OFF_END]], axis=1)
    wi_t = jnp.pad(w_bf[:, _OFF_WI:_OFF_U].T, ((0, LANES - IDX_HEADS), (0, 0)))
    return w_bf[:, :_OFF_KI], wtail, wi_t


def _mask_heads(q_ref, qm_ref, heads, tq):
    _, m0, m1 = _half_masks(tq)
    for h in range(heads):
        pair = slice(LANES * (h // 2), LANES * (h // 2 + 1))
        qm_ref[h] = q_ref[0, :, pair] * (m0 if h % 2 == 0 else m1)


def _attend_scratch(tq):
    return [pltpu.VMEM((N_HEADS, VALUE_ROWS, tq), F32),
            pltpu.VMEM((ATT_WIDTH, tq), F32),
            pltpu.VMEM((2, N_HEADS, KEY_TILE, tq), F32),
            pltpu.VMEM((2, N_HEADS, KEY_TILE, tq), BF16)]


def _attend_init(acc_ref, tq):
    for h in range(N_HEADS):
        acc_ref[h] = jnp.zeros((VALUE_ROWS, tq), F32)
    return tuple(jnp.full((1, tq), NEG, F32) for _ in range(N_HEADS))


def _value_selectors():
    row = lax.broadcasted_iota(jnp.int32, (VALUE_ROWS, LANES), 0)
    lane = lax.broadcasted_iota(jnp.int32, (VALUE_ROWS, LANES), 1)
    pick = lambda off: jnp.where(row < HEAD_DIM, jnp.where(lane == row + off, 1.0, 0.0),
                                 0.0).astype(BF16)
    ones_rows = jnp.where(
        lax.broadcasted_iota(jnp.int32, (VALUE_ROWS, KEY_TILE), 0) >= HEAD_DIM, 1.0, 0.0)
    return pick(0), pick(HEAD_DIM), ones_rows


def _attend_tile(k_ref, v_ref, tile, bias_of_head, qm_ref, acc_ref, s_ref, p_ref, m_run, slot,
                 selectors):
    start = pl.multiple_of(tile * KEY_TILE, KEY_TILE)
    sel0, sel1, ones_rows = selectors
    m_new = []
    for pair in range(N_PAIRS):
        kt = k_ref[0, pl.ds(start, KEY_TILE), LANES * pair:LANES * (pair + 1)].astype(BF16)
        for h in (2 * pair, 2 * pair + 1):
            s = _dot_t(kt, qm_ref[h]) + bias_of_head(h)
            s_ref[slot, h] = s
            col_max = jnp.max(_fold(s, jnp.maximum), axis=0, keepdims=True)
            m_new.append(jnp.maximum(m_run[h], col_max))
    for h in range(N_HEADS):
        p_ref[slot, h] = jnp.exp2(s_ref[slot, h] - m_new[h]).astype(BF16)
        acc_ref[h] = jnp.exp2(m_run[h] - m_new[h]) * acc_ref[h]
    for pair in range(N_PAIRS):
        vp = v_ref[0, pl.ds(start, KEY_TILE), LANES * pair:LANES * (pair + 1)].astype(BF16)
        for h, sel in ((2 * pair, sel0), (2 * pair + 1, sel1)):
            vt = (_dot_t(sel, vp) + ones_rows).astype(BF16)
            acc_ref[h] = acc_ref[h] + _dot(vt, p_ref[slot, h])
    return tuple(m_new)


def _attend_finish(acc_ref, out_ref, o_ref):
    for h in range(N_HEADS):
        acc = acc_ref[h]
        out_ref[HEAD_DIM * h:HEAD_DIM * (h + 1), :] = acc[:HEAD_DIM] / acc[HEAD_DIM:HEAD_DIM + 1]
    o_ref[0] = out_ref[...].T.astype(BF16)


def _band_kernel(q_ref, k_ref, v_ref, tab_ref, o_ref, qm_ref, acc_ref, out_ref, s_ref, p_ref,
                 *, tq, shift):
    j = pl.program_id(1)
    _mask_heads(q_ref, qm_ref, N_HEADS, tq)
    selectors = _value_selectors()
    m_run = _attend_init(acc_ref, tq)
    for t in range(3):
        blk = j + (t - shift)
        bias = lambda h, t=t, blk=blk: jnp.where(blk >= 0, tab_ref[h, t], NEG)
        m_run = _attend_tile(k_ref, v_ref, jnp.maximum(blk, 0), bias, qm_ref, acc_ref, s_ref,
                             p_ref, m_run, t % 2, selectors)
    _attend_finish(acc_ref, out_ref, o_ref)


def _band(q, k, v, tab, *, tq, shift, k_batch0=0):
    bn, tq_total, _ = q.shape
    tk_total = k.shape[1]
    kern = functools.partial(_band_kernel, tq=tq, shift=shift)
    qmap = lambda b, j: (b, j, 0)
    kmap = lambda b, j: (k_batch0 + b, 0, 0)
    return pl.pallas_call(
        kern,
        out_shape=jax.ShapeDtypeStruct((bn, tq_total, ATT_WIDTH), BF16),
        grid=(bn, tq_total // tq),
        in_specs=[pl.BlockSpec((1, tq, ATT_WIDTH), qmap),
                  pl.BlockSpec((1, tk_total, ATT_WIDTH), kmap),
                  pl.BlockSpec((1, tk_total, ATT_WIDTH), kmap),
                  _const_spec(tab.shape)],
        out_specs=pl.BlockSpec((1, tq, ATT_WIDTH), qmap),
        scratch_shapes=[pltpu.VMEM((N_HEADS, tq, LANES), BF16),
                        *_attend_scratch(tq)],
        compiler_params=_params(("parallel", "arbitrary")),
        name="band",
    )(q, k, v, tab)


def _band_table(rel_bias, q_pos0, k_pos0, tq, n_real):
    tk = 3 * KEY_TILE
    span = tq + tk
    diff = (q_pos0 - k_pos0) - (tk - 1) + jnp.arange(span)
    vec = jnp.take(rel_bias.astype(F32), jnp.clip(diff, -REL_CLIP, REL_CLIP) + REL_CLIP, axis=1)
    h = vec.shape[0]
    skew = jnp.tile(vec, (1, tk))[:, :tk * (span - 1)].reshape(h, tk, span - 1)
    bias = skew[:, :, tk - 1:tk - 1 + tq]
    qc = (q_pos0 + jnp.arange(tq)) // CHUNK
    kc = (k_pos0 + jnp.arange(tk)) // CHUNK
    valid = ((kc[:, None] <= qc[None, :]) & (kc[:, None] >= qc[None, :] - A_LEFT_CHUNKS)
             & (jnp.arange(tk) < n_real)[:, None])
    tab = jnp.where(valid[None], bias * LOG2E, NEG)
    return tab.reshape(h, 3, KEY_TILE, tq)


def _sparse_kernel(qi_ref, wit_ref, ki2_ref, qb_ref, kb_ref, vb_ref, tri_ref, o_ref,
                   sc_ref, mb_ref, qim_ref, qbm_ref, acc_ref, out_ref, s_ref, p_ref,
                   *, tq, n_real, q_pos0, n_keys):
    j = pl.program_id(1)
    q_first = q_pos0 + j * n_real
    last_chunk = lax.shift_right_logical(q_first + (n_real - 1), CHUNK_SHIFT)
    k_end = jnp.minimum((last_chunk + 1) * CHUNK, n_keys)
    n_tiles = lax.shift_right_logical(k_end + (KEY_TILE - 1), KEY_TILE_SHIFT)

    lane = lax.broadcasted_iota(jnp.int32, (1, tq), 1)
    q_pos = q_first + lane
    k_lim = jnp.minimum((lax.shift_right_logical(q_pos, CHUNK_SHIFT) + 1) * CHUNK, n_keys)
    key_row = lax.broadcasted_iota(jnp.int32, (KEY_TILE, tq), 0)

    _mask_heads(qi_ref, qim_ref, IDX_HEADS, tq)
    _mask_heads(qb_ref, qbm_ref, N_HEADS, tq)
    w = wit_ref[0:SUBLANES, :] * IDX_SCALE

    def score_tile(t, carry, _):
        rmin8, rmax8 = carry
        start = pl.multiple_of(t * KEY_TILE, KEY_TILE)
        kt = ki2_ref[0, pl.ds(start, KEY_TILE), :].astype(BF16)
        sc = jnp.zeros((KEY_TILE, tq), F32)
        for h in range(IDX_HEADS):
            sc = sc + jnp.maximum(_dot_t(kt, qim_ref[h]), 0.0) * w[h:h + 1, :]
        adm = (start + key_row) < k_lim
        lowest = jnp.where(adm, sc, -jnp.inf)
        sc_ref[t] = lowest
        rmax8 = jnp.maximum(rmax8, _fold(lowest, jnp.maximum))
        rmin8 = jnp.minimum(rmin8, _fold(jnp.where(adm, sc, jnp.inf), jnp.minimum))
        return rmin8, rmax8

    rmin8, rmax8 = _pair_loop(
        n_tiles, score_tile,
        (jnp.full((SUBLANES, tq), jnp.inf, F32), jnp.full((SUBLANES, tq), -jnp.inf, F32)))
    rmin = jnp.min(rmin8, axis=0, keepdims=True)
    rmax = jnp.max(rmax8, axis=0, keepdims=True)

    def count_ge(thr):
        def body(t, c8):
            return c8 + _fold(jnp.where(sc_ref[t] >= thr, 1.0, 0.0), jnp.add)
        c8 = lax.fori_loop(0, n_tiles, body, jnp.zeros((SUBLANES, tq), F32))
        return jnp.sum(c8, axis=0, keepdims=True)

    def n_unsettled(c_lo):
        return jnp.sum(jnp.where(c_lo > TOPK, 1.0, 0.0))

    def search(carry, last_step):
        def cond(c):
            return jnp.logical_and(c[0] < last_step, c[1] > 0.0)

        def step(c):
            s, _, lo, hi, c_lo, c_hi = c
            for _ in range(STEP_GROUP):
                mid = lo + (hi - lo) * 0.5
                c_mid = count_ge(mid)
                unsettled = c_lo > TOPK
                enough = c_mid >= TOPK
                lo, c_lo = (jnp.where(unsettled, jnp.where(enough, mid, lo), lo),
                            jnp.where(unsettled, jnp.where(enough, c_mid, c_lo), c_lo))
                hi, c_hi = (jnp.where(unsettled, jnp.where(enough, hi, mid), hi),
                            jnp.where(unsettled, jnp.where(enough, c_hi, c_mid), c_hi))
            return s + STEP_GROUP, n_unsettled(c_lo), lo, hi, c_lo, c_hi

        return lax.while_loop(cond, step, carry)

    hi0 = rmax + jnp.maximum(jnp.abs(rmax), 1e-30) * (2.0 ** -10)
    c_lo0 = jnp.where(lane < n_real, k_lim, 0).astype(F32)
    steps, n_open, lo, hi, c_lo, c_hi = search(
        (jnp.int32(0), n_unsettled(c_lo0), rmin, hi0, c_lo0, jnp.zeros((1, tq), F32)),
        FIRST_STEPS)

    def n_untied():
        def body(t, carry):
            vmin8, vmax8 = carry
            x = sc_ref[t]
            vmax8 = jnp.maximum(vmax8, _fold(
                jnp.where(x >= lo, jnp.where(x < hi, x, -jnp.inf), -jnp.inf), jnp.maximum))
            vmin8 = jnp.minimum(vmin8, _fold(
                jnp.where(x >= lo, jnp.where(x < hi, x, jnp.inf), jnp.inf), jnp.minimum))
            return vmin8, vmax8
        vmin8, vmax8 = lax.fori_loop(
            0, n_tiles, body,
            (jnp.full((SUBLANES, tq), jnp.inf, F32), jnp.full((SUBLANES, tq), -jnp.inf, F32)))
        vmin = jnp.min(vmin8, axis=0, keepdims=True)
        vmax = jnp.max(vmax8, axis=0, keepdims=True)
        return jnp.sum(jnp.where(c_lo > TOPK, jnp.where(vmax == vmin, 0.0, 1.0), 0.0))

    n_open = lax.cond(n_open > 0.0, n_untied, lambda: jnp.float32(0.0))
    _, _, lo, hi, _, c_hi = search((steps, n_open, lo, hi, c_lo, c_hi), SEARCH_STEPS)

    need = TOPK - c_hi

    def mask_tile(t, run, _):
        x = sc_ref[t]
        inr = jnp.where(x >= lo, jnp.where(x < hi, 1.0, 0.0), 0.0)
        rank = _dot(tri_ref[...], inr.astype(BF16)) + run
        mb_ref[t] = jnp.where(
            x >= hi, 0.0, jnp.where(inr > 0.0, jnp.where(rank <= need, 0.0, NEG), NEG))
        return rank[KEY_TILE - 1:KEY_TILE, :]

    _pair_loop(n_tiles, mask_tile, jnp.zeros((1, tq), F32))

    selectors = _value_selectors()

    def attend(t, m_run, slot):
        return _attend_tile(kb_ref, vb_ref, t, lambda h: mb_ref[t], qbm_ref, acc_ref, s_ref, p_ref,
                            m_run, slot, selectors)

    _pair_loop(n_tiles, attend, _attend_init(acc_ref, tq))
    _attend_finish(acc_ref, out_ref, o_ref)


def _sparse(qi, wit, ki2, qb, kb, vb, tri, *, tq, n_real, q_pos0, n_keys, k_batch0=0):
    bn, tq_total, _ = qb.shape
    tk_total = kb.shape[1]
    assert tk_total % KEY_TILE == 0 and n_keys <= tk_total and tq % LANES == 0
    max_tiles = tk_total // KEY_TILE
    blocks = tq_total // tq
    kern = functools.partial(_sparse_kernel, tq=tq, n_real=n_real, q_pos0=q_pos0, n_keys=n_keys)
    qmap = lambda b, j: (b, j, 0)
    kmap = lambda b, j: (k_batch0 + b, 0, 0)
    return pl.pallas_call(
        kern,
        out_shape=jax.ShapeDtypeStruct((bn, tq_total, ATT_WIDTH), BF16),
        grid=(bn, blocks),
        in_specs=[pl.BlockSpec((1, tq, IDX_HEADS * IDX_DIM), qmap),
                  pl.BlockSpec((LANES, tq), lambda b, j: (0, b * blocks + j)),
                  pl.BlockSpec((1, tk_total, LANES), kmap),
                  pl.BlockSpec((1, tq, ATT_WIDTH), qmap),
                  pl.BlockSpec((1, tk_total, ATT_WIDTH), kmap),
                  pl.BlockSpec((1, tk_total, ATT_WIDTH), kmap),
                  _const_spec((KEY_TILE, KEY_TILE))],
        out_specs=pl.BlockSpec((1, tq, ATT_WIDTH), qmap),
        scratch_shapes=[pltpu.VMEM((max_tiles, KEY_TILE, tq), F32),
                        pltpu.VMEM((max_tiles, KEY_TILE, tq), F32),
                        pltpu.VMEM((IDX_HEADS, tq, LANES), BF16),
                        pltpu.VMEM((N_HEADS, tq, LANES), BF16),
                        *_attend_scratch(tq)],
        compiler_params=_params(("parallel", "arbitrary")),
        name="sparse",
    )(qi, wit, ki2, qb, kb, vb, tri)


def _pool_kernel(u_ref, prev_ref, w_ref, scale_ref, o_ref, f_ref, a_ref, b_ref, *, t, pos0):
    n = t + POOL_PAD
    u = u_ref[0]
    f_ref[0:16, :] = jnp.zeros((16, C_WIDTH), F32)
    f_ref[16:32, :] = prev_ref[0]
    f_ref[pl.ds(POOL_PAD, t), :] = u
    s2 = f_ref[pl.ds(8, n - 8), :] + f_ref[pl.ds(7, n - 8), :]
    a_ref[pl.ds(8, n - 8), :] = s2
    s4 = a_ref[pl.ds(16, n - 16), :] + a_ref[pl.ds(14, n - 16), :]
    b_ref[pl.ds(16, n - 16), :] = s4
    s8 = b_ref[pl.ds(24, n - 24), :] + b_ref[pl.ds(20, n - 24), :]
    a_ref[pl.ds(24, n - 24), :] = s8
    s16 = a_ref[pl.ds(32, t), :] + a_ref[pl.ds(24, t), :]
    lane = lax.broadcasted_iota(jnp.int32, (t, C_WIDTH), 1)
    g0, g1, g2 = lane < POOL_GROUP, lane < 2 * POOL_GROUP, lane < 3 * POOL_GROUP
    total = jnp.where(g0, s2[24:], jnp.where(g1, s4[16:], jnp.where(g2, s8[8:], s16)))
    win = jnp.where(g0, POOL_WINDOWS[0],
                    jnp.where(g1, POOL_WINDOWS[1],
                              jnp.where(g2, POOL_WINDOWS[2], POOL_WINDOWS[3])))
    pos = pos0 + lax.broadcasted_iota(jnp.int32, (t, C_WIDTH), 0)
    cnt = jnp.minimum(pos + 1, win).astype(F32)
    pooled = total / cnt - u
    o_ref[0] = (_dot(pooled.astype(BF16), w_ref[...]) * scale_ref[...]).astype(BF16)


def _pool(u, prev16, w_blk, scale, *, pos0):
    bn, t, _ = u.shape
    kern = functools.partial(_pool_kernel, t=t, pos0=pos0)
    bmap = lambda b: (b, 0, 0)
    n = t + POOL_PAD
    return pl.pallas_call(
        kern,
        out_shape=jax.ShapeDtypeStruct((bn, t, C_WIDTH), BF16),
        grid=(bn,),
        in_specs=[pl.BlockSpec((1, t, C_WIDTH), bmap),
                  pl.BlockSpec((1, POOL_MAX, C_WIDTH), bmap),
                  _const_spec((C_WIDTH, C_WIDTH)),
                  _const_spec((1, C_WIDTH))],
        out_specs=pl.BlockSpec((1, t, C_WIDTH), bmap),
        scratch_shapes=[pltpu.VMEM((n, C_WIDTH), F32)] * 3,
        compiler_params=_params(("parallel",)),
        name="pool",
    )(u, prev16, w_blk, scale)


def _pool_weight(w_pool):
    w = jnp.zeros((C_WIDTH, C_WIDTH), w_pool.dtype)
    for g in range(len(POOL_WINDOWS)):
        sl = slice(g * POOL_GROUP, (g + 1) * POOL_GROUP)
        w = w.at[sl, sl].set(w_pool[g])
    return w.astype(BF16)


def _outmlp_kernel(ya_ref, yb_ref, yc_ref, x_ref, woa_ref, wob_ref, woc_ref,
                   g1_ref, g2_ref, g3_ref, w1_ref, w2_ref, o_ref):
    y = _dot(ya_ref[...], woa_ref[...]) + _dot(yb_ref[...], wob_ref[...])
    y = y + _dot(yc_ref[...], woc_ref[...])
    x1 = x_ref[...] + _rms(y, g1_ref[...])
    h = _rms(x1, g2_ref[...]).astype(BF16)
    m = jnp.zeros_like(x1)
    for f in range(D_FF // D_MODEL):
        sl = slice(f * D_MODEL, (f + 1) * D_MODEL)
        a = jnp.maximum(_dot(h, w1_ref[:, sl]), 0.0)
        m = m + _dot((a * a).astype(BF16), w2_ref[sl, :])
    o_ref[...] = x1 + _rms(m, g3_ref[...])


def _outmlp(ya, yb, yc, x, woa, wob, woc, g1, g2, g3, w1, w2):
    n = x.shape[0]
    assert n % ROW_TILE == 0
    row = lambda i: (i, 0)
    return pl.pallas_call(
        _outmlp_kernel,
        out_shape=jax.ShapeDtypeStruct((n, D_MODEL), F32),
        grid=(n // ROW_TILE,),
        in_specs=[pl.BlockSpec((ROW_TILE, ATT_WIDTH), row),
                  pl.BlockSpec((ROW_TILE, ATT_WIDTH), row),
                  pl.BlockSpec((ROW_TILE, C_WIDTH), row),
                  pl.BlockSpec((ROW_TILE, D_MODEL), row),
                  _const_spec((ATT_WIDTH, D_MODEL)),
                  _const_spec((ATT_WIDTH, D_MODEL)),
                  _const_spec((C_WIDTH, D_MODEL)),
                  _const_spec((1, D_MODEL)),
                  _const_spec((1, D_MODEL)),
                  _const_spec((1, D_MODEL)),
                  _const_spec((D_MODEL, D_FF)),
                  _const_spec((D_FF, D_MODEL))],
        out_specs=pl.BlockSpec((ROW_TILE, D_MODEL), row),
        compiler_params=_params(("parallel",)),
        name="outmlp",
    )(ya, yb, yc, x, woa, wob, woc, g1, g2, g3, w1, w2)


def _pad_axis(x, axis, size):
    pads = [(0, 0)] * x.ndim
    pads[axis] = (0, size - x.shape[axis])
    return jnp.pad(x, pads)


def _layer(x, weights, tri, stacked, *, layer, depth, seq, band_tab, band_shift, q_pos0,
           past=None):
    (g_pre_mix, w_proj, w_proj_tail, w_proj_t, w_pool_blk, pool_scale, woa, wob, woc, g_post_mix,
     g_pre_mlp, w1, w2, g_post_mlp) = weights
    bn = x.shape[0] // seq
    (qa, qb, qi, u, wit), stacked = _proj(x, g_pre_mix, w_proj, w_proj_tail, w_proj_t, stacked,
                                          layer=layer, depth=depth)
    per_batch = lambda t: t.reshape(bn, seq, t.shape[-1])
    qa, qb, qi, u = map(per_batch, (qa, qb, qi, u))
    ka_all, va_all, kb_all, vb_all, ki2_all = (t.reshape(depth * bn, seq, t.shape[-1])
                                               for t in stacked)

    if past is None:
        k_batch0 = layer * bn
        prev16 = jnp.zeros((bn, POOL_MAX, C_WIDTH), F32)
        n_keys = seq
        tq, qa_q, qi_q, qb_q, wit_q = KEY_TILE, qa, qi, qb, wit
    else:
        c_a_k, c_a_v, c_b_k, c_b_v, c_b_kidx, c_pool = past
        k_batch0 = 0
        ka, va, kb, vb, ki2 = (t[layer].reshape(bn, seq, t.shape[-1]) for t in stacked)
        flat = lambda t: t.reshape(bn, t.shape[1], ATT_WIDTH)
        band_rows = 3 * KEY_TILE
        ka_all = _pad_axis(jnp.concatenate([flat(c_a_k), ka], axis=1), 1, band_rows)
        va_all = _pad_axis(jnp.concatenate([flat(c_a_v), va], axis=1), 1, band_rows)
        n_keys = c_b_k.shape[1] + seq
        key_rows = -(-n_keys // KEY_TILE) * KEY_TILE
        kb_all = _pad_axis(jnp.concatenate([flat(c_b_k), kb], axis=1), 1, key_rows)
        vb_all = _pad_axis(jnp.concatenate([flat(c_b_v), vb], axis=1), 1, key_rows)
        c_ki2 = jnp.concatenate([c_b_kidx, c_b_kidx], axis=-1)
        ki2_all = _pad_axis(jnp.concatenate([c_ki2, ki2], axis=1), 1, key_rows)
        prev16 = jnp.pad(c_pool, ((0, 0), (1, 0), (0, 0)))
        tq = LANES
        qa_q, qi_q, qb_q = (_pad_axis(t, 1, LANES) for t in (qa, qi, qb))
        wit_q = _pad_axis(wit.reshape(LANES, bn, seq), 2, LANES).reshape(LANES, bn * LANES)

    ya = _band(qa_q, ka_all, va_all, band_tab, tq=tq, shift=band_shift,
               k_batch0=k_batch0)[:, :seq]
    yb = _sparse(qi_q, wit_q, ki2_all, qb_q, kb_all, vb_all, tri, tq=tq, n_real=min(seq, tq),
                 q_pos0=q_pos0, n_keys=n_keys, k_batch0=k_batch0)[:, :seq]
    yc = _pool(u, prev16, w_pool_blk, pool_scale, pos0=q_pos0)
    flat2 = lambda t: t.reshape(bn * seq, t.shape[-1])
    x = _outmlp(flat2(ya), flat2(yb), flat2(yc), x, woa, wob, woc,
                g_post_mix, g_pre_mlp, g_post_mlp, w1, w2)
    return x, stacked, u


def _state_leaves(stacked, depth, bn, seq, n_keep):
    ka, va, kb, vb, ki2 = (t.reshape(depth, bn, seq, t.shape[-1]) for t in stacked)
    heads = lambda t: t.reshape(depth, bn, t.shape[2], N_HEADS, HEAD_DIM)
    return (heads(ka[:, :, seq - n_keep:]), heads(va[:, :, seq - n_keep:]), heads(kb), heads(vb),
            ki2[..., :IDX_DIM])


def kernel(x_prompt, x_sample, cache_a_k, cache_a_v, cache_b_k, cache_b_v, cache_b_kidx, state_pool, g_pre_mix, w_in, rel_bias, w_pool, pool_scale, w_out, g_post_mix, g_pre_mlp, w_ff1, w_ff2, g_post_mlp):
    batch, seq, _ = x_prompt.shape
    dec_batch, dec_seq, _ = x_sample.shape
    depth = w_in.shape[0]
    past_len = cache_b_k.shape[2]
    n_a = cache_a_k.shape[2]
    assert seq % KEY_TILE == 0 and (batch * seq) % ROW_TILE == 0
    assert (dec_batch * dec_seq) % ROW_TILE == 0 and n_a + dec_seq <= 3 * KEY_TILE
    assert dec_seq <= LANES

    tri = (jnp.arange(KEY_TILE)[:, None] >= jnp.arange(KEY_TILE)[None, :]).astype(BF16)
    xp = x_prompt.reshape(batch * seq, D_MODEL)
    xs = x_sample.reshape(dec_batch * dec_seq, D_MODEL)
    p_stacked = s_stacked = None
    p_pools, s_pools = [], []
    row = lambda t: t.reshape(1, -1)
    for l in range(depth):
        weights = (row(g_pre_mix[l]), *_proj_weight(w_in[l]), _pool_weight(w_pool[l]),
                   row(pool_scale[l]),
                   w_out[l, :ATT_WIDTH].astype(BF16),
                   w_out[l, ATT_WIDTH:2 * ATT_WIDTH].astype(BF16),
                   w_out[l, 2 * ATT_WIDTH:].astype(BF16),
                   row(g_post_mix[l]), row(g_pre_mlp[l]),
                   w_ff1[l].astype(BF16), w_ff2[l].astype(BF16), row(g_post_mlp[l]))
        xp, p_stacked, u = _layer(
            xp, weights, tri, p_stacked, layer=l, depth=depth, seq=seq,
            band_tab=_band_table(rel_bias[l], 2 * KEY_TILE, 0, KEY_TILE, 3 * KEY_TILE),
            band_shift=2, q_pos0=0)
        p_pools.append(u[:, seq - (POOL_MAX - 1):])
        xs, s_stacked, u = _layer(
            xs, weights, tri, s_stacked, layer=l, depth=depth, seq=dec_seq,
            band_tab=_band_table(rel_bias[l], past_len, past_len - n_a, LANES, n_a + dec_seq),
            band_shift=0, q_pos0=past_len,
            past=(cache_a_k[l], cache_a_v[l], cache_b_k[l], cache_b_v[l], cache_b_kidx[l],
                  state_pool[l]))
        s_pools.append(jnp.concatenate([state_pool[l], u], axis=1)[:, dec_seq:])
    return ((xp.reshape(batch, seq, D_MODEL), xs.reshape(dec_batch, dec_seq, D_MODEL))
            + _state_leaves(p_stacked, depth, batch, seq, min(A_WINDOW, seq))
            + (jnp.stack(p_pools, axis=0),)
            + _state_leaves(s_stacked, depth, dec_batch, dec_seq, dec_seq)
            + (jnp.stack(s_pools, axis=0),))
```

```python
import functools

import jax
import jax.numpy as jnp
from jax import lax
from jax.experimental import pallas as pl
from jax.experimental.pallas import tpu as pltpu

D_MODEL = 1024
CHUNK = 64
CHUNK_SHIFT = 6
HEAD_DIM = 64
N_HEADS = 6
ATT_WIDTH = N_HEADS * HEAD_DIM
N_PAIRS = N_HEADS // 2
C_WIDTH = 256
A_LEFT_CHUNKS = 8
A_WINDOW = A_LEFT_CHUNKS * CHUNK
REL_CLIP = 128
IDX_HEADS = 8
IDX_DIM = 64
TOPK = 256
POOL_WINDOWS = (2, 4, 8, 16)
POOL_GROUP = 64
POOL_MAX = 16
POOL_PAD = 32
D_FF = 4 * D_MODEL
RMS_EPS = 1e-6
IDX_SCALE = IDX_HEADS ** -0.5 * IDX_DIM ** -0.5
ATT_SCALE = HEAD_DIM ** -0.5

LANES = 128
SUBLANES = 8
KEY_TILE = 256
KEY_TILE_SHIFT = 8
ROW_TILE = 512
NEG = -1e30
LOG2E = 1.4426950408889634
VALUE_ROWS = HEAD_DIM + 16
STEP_GROUP = 4
FIRST_STEPS = 20
SEARCH_STEPS = 40
VMEM_LIMIT = 48 * 1024 * 1024

_OFF_KB, _OFF_QI, _OFF_KI, _OFF_WI, _OFF_U, _OFF_END = 1536, 2304, 2816, 2880, 2888, 3144

F32 = jnp.float32
BF16 = jnp.bfloat16


def _const_spec(shape):
    zeros = (0,) * len(shape)
    return pl.BlockSpec(shape, lambda *_: zeros, pipeline_mode=pl.Buffered(1))


def _params(semantics):
    return pltpu.CompilerParams(dimension_semantics=semantics, vmem_limit_bytes=VMEM_LIMIT)


def _rms(x, g):
    ms = jnp.mean(x * x, axis=-1, keepdims=True)
    return x * lax.rsqrt(ms + RMS_EPS) * g


def _dot(a, b):
    return jnp.dot(a, b, preferred_element_type=F32)


def _dot_t(a, b):
    return lax.dot_general(a, b, (((1,), (1,)), ((), ())), preferred_element_type=F32)


def _half_masks(rows):
    lane = lax.broadcasted_iota(jnp.int32, (rows, LANES), 1)
    first = lane < HEAD_DIM
    m0 = jnp.where(first, 1.0, 0.0).astype(BF16)
    m1 = jnp.where(first, 0.0, 1.0).astype(BF16)
    return first, m0, m1


def _fold(x, op):
    parts = [x[i:i + SUBLANES] for i in range(0, x.shape[0], SUBLANES)]
    lanes = min(4, len(parts))
    acc = parts[:lanes]
    for i, part in enumerate(parts[lanes:]):
        acc[i % lanes] = op(acc[i % lanes], part)
    while len(acc) > 1:
        acc = [op(acc[i], acc[i + 1]) for i in range(0, len(acc), 2)]
    return acc[0]


def _pair_loop(n, body, init):
    def pair(i, carry):
        return body(2 * i + 1, body(2 * i, carry, 0), 1)
    carry = lax.fori_loop(0, lax.shift_right_logical(n, 1), pair, init)
    return lax.cond(lax.rem(n, 2) == 1, lambda c: body(n - 1, c, 0), lambda c: c, carry)


def _proj_kernel(x_ref, g_ref, w_ref, wtail_ref, wt_ref, qa_ref, qb_ref, qi_ref, u_ref, wit_ref,
                 ka_ref, va_ref, kb_ref, vb_ref, ki2_ref):
    def put(ref, val):
        ref[0] = val
        if ref.shape[0] > 1:
            ref[1:] = jnp.zeros((ref.shape[0] - 1,) + val.shape, val.dtype)

    h = _rms(x_ref[...], g_ref[...]).astype(BF16)
    z = _dot(h, w_ref[:, 0:768])
    qa_ref[...] = (z[:, 0:384] * (ATT_SCALE * LOG2E)).astype(BF16)
    put(ka_ref, z[:, 384:768])
    z = _dot(h, w_ref[:, 768:1536])
    put(va_ref, z[:, 0:384])
    qb_ref[...] = (z[:, 384:768] * (ATT_SCALE * LOG2E)).astype(BF16)
    z = _dot(h, w_ref[:, 1536:2304])
    put(kb_ref, z[:, 0:384])
    put(vb_ref, z[:, 384:768])
    qi_ref[...] = _dot(h, w_ref[:, 2304:2816]).astype(BF16)
    z = _dot(h, wtail_ref[...])
    put(ki2_ref, z[:, 0:128])
    u_ref[...] = z[:, 128:384]
    wit_ref[...] = _dot_t(wt_ref[...], h)


N_PROJ_IN = 5
N_PROJ_PLAIN = 5
STACKED_WIDTHS = (ATT_WIDTH,) * 4 + (LANES,)


def _proj(x, g, w, wtail, wt, stacked, *, layer, depth):
    n = x.shape[0]
    assert n % ROW_TILE == 0
    row = lambda i: (i, 0)
    plain = ((ATT_WIDTH, BF16), (ATT_WIDTH, BF16), (IDX_HEADS * IDX_DIM, BF16), (C_WIDTH, F32))
    out_shape = [jax.ShapeDtypeStruct((n, wd), dt) for wd, dt in plain]
    out_specs = [pl.BlockSpec((ROW_TILE, wd), row) for wd, _ in plain]
    out_shape += [jax.ShapeDtypeStruct((LANES, n), F32)]
    out_specs += [pl.BlockSpec((LANES, ROW_TILE), lambda i: (0, i))]
    assert len(out_shape) == N_PROJ_PLAIN
    out_shape += [jax.ShapeDtypeStruct((depth, n, wd), F32) for wd in STACKED_WIDTHS]
    in_specs = [pl.BlockSpec((ROW_TILE, D_MODEL), row),
                _const_spec((1, D_MODEL)),
                _const_spec((D_MODEL, _OFF_KI)),
                _const_spec((D_MODEL, 2 * IDX_DIM + C_WIDTH)),
                _const_spec((LANES, D_MODEL))]
    if stacked is None:
        assert layer == 0
        operands, aliases, kern = (), {}, _proj_kernel
        out_specs += [pl.BlockSpec((depth, ROW_TILE, wd), lambda i: (0, i, 0))
                      for wd in STACKED_WIDTHS]
    else:
        operands = tuple(stacked)
        in_specs += [pl.BlockSpec(memory_space=pl.ANY)] * len(operands)
        aliases = {N_PROJ_IN + k: N_PROJ_PLAIN + k for k in range(len(operands))}
        out_specs += [pl.BlockSpec((1, ROW_TILE, wd), lambda i: (layer, i, 0))
                      for wd in STACKED_WIDTHS]

        def kern(*refs):
            _proj_kernel(*refs[:N_PROJ_IN], *refs[N_PROJ_IN + len(operands):])

    outs = pl.pallas_call(
        kern,
        out_shape=tuple(out_shape),
        grid=(n // ROW_TILE,),
        in_specs=in_specs,
        out_specs=tuple(out_specs),
        input_output_aliases=aliases,
        compiler_params=_params(("parallel",)),
        name="proj",
    )(x, g, w, wtail, wt, *operands)
    return outs[:N_PROJ_PLAIN], outs[N_PROJ_PLAIN:]


def _proj_weight(w_in):
    w_bf = w_in.astype(BF16)
    ki = w_bf[:, _OFF_KI:_OFF_WI]
    wtail = jnp.concatenate([ki, ki, w_bf[:, _OFF_U:_OFF_END]], axis=1)
    wi_t = jnp.pad(w_bf[:, _OFF_WI:_OFF_U].T, ((0, LANES - IDX_HEADS), (0, 0)))
    return w_bf[:, :_OFF_KI], wtail, wi_t


def _mask_heads(q_ref, qm_ref, heads, tq):
    _, m0, m1 = _half_masks(tq)
    for h in range(heads):
        pair = slice(LANES * (h // 2), LANES * (h // 2 + 1))
        qm_ref[h] = q_ref[0, :, pair] * (m0 if h % 2 == 0 else m1)


def _attend_scratch(tq):
    return [pltpu.VMEM((N_HEADS, VALUE_ROWS, tq), F32),
            pltpu.VMEM((ATT_WIDTH, tq), F32),
            pltpu.VMEM((2, N_HEADS, KEY_TILE, tq), F32),
            pltpu.VMEM((2, N_HEADS, KEY_TILE, tq), BF16)]


def _attend_init(acc_ref, tq):
    for h in range(N_HEADS):
        acc_ref[h] = jnp.zeros((VALUE_ROWS, tq), F32)
    return tuple(jnp.full((1, tq), NEG, F32) for _ in range(N_HEADS))


def _value_selectors():
    row = lax.broadcasted_iota(jnp.int32, (VALUE_ROWS, LANES), 0)
    lane = lax.broadcasted_iota(jnp.int32, (VALUE_ROWS, LANES), 1)
    pick = lambda off: jnp.where(row < HEAD_DIM, jnp.where(lane == row + off, 1.0, 0.0),
                                 0.0).astype(BF16)
    ones_rows = jnp.where(
        lax.broadcasted_iota(jnp.int32, (VALUE_ROWS, KEY_TILE), 0) >= HEAD_DIM, 1.0, 0.0)
    return pick(0), pick(HEAD_DIM), ones_rows


def _attend_tile(k_ref, v_ref, tile, bias_of_head, qm_ref, acc_ref, s_ref, p_ref, m_run, slot,
                 selectors):
    start = pl.multiple_of(tile * KEY_TILE, KEY_TILE)
    sel0, sel1, ones_rows = selectors
    m_new = []
    for pair in range(N_PAIRS):
        kt = k_ref[0, pl.ds(start, KEY_TILE), LANES * pair:LANES * (pair + 1)].astype(BF16)
        for h in (2 * pair, 2 * pair + 1):
            s = _dot_t(kt, qm_ref[h]) + bias_of_head(h)
            s_ref[slot, h] = s
            col_max = jnp.max(_fold(s, jnp.maximum), axis=0, keepdims=True)
            m_new.append(jnp.maximum(m_run[h], col_max))
    for h in range(N_HEADS):
        p_ref[slot, h] = jnp.exp2(s_ref[slot, h] - m_new[h]).astype(BF16)
        acc_ref[h] = jnp.exp2(m_run[h] - m_new[h]) * acc_ref[h]
    for pair in range(N_PAIRS):
        vp = v_ref[0, pl.ds(start, KEY_TILE), LANES * pair:LANES * (pair + 1)].astype(BF16)
        for h, sel in ((2 * pair, sel0), (2 * pair + 1, sel1)):
            vt = (_dot_t(sel, vp) + ones_rows).astype(BF16)
            acc_ref[h] = acc_ref[h] + _dot(vt, p_ref[slot, h])
    return tuple(m_new)


def _attend_finish(acc_ref, out_ref, o_ref):
    for h in range(N_HEADS):
        acc = acc_ref[h]
        out_ref[HEAD_DIM * h:HEAD_DIM * (h + 1), :] = acc[:HEAD_DIM] / acc[HEAD_DIM:HEAD_DIM + 1]
    o_ref[0] = out_ref[...].T.astype(BF16)


def _band_kernel(q_ref, k_ref, v_ref, tab_ref, o_ref, qm_ref, acc_ref, out_ref, s_ref, p_ref,
                 *, tq, shift):
    j = pl.program_id(1)
    _mask_heads(q_ref, qm_ref, N_HEADS, tq)
    selectors = _value_selectors()
    m_run = _attend_init(acc_ref, tq)
    for t in range(3):
        blk = j + (t - shift)
        bias = lambda h, t=t, blk=blk: jnp.where(blk >= 0, tab_ref[h, t], NEG)
        m_run = _attend_tile(k_ref, v_ref, jnp.maximum(blk, 0), bias, qm_ref, acc_ref, s_ref,
                             p_ref, m_run, t % 2, selectors)
    _attend_finish(acc_ref, out_ref, o_ref)


def _band(q, k, v, tab, *, tq, shift, k_batch0=0):
    bn, tq_total, _ = q.shape
    tk_total = k.shape[1]
    kern = functools.partial(_band_kernel, tq=tq, shift=shift)
    qmap = lambda b, j: (b, j, 0)
    kmap = lambda b, j: (k_batch0 + b, 0, 0)
    return pl.pallas_call(
        kern,
        out_shape=jax.ShapeDtypeStruct((bn, tq_total, ATT_WIDTH), BF16),
        grid=(bn, tq_total // tq),
        in_specs=[pl.BlockSpec((1, tq, ATT_WIDTH), qmap),
                  pl.BlockSpec((1, tk_total, ATT_WIDTH), kmap),
                  pl.BlockSpec((1, tk_total, ATT_WIDTH), kmap),
                  _const_spec(tab.shape)],
        out_specs=pl.BlockSpec((1, tq, ATT_WIDTH), qmap),
        scratch_shapes=[pltpu.VMEM((N_HEADS, tq, LANES), BF16),
                        *_attend_scratch(tq)],
        compiler_params=_params(("parallel", "arbitrary")),
        name="band",
    )(q, k, v, tab)


def _band_table(rel_bias, q_pos0, k_pos0, tq, n_real):
    tk = 3 * KEY_TILE
    span = tq + tk - 1
    diff = (q_pos0 - k_pos0) - (tk - 1) + jnp.arange(span)
    vec = jnp.take(rel_bias.astype(F32), jnp.clip(diff, -REL_CLIP, REL_CLIP) + REL_CLIP, axis=1)
    h = vec.shape[0]
    skew = jnp.tile(vec, (1, tq + 1))[:, :tq * (span + 1)].reshape(h, tq, span + 1)
    bias = jnp.flip(skew[:, :, :tk], axis=-1)
    qc = (q_pos0 + jnp.arange(tq)) // CHUNK
    kc = (k_pos0 + jnp.arange(tk)) // CHUNK
    valid = ((kc[None, :] <= qc[:, None]) & (kc[None, :] >= qc[:, None] - A_LEFT_CHUNKS)
             & (jnp.arange(tk) < n_real)[None, :])
    tab = jnp.where(valid[None], bias * LOG2E, NEG)
    return tab.reshape(h, tq, 3, KEY_TILE).transpose(0, 2, 3, 1)


def _sparse_kernel(qi_ref, wit_ref, ki2_ref, qb_ref, kb_ref, vb_ref, tri_ref, o_ref,
                   sc_ref, mb_ref, qim_ref, qbm_ref, acc_ref, out_ref, s_ref, p_ref,
                   *, tq, n_real, q_pos0, n_keys):
    j = pl.program_id(1)
    q_first = q_pos0 + j * n_real
    last_chunk = lax.shift_right_logical(q_first + (n_real - 1), CHUNK_SHIFT)
    k_end = jnp.minimum((last_chunk + 1) * CHUNK, n_keys)
    n_tiles = lax.shift_right_logical(k_end + (KEY_TILE - 1), KEY_TILE_SHIFT)

    lane = lax.broadcasted_iota(jnp.int32, (1, tq), 1)
    q_pos = q_first + lane
    k_lim = jnp.minimum((lax.shift_right_logical(q_pos, CHUNK_SHIFT) + 1) * CHUNK, n_keys)
    key_row = lax.broadcasted_iota(jnp.int32, (KEY_TILE, tq), 0)

    _mask_heads(qi_ref, qim_ref, IDX_HEADS, tq)
    _mask_heads(qb_ref, qbm_ref, N_HEADS, tq)
    w = wit_ref[0:SUBLANES, :] * IDX_SCALE

    def score_tile(t, carry, _):
        rmin8, rmax8 = carry
        start = pl.multiple_of(t * KEY_TILE, KEY_TILE)
        kt = ki2_ref[0, pl.ds(start, KEY_TILE), :].astype(BF16)
        sc = jnp.zeros((KEY_TILE, tq), F32)
        for h in range(IDX_HEADS):
            sc = sc + jnp.maximum(_dot_t(kt, qim_ref[h]), 0.0) * w[h:h + 1, :]
        adm = (start + key_row) < k_lim
        lowest = jnp.where(adm, sc, -jnp.inf)
        sc_ref[t] = lowest
        rmax8 = jnp.maximum(rmax8, _fold(lowest, jnp.maximum))
        rmin8 = jnp.minimum(rmin8, _fold(jnp.where(adm, sc, jnp.inf), jnp.minimum))
        return rmin8, rmax8

    rmin8, rmax8 = _pair_loop(
        n_tiles, score_tile,
        (jnp.full((SUBLANES, tq), jnp.inf, F32), jnp.full((SUBLANES, tq), -jnp.inf, F32)))
    rmin = jnp.min(rmin8, axis=0, keepdims=True)
    rmax = jnp.max(rmax8, axis=0, keepdims=True)

    def count_ge(thr):
        def body(t, c8):
            return c8 + _fold(jnp.where(sc_ref[t] >= thr, 1.0, 0.0), jnp.add)
        c8 = lax.fori_loop(0, n_tiles, body, jnp.zeros((SUBLANES, tq), F32))
        return jnp.sum(c8, axis=0, keepdims=True)

    def n_unsettled(c_lo):
        return jnp.sum(jnp.where(c_lo > TOPK, 1.0, 0.0))

    def search(carry, last_step):
        def cond(c):
            return jnp.logical_and(c[0] < last_step, c[1] > 0.0)

        def step(c):
            s, _, lo, hi, c_lo, c_hi = c
            for _ in range(STEP_GROUP):
                mid = lo + (hi - lo) * 0.5
                c_mid = count_ge(mid)
                unsettled = c_lo > TOPK
                enough = c_mid >= TOPK
                lo, c_lo = (jnp.where(unsettled, jnp.where(enough, mid, lo), lo),
                            jnp.where(unsettled, jnp.where(enough, c_mid, c_lo), c_lo))
                hi, c_hi = (jnp.where(unsettled, jnp.where(enough, hi, mid), hi),
                            jnp.where(unsettled, jnp.where(enough, c_hi, c_mid), c_hi))
            return s + STEP_GROUP, n_unsettled(c_lo), lo, hi, c_lo, c_hi

        return lax.while_loop(cond, step, carry)

    hi0 = rmax + jnp.maximum(jnp.abs(rmax), 1e-30) * (2.0 ** -10)
    c_lo0 = jnp.where(lane < n_real, k_lim, 0).astype(F32)
    steps, n_open, lo, hi, c_lo, c_hi = search(
        (jnp.int32(0), n_unsettled(c_lo0), rmin, hi0, c_lo0, jnp.zeros((1, tq), F32)),
        FIRST_STEPS)

    def n_untied():
        def body(t, carry):
            vmin8, vmax8 = carry
            x = sc_ref[t]
            vmax8 = jnp.maximum(vmax8, _fold(
                jnp.where(x >= lo, jnp.where(x < hi, x, -jnp.inf), -jnp.inf), jnp.maximum))
            vmin8 = jnp.minimum(vmin8, _fold(
                jnp.where(x >= lo, jnp.where(x < hi, x, jnp.inf), jnp.inf), jnp.minimum))
            return vmin8, vmax8
        vmin8, vmax8 = lax.fori_loop(
            0, n_tiles, body,
            (jnp.full((SUBLANES, tq), jnp.inf, F32), jnp.full((SUBLANES, tq), -jnp.inf, F32)))
        vmin = jnp.min(vmin8, axis=0, keepdims=True)
        vmax = jnp.max(vmax8, axis=0, keepdims=True)
        return jnp.sum(jnp.where(c_lo > TOPK, jnp.where(vmax == vmin, 0.0, 1.0), 0.0))

    n_open = lax.cond(n_open > 0.0, n_untied, lambda: jnp.float32(0.0))
    _, _, lo, hi, _, c_hi = search((steps, n_open, lo, hi, c_lo, c_hi), SEARCH_STEPS)

    need = TOPK - c_hi

    def mask_tile(t, run, _):
        x = sc_ref[t]
        inr = jnp.where(x >= lo, jnp.where(x < hi, 1.0, 0.0), 0.0)
        rank = _dot(tri_ref[...], inr.astype(BF16)) + run
        mb_ref[t] = jnp.where(
            x >= hi, 0.0, jnp.where(inr > 0.0, jnp.where(rank <= need, 0.0, NEG), NEG))
        return rank[KEY_TILE - 1:KEY_TILE, :]

    _pair_loop(n_tiles, mask_tile, jnp.zeros((1, tq), F32))

    selectors = _value_selectors()

    def attend(t, m_run, slot):
        return _attend_tile(kb_ref, vb_ref, t, lambda h: mb_ref[t], qbm_ref, acc_ref, s_ref, p_ref,
                            m_run, slot, selectors)

    _pair_loop(n_tiles, attend, _attend_init(acc_ref, tq))
    _attend_finish(acc_ref, out_ref, o_ref)


def _sparse(qi, wit, ki2, qb, kb, vb, tri, *, tq, n_real, q_pos0, n_keys, k_batch0=0):
    bn, tq_total, _ = qb.shape
    tk_total = kb.shape[1]
    assert tk_total % KEY_TILE == 0 and n_keys <= tk_total and tq % LANES == 0
    max_tiles = tk_total // KEY_TILE
    blocks = tq_total // tq
    kern = functools.partial(_sparse_kernel, tq=tq, n_real=n_real, q_pos0=q_pos0, n_keys=n_keys)
    qmap = lambda b, j: (b, j, 0)
    kmap = lambda b, j: (k_batch0 + b, 0, 0)
    return pl.pallas_call(
        kern,
        out_shape=jax.ShapeDtypeStruct((bn, tq_total, ATT_WIDTH), BF16),
        grid=(bn, blocks),
        in_specs=[pl.BlockSpec((1, tq, IDX_HEADS * IDX_DIM), qmap),
                  pl.BlockSpec((LANES, tq), lambda b, j: (0, b * blocks + j)),
                  pl.BlockSpec((1, tk_total, LANES), kmap),
                  pl.BlockSpec((1, tq, ATT_WIDTH), qmap),
                  pl.BlockSpec((1, tk_total, ATT_WIDTH), kmap),
                  pl.BlockSpec((1, tk_total, ATT_WIDTH), kmap),
                  _const_spec((KEY_TILE, KEY_TILE))],
        out_specs=pl.BlockSpec((1, tq, ATT_WIDTH), qmap),
        scratch_shapes=[pltpu.VMEM((max_tiles, KEY_TILE, tq), F32),
                        pltpu.VMEM((max_tiles, KEY_TILE, tq), F32),
                        pltpu.VMEM((IDX_HEADS, tq, LANES), BF16),
                        pltpu.VMEM((N_HEADS, tq, LANES), BF16),
                        *_attend_scratch(tq)],
        compiler_params=_params(("parallel", "arbitrary")),
        name="sparse",
    )(qi, wit, ki2, qb, kb, vb, tri)


def _pool_kernel(u_ref, prev_ref, w_ref, scale_ref, o_ref, f_ref, a_ref, b_ref, *, t, pos0):
    n = t + POOL_PAD
    u = u_ref[0]
    f_ref[0:16, :] = jnp.zeros((16, C_WIDTH), F32)
    f_ref[16:32, :] = prev_ref[0]
    f_ref[pl.ds(POOL_PAD, t), :] = u
    s2 = f_ref[pl.ds(8, n - 8), :] + f_ref[pl.ds(7, n - 8), :]
    a_ref[pl.ds(8, n - 8), :] = s2
    s4 = a_ref[pl.ds(16, n - 16), :] + a_ref[pl.ds(14, n - 16), :]
    b_ref[pl.ds(16, n - 16), :] = s4
    s8 = b_ref[pl.ds(24, n - 24), :] + b_ref[pl.ds(20, n - 24), :]
    a_ref[pl.ds(24, n - 24), :] = s8
    s16 = a_ref[pl.ds(32, t), :] + a_ref[pl.ds(24, t), :]
    lane = lax.broadcasted_iota(jnp.int32, (t, C_WIDTH), 1)
    g0, g1, g2 = lane < POOL_GROUP, lane < 2 * POOL_GROUP, lane < 3 * POOL_GROUP
    total = jnp.where(g0, s2[24:], jnp.where(g1, s4[16:], jnp.where(g2, s8[8:], s16)))
    win = jnp.where(g0, POOL_WINDOWS[0],
                    jnp.where(g1, POOL_WINDOWS[1],
                              jnp.where(g2, POOL_WINDOWS[2], POOL_WINDOWS[3])))
    pos = pos0 + lax.broadcasted_iota(jnp.int32, (t, C_WIDTH), 0)
    cnt = jnp.minimum(pos + 1, win).astype(F32)
    pooled = total / cnt - u
    o_ref[0] = (_dot(pooled.astype(BF16), w_ref[...]) * scale_ref[...]).astype(BF16)


def _pool(u, prev16, w_blk, scale, *, pos0):
    bn, t, _ = u.shape
    kern = functools.partial(_pool_kernel, t=t, pos0=pos0)
    bmap = lambda b: (b, 0, 0)
    n = t + POOL_PAD
    return pl.pallas_call(
        kern,
        out_shape=jax.ShapeDtypeStruct((bn, t, C_WIDTH), BF16),
        grid=(bn,),
        in_specs=[pl.BlockSpec((1, t, C_WIDTH), bmap),
                  pl.BlockSpec((1, POOL_MAX, C_WIDTH), bmap),
                  _const_spec((C_WIDTH, C_WIDTH)),
                  _const_spec((1, C_WIDTH))],
        out_specs=pl.BlockSpec((1, t, C_WIDTH), bmap),
        scratch_shapes=[pltpu.VMEM((n, C_WIDTH), F32)] * 3,
        compiler_params=_params(("parallel",)),
        name="pool",
    )(u, prev16, w_blk, scale)


def _pool_weight(w_pool):
    w = jnp.zeros((C_WIDTH, C_WIDTH), w_pool.dtype)
    for g in range(len(POOL_WINDOWS)):
        sl = slice(g * POOL_GROUP, (g + 1) * POOL_GROUP)
        w = w.at[sl, sl].set(w_pool[g])
    return w.astype(BF16)


def _outmlp_kernel(ya_ref, yb_ref, yc_ref, x_ref, woa_ref, wob_ref, woc_ref,
                   g1_ref, g2_ref, g3_ref, w1_ref, w2_ref, o_ref):
    y = _dot(ya_ref[...], woa_ref[...]) + _dot(yb_ref[...], wob_ref[...])
    y = y + _dot(yc_ref[...], woc_ref[...])
    x1 = x_ref[...] + _rms(y, g1_ref[...])
    h = _rms(x1, g2_ref[...]).astype(BF16)
    m = jnp.zeros_like(x1)
    for f in range(D_FF // D_MODEL):
        sl = slice(f * D_MODEL, (f + 1) * D_MODEL)
        a = jnp.maximum(_dot(h, w1_ref[:, sl]), 0.0)
        m = m + _dot((a * a).astype(BF16), w2_ref[sl, :])
    o_ref[...] = x1 + _rms(m, g3_ref[...])


def _outmlp(ya, yb, yc, x, woa, wob, woc, g1, g2, g3, w1, w2):
    n = x.shape[0]
    assert n % ROW_TILE == 0
    row = lambda i: (i, 0)
    return pl.pallas_call(
        _outmlp_kernel,
        out_shape=jax.ShapeDtypeStruct((n, D_MODEL), F32),
        grid=(n // ROW_TILE,),
        in_specs=[pl.BlockSpec((ROW_TILE, ATT_WIDTH), row),
                  pl.BlockSpec((ROW_TILE, ATT_WIDTH), row),
                  pl.BlockSpec((ROW_TILE, C_WIDTH), row),
                  pl.BlockSpec((ROW_TILE, D_MODEL), row),
                  _const_spec((ATT_WIDTH, D_MODEL)),
                  _const_spec((ATT_WIDTH, D_MODEL)),
                  _const_spec((C_WIDTH, D_MODEL)),
                  _const_spec((1, D_MODEL)),
                  _const_spec((1, D_MODEL)),
                  _const_spec((1, D_MODEL)),
                  _const_spec((D_MODEL, D_FF)),
                  _const_spec((D_FF, D_MODEL))],
        out_specs=pl.BlockSpec((ROW_TILE, D_MODEL), row),
        compiler_params=_params(("parallel",)),
        name="outmlp",
    )(ya, yb, yc, x, woa, wob, woc, g1, g2, g3, w1, w2)


def _pad_axis(x, axis, size):
    pads = [(0, 0)] * x.ndim
    pads[axis] = (0, size - x.shape[axis])
    return jnp.pad(x, pads)


def _layer(x, weights, tri, stacked, *, layer, depth, seq, band_tab, band_shift, q_pos0,
           past=None):
    (g_pre_mix, w_proj, w_proj_tail, w_proj_t, w_pool_blk, pool_scale, woa, wob, woc, g_post_mix,
     g_pre_mlp, w1, w2, g_post_mlp) = weights
    bn = x.shape[0] // seq
    (qa, qb, qi, u, wit), stacked = _proj(x, g_pre_mix, w_proj, w_proj_tail, w_proj_t, stacked,
                                          layer=layer, depth=depth)
    per_batch = lambda t: t.reshape(bn, seq, t.shape[-1])
    qa, qb, qi, u = map(per_batch, (qa, qb, qi, u))
    ka_all, va_all, kb_all, vb_all, ki2_all = (t.reshape(depth * bn, seq, t.shape[-1])
                                               for t in stacked)

    if past is None:
        k_batch0 = layer * bn
        prev16 = jnp.zeros((bn, POOL_MAX, C_WIDTH), F32)
        n_keys = seq
        tq, qa_q, qi_q, qb_q, wit_q = KEY_TILE, qa, qi, qb, wit
    else:
        c_a_k, c_a_v, c_b_k, c_b_v, c_b_kidx, c_pool = past
        k_batch0 = 0
        ka, va, kb, vb, ki2 = (t[layer].reshape(bn, seq, t.shape[-1]) for t in stacked)
        flat = lambda t: t.reshape(bn, t.shape[1], ATT_WIDTH)
        band_rows = 3 * KEY_TILE
        ka_all = _pad_axis(jnp.concatenate([flat(c_a_k), ka], axis=1), 1, band_rows)
        va_all = _pad_axis(jnp.concatenate([flat(c_a_v), va], axis=1), 1, band_rows)
        n_keys = c_b_k.shape[1] + seq
        key_rows = -(-n_keys // KEY_TILE) * KEY_TILE
        kb_all = _pad_axis(jnp.concatenate([flat(c_b_k), kb], axis=1), 1, key_rows)
        vb_all = _pad_axis(jnp.concatenate([flat(c_b_v), vb], axis=1), 1, key_rows)
        c_ki2 = jnp.concatenate([c_b_kidx, c_b_kidx], axis=-1)
        ki2_all = _pad_axis(jnp.concatenate([c_ki2, ki2], axis=1), 1, key_rows)
        prev16 = jnp.pad(c_pool, ((0, 0), (1, 0), (0, 0)))
        tq = LANES
        qa_q, qi_q, qb_q = (_pad_axis(t, 1, LANES) for t in (qa, qi, qb))
        wit_q = _pad_axis(wit.reshape(LANES, bn, seq), 2, LANES).reshape(LANES, bn * LANES)

    ya = _band(qa_q, ka_all, va_all, band_tab, tq=tq, shift=band_shift,
               k_batch0=k_batch0)[:, :seq]
    yb = _sparse(qi_q, wit_q, ki2_all, qb_q, kb_all, vb_all, tri, tq=tq, n_real=min(seq, tq),
                 q_pos0=q_pos0, n_keys=n_keys, k_batch0=k_batch0)[:, :seq]
    yc = _pool(u, prev16, w_pool_blk, pool_scale, pos0=q_pos0)
    flat2 = lambda t: t.reshape(bn * seq, t.shape[-1])
    x = _outmlp(flat2(ya), flat2(yb), flat2(yc), x, woa, wob, woc,
                g_post_mix, g_pre_mlp, g_post_mlp, w1, w2)
    return x, stacked, u


def _state_leaves(stacked, depth, bn, seq, n_keep):
    ka, va, kb, vb, ki2 = (t.reshape(depth, bn, seq, t.shape[-1]) for t in stacked)
    heads = lambda t: t.reshape(depth, bn, t.shape[2], N_HEADS, HEAD_DIM)
    return (heads(ka[:, :, seq - n_keep:]), heads(va[:, :, seq - n_keep:]), heads(kb), heads(vb),
            ki2[..., :IDX_DIM])


def kernel(x_prompt, x_sample, cache_a_k, cache_a_v, cache_b_k, cache_b_v, cache_b_kidx, state_pool, g_pre_mix, w_in, rel_bias, w_pool, pool_scale, w_out, g_post_mix, g_pre_mlp, w_ff1, w_ff2, g_post_mlp):
    batch, seq, _ = x_prompt.shape
    dec_batch, dec_seq, _ = x_sample.shape
    depth = w_in.shape[0]
    past_len = cache_b_k.shape[2]
    n_a = cache_a_k.shape[2]
    assert seq % KEY_TILE == 0 and (batch * seq) % ROW_TILE == 0
    assert (dec_batch * dec_seq) % ROW_TILE == 0 and n_a + dec_seq <= 3 * KEY_TILE
    assert dec_seq <= LANES

    tri = (jnp.arange(KEY_TILE)[:, None] >= jnp.arange(KEY_TILE)[None, :]).astype(BF16)
    xp = x_prompt.reshape(batch * seq, D_MODEL)
    xs = x_sample.reshape(dec_batch * dec_seq, D_MODEL)
    p_stacked = s_stacked = None
    p_pools, s_pools = [], []
    row = lambda t: t.reshape(1, -1)
    for l in range(depth):
        weights = (row(g_pre_mix[l]), *_proj_weight(w_in[l]), _pool_weight(w_pool[l]),
                   row(pool_scale[l]),
                   w_out[l, :ATT_WIDTH].astype(BF16),
                   w_out[l, ATT_WIDTH:2 * ATT_WIDTH].astype(BF16),
                   w_out[l, 2 * ATT_WIDTH:].astype(BF16),
                   row(g_post_mix[l]), row(g_pre_mlp[l]),
                   w_ff1[l].astype(BF16), w_ff2[l].astype(BF16), row(g_post_mlp[l]))
        xp, p_stacked, u = _layer(
            xp, weights, tri, p_stacked, layer=l, depth=depth, seq=seq,
            band_tab=_band_table(rel_bias[l], 2 * KEY_TILE, 0, KEY_TILE, 3 * KEY_TILE),
            band_shift=2, q_pos0=0)
        p_pools.append(u[:, seq - (POOL_MAX - 1):])
        xs, s_stacked, u = _layer(
            xs, weights, tri, s_stacked, layer=l, depth=depth, seq=dec_seq,
            band_tab=_band_table(rel_bias[l], past_len, past_len - n_a, LANES, n_a + dec_seq),
            band_shift=0, q_pos0=past_len,
            past=(cache_a_k[l], cache_a_v[l], cache_b_k[l], cache_b_v[l], cache_b_kidx[l],
                  state_pool[l]))
        s_pools.append(jnp.concatenate([state_pool[l], u], axis=1)[:, dec_seq:])
    return ((xp.reshape(batch, seq, D_MODEL), xs.reshape(dec_batch, dec_seq, D_MODEL))
            + _state_leaves(p_stacked, depth, batch, seq, min(A_WINDOW, seq))
            + (jnp.stack(p_pools, axis=0),)
            + _state_leaves(s_stacked, depth, dec_batch, dec_seq, dec_seq)
            + (jnp.stack(s_pools, axis=0),))
```

```python
import functools

import jax
import jax.numpy as jnp
from jax import lax
from jax.experimental import pallas as pl
from jax.experimental.pallas import tpu as pltpu

D_MODEL = 1024
CHUNK = 64
CHUNK_SHIFT = 6
HEAD_DIM = 64
N_HEADS = 6
ATT_WIDTH = N_HEADS * HEAD_DIM
N_PAIRS = N_HEADS // 2
C_WIDTH = 256
A_LEFT_CHUNKS = 8
A_WINDOW = A_LEFT_CHUNKS * CHUNK
REL_CLIP = 128
IDX_HEADS = 8
IDX_DIM = 64
TOPK = 256
POOL_WINDOWS = (2, 4, 8, 16)
POOL_GROUP = 64
POOL_MAX = 16
POOL_PAD = 32
D_FF = 4 * D_MODEL
RMS_EPS = 1e-6
IDX_SCALE = IDX_HEADS ** -0.5 * IDX_DIM ** -0.5
ATT_SCALE = HEAD_DIM ** -0.5

LANES = 128
SUBLANES = 8
KEY_TILE = 256
KEY_TILE_SHIFT = 8
ROW_TILE = 512
NEG = -1e30
LOG2E = 1.4426950408889634
VALUE_ROWS = HEAD_DIM + 16
STEP_GROUP = 4
FIRST_STEPS = 20
SEARCH_STEPS = 40
VMEM_LIMIT = 48 * 1024 * 1024

_OFF_KI, _OFF_WI, _OFF_U, _OFF_END = 2816, 2880, 2888, 3144

F32 = jnp.float32
BF16 = jnp.bfloat16


def _const_spec(shape):
    zeros = (0,) * len(shape)
    return pl.BlockSpec(shape, lambda *_: zeros, pipeline_mode=pl.Buffered(1))


def _params(semantics):
    return pltpu.CompilerParams(dimension_semantics=semantics, vmem_limit_bytes=VMEM_LIMIT)


def _rms(x, g):
    ms = jnp.mean(x * x, axis=-1, keepdims=True)
    return x * lax.rsqrt(ms + RMS_EPS) * g


def _dot(a, b):
    return jnp.dot(a, b, preferred_element_type=F32)


def _dot_t(a, b):
    return lax.dot_general(a, b, (((1,), (1,)), ((), ())), preferred_element_type=F32)


def _half_masks(rows):
    lane = lax.broadcasted_iota(jnp.int32, (rows, LANES), 1)
    first = lane < HEAD_DIM
    m0 = jnp.where(first, 1.0, 0.0).astype(BF16)
    m1 = jnp.where(first, 0.0, 1.0).astype(BF16)
    return first, m0, m1


def _fold(x, op):
    parts = [x[i:i + SUBLANES] for i in range(0, x.shape[0], SUBLANES)]
    lanes = min(4, len(parts))
    acc = parts[:lanes]
    for i, part in enumerate(parts[lanes:]):
        acc[i % lanes] = op(acc[i % lanes], part)
    while len(acc) > 1:
        acc = [op(acc[i], acc[i + 1]) for i in range(0, len(acc), 2)]
    return acc[0]


def _pair_loop(n, body, init):
    def pair(i, carry):
        return body(2 * i + 1, body(2 * i, carry, 0), 1)
    carry = lax.fori_loop(0, lax.shift_right_logical(n, 1), pair, init)
    return lax.cond(lax.rem(n, 2) == 1, lambda c: body(n - 1, c, 0), lambda c: c, carry)


def _proj_kernel(x_ref, g_ref, w_ref, wtail_ref, wt_ref, qa_ref, qb_ref, qi_ref, u_ref, wit_ref,
                 ka_ref, va_ref, kb_ref, vb_ref, ki2_ref):
    def put(ref, val):
        ref[0] = val
        if ref.shape[0] > 1:
            ref[1:] = jnp.zeros((ref.shape[0] - 1,) + val.shape, val.dtype)

    h = _rms(x_ref[...], g_ref[...]).astype(BF16)
    z = _dot(h, w_ref[:, 0:768])
    qa_ref[...] = (z[:, 0:384] * (ATT_SCALE * LOG2E)).astype(BF16)
    put(ka_ref, z[:, 384:768])
    z = _dot(h, w_ref[:, 768:1536])
    put(va_ref, z[:, 0:384])
    qb_ref[...] = (z[:, 384:768] * (ATT_SCALE * LOG2E)).astype(BF16)
    z = _dot(h, w_ref[:, 1536:2304])
    put(kb_ref, z[:, 0:384])
    put(vb_ref, z[:, 384:768])
    qi_ref[...] = _dot(h, w_ref[:, 2304:2816]).astype(BF16)
    z = _dot(h, wtail_ref[...])
    put(ki2_ref, z[:, 0:128])
    u_ref[...] = z[:, 128:384]
    wit_ref[...] = _dot_t(wt_ref[...], h)


N_PROJ_IN = 5
N_PROJ_PLAIN = 5
STACKED_WIDTHS = (ATT_WIDTH,) * 4 + (LANES,)


def _proj(x, g, w, wtail, wt, stacked, *, layer, depth):
    n = x.shape[0]
    assert n % ROW_TILE == 0
    row = lambda i: (i, 0)
    plain = ((ATT_WIDTH, BF16), (ATT_WIDTH, BF16), (IDX_HEADS * IDX_DIM, BF16), (C_WIDTH, F32))
    out_shape = [jax.ShapeDtypeStruct((n, wd), dt) for wd, dt in plain]
    out_specs = [pl.BlockSpec((ROW_TILE, wd), row) for wd, _ in plain]
    out_shape += [jax.ShapeDtypeStruct((LANES, n), F32)]
    out_specs += [pl.BlockSpec((LANES, ROW_TILE), lambda i: (0, i))]
    assert len(out_shape) == N_PROJ_PLAIN
    out_shape += [jax.ShapeDtypeStruct((depth, n, wd), F32) for wd in STACKED_WIDTHS]
    in_specs = [pl.BlockSpec((ROW_TILE, D_MODEL), row),
                _const_spec((1, D_MODEL)),
                _const_spec((D_MODEL, _OFF_KI)),
                _const_spec((D_MODEL, 2 * IDX_DIM + C_WIDTH)),
                _const_spec((LANES, D_MODEL))]
    if stacked is None:
        assert layer == 0
        operands, aliases, kern = (), {}, _proj_kernel
        out_specs += [pl.BlockSpec((depth, ROW_TILE, wd), lambda i: (0, i, 0))
                      for wd in STACKED_WIDTHS]
    else:
        operands = tuple(stacked)
        in_specs += [pl.BlockSpec(memory_space=pl.ANY)] * len(operands)
        aliases = {N_PROJ_IN + k: N_PROJ_PLAIN + k for k in range(len(operands))}
        out_specs += [pl.BlockSpec((1, ROW_TILE, wd), lambda i: (layer, i, 0))
                      for wd in STACKED_WIDTHS]

        def kern(*refs):
            _proj_kernel(*refs[:N_PROJ_IN], *refs[N_PROJ_IN + len(operands):])

    outs = pl.pallas_call(
        kern,
        out_shape=tuple(out_shape),
        grid=(n // ROW_TILE,),
        in_specs=in_specs,
        out_specs=tuple(out_specs),
        input_output_aliases=aliases,
        compiler_params=_params(("parallel",)),
        name="proj",
    )(x, g, w, wtail, wt, *operands)
    return outs[:N_PROJ_PLAIN], outs[N_PROJ_PLAIN:]


def _proj_weight(w_in):
    w_bf = w_in.astype(BF16)
    ki = w_bf[:, _OFF_KI:_OFF_WI]
    wtail = jnp.concatenate([ki, ki, w_bf[:, _OFF_U:_OFF_END]], axis=1)
    wi_t = jnp.pad(w_bf[:, _OFF_WI:_OFF_U].T, ((0, LANES - IDX_HEADS), (0, 0)))
    return w_bf[:, :_OFF_KI], wtail, wi_t


def _mask_heads(q_ref, qm_ref, heads, tq):
    _, m0, m1 = _half_masks(tq)
    for h in range(heads):
        pair = slice(LANES * (h // 2), LANES * (h // 2 + 1))
        qm_ref[h] = q_ref[0, :, pair] * (m0 if h % 2 == 0 else m1)


def _attend_scratch(tq):
    return [pltpu.VMEM((N_HEADS, VALUE_ROWS, tq), F32),
            pltpu.VMEM((ATT_WIDTH, tq), F32),
            pltpu.VMEM((2, N_HEADS, KEY_TILE, tq), F32),
            pltpu.VMEM((2, N_HEADS, KEY_TILE, tq), BF16)]


def _attend_init(acc_ref, tq):
    for h in range(N_HEADS):
        acc_ref[h] = jnp.zeros((VALUE_ROWS, tq), F32)
    return tuple(jnp.full((1, tq), NEG, F32) for _ in range(N_HEADS))


def _value_selectors():
    row = lax.broadcasted_iota(jnp.int32, (VALUE_ROWS, LANES), 0)
    lane = lax.broadcasted_iota(jnp.int32, (VALUE_ROWS, LANES), 1)
    pick = lambda off: jnp.where(row < HEAD_DIM, jnp.where(lane == row + off, 1.0, 0.0),
                                 0.0).astype(BF16)
    ones_rows = jnp.where(
        lax.broadcasted_iota(jnp.int32, (VALUE_ROWS, KEY_TILE), 0) >= HEAD_DIM, 1.0, 0.0)
    return pick(0), pick(HEAD_DIM), ones_rows


def _attend_tile(k_ref, v_ref, tile, bias_of_head, qm_ref, acc_ref, s_ref, p_ref, m_run, slot,
                 selectors):
    start = pl.multiple_of(tile * KEY_TILE, KEY_TILE)
    sel0, sel1, ones_rows = selectors
    m_new = []
    for pair in range(N_PAIRS):
        kt = k_ref[0, pl.ds(start, KEY_TILE), LANES * pair:LANES * (pair + 1)].astype(BF16)
        for h in (2 * pair, 2 * pair + 1):
            s = _dot_t(kt, qm_ref[h]) + bias_of_head(h)
            s_ref[slot, h] = s
            col_max = jnp.max(_fold(s, jnp.maximum), axis=0, keepdims=True)
            m_new.append(jnp.maximum(m_run[h], col_max))
    for h in range(N_HEADS):
        p_ref[slot, h] = jnp.exp2(s_ref[slot, h] - m_new[h]).astype(BF16)
        acc_ref[h] = jnp.exp2(m_run[h] - m_new[h]) * acc_ref[h]
    for pair in range(N_PAIRS):
        vp = v_ref[0, pl.ds(start, KEY_TILE), LANES * pair:LANES * (pair + 1)].astype(BF16)
        for h, sel in ((2 * pair, sel0), (2 * pair + 1, sel1)):
            vt = (_dot_t(sel, vp) + ones_rows).astype(BF16)
            acc_ref[h] = acc_ref[h] + _dot(vt, p_ref[slot, h])
    return tuple(m_new)


def _attend_finish(acc_ref, out_ref, o_ref):
    for h in range(N_HEADS):
        acc = acc_ref[h]
        out_ref[HEAD_DIM * h:HEAD_DIM * (h + 1), :] = acc[:HEAD_DIM] / acc[HEAD_DIM:HEAD_DIM + 1]
    o_ref[0] = out_ref[...].T.astype(BF16)


def _band_kernel(q_ref, k_ref, v_ref, tab_ref, o_ref, qm_ref, acc_ref, out_ref, s_ref, p_ref,
                 *, tq, shift):
    j = pl.program_id(1)
    _mask_heads(q_ref, qm_ref, N_HEADS, tq)
    selectors = _value_selectors()
    m_run = _attend_init(acc_ref, tq)
    for t in range(3):
        blk = j + (t - shift)
        if t >= shift:
            bias = lambda h, t=t: tab_ref[h, t]
        else:
            bias = lambda h, t=t, blk=blk: jnp.where(blk >= 0, tab_ref[h, t], NEG)
        m_run = _attend_tile(k_ref, v_ref, jnp.maximum(blk, 0), bias, qm_ref, acc_ref, s_ref,
                             p_ref, m_run, t % 2, selectors)
    _attend_finish(acc_ref, out_ref, o_ref)


def _band(q, k, v, tab, *, tq, shift, k_batch0=0):
    bn, tq_total, _ = q.shape
    tk_total = k.shape[1]
    kern = functools.partial(_band_kernel, tq=tq, shift=shift)
    qmap = lambda b, j: (b, j, 0)
    kmap = lambda b, j: (k_batch0 + b, 0, 0)
    return pl.pallas_call(
        kern,
        out_shape=jax.ShapeDtypeStruct((bn, tq_total, ATT_WIDTH), BF16),
        grid=(bn, tq_total // tq),
        in_specs=[pl.BlockSpec((1, tq, ATT_WIDTH), qmap),
                  pl.BlockSpec((1, tk_total, ATT_WIDTH), kmap),
                  pl.BlockSpec((1, tk_total, ATT_WIDTH), kmap),
                  _const_spec(tab.shape)],
        out_specs=pl.BlockSpec((1, tq, ATT_WIDTH), qmap),
        scratch_shapes=[pltpu.VMEM((N_HEADS, tq, LANES), BF16),
                        *_attend_scratch(tq)],
        compiler_params=_params(("parallel", "arbitrary")),
        name="band",
    )(q, k, v, tab)


def _band_table(rel_bias, q_pos0, k_pos0, tq, n_real):
    tk = 3 * KEY_TILE
    span = tq + tk - 1
    diff = (q_pos0 - k_pos0) - (tk - 1) + jnp.arange(span)
    vec = jnp.take(rel_bias.astype(F32), jnp.clip(diff, -REL_CLIP, REL_CLIP) + REL_CLIP, axis=1)
    h = vec.shape[0]
    skew = jnp.tile(vec, (1, tq + 1))[:, :tq * (span + 1)].reshape(h, tq, span + 1)
    bias = jnp.flip(skew[:, :, :tk], axis=-1)
    qc = (q_pos0 + jnp.arange(tq)) // CHUNK
    kc = (k_pos0 + jnp.arange(tk)) // CHUNK
    valid = ((kc[None, :] <= qc[:, None]) & (kc[None, :] >= qc[:, None] - A_LEFT_CHUNKS)
             & (jnp.arange(tk) < n_real)[None, :])
    tab = jnp.where(valid[None], bias * LOG2E, NEG)
    return tab.reshape(h, tq, 3, KEY_TILE).transpose(0, 2, 3, 1)


def _sparse_kernel(qi_ref, wit_ref, ki2_ref, qb_ref, kb_ref, vb_ref, tri_ref, o_ref,
                   sc_ref, mb_ref, qim_ref, qbm_ref, acc_ref, out_ref, s_ref, p_ref,
                   *, tq, n_real, q_pos0, n_keys):
    j = pl.program_id(1)
    q_first = q_pos0 + j * n_real
    last_chunk = lax.shift_right_logical(q_first + (n_real - 1), CHUNK_SHIFT)
    k_end = jnp.minimum((last_chunk + 1) * CHUNK, n_keys)
    n_tiles = lax.shift_right_logical(k_end + (KEY_TILE - 1), KEY_TILE_SHIFT)

    lane = lax.broadcasted_iota(jnp.int32, (1, tq), 1)
    q_pos = q_first + lane
    k_lim = jnp.minimum((lax.shift_right_logical(q_pos, CHUNK_SHIFT) + 1) * CHUNK, n_keys)
    key_row = lax.broadcasted_iota(jnp.int32, (KEY_TILE, tq), 0)

    _mask_heads(qi_ref, qim_ref, IDX_HEADS, tq)
    _mask_heads(qb_ref, qbm_ref, N_HEADS, tq)
    w = wit_ref[0:SUBLANES, :] * IDX_SCALE

    def score_tile(t, carry, _):
        rmin8, rmax8 = carry
        start = pl.multiple_of(t * KEY_TILE, KEY_TILE)
        kt = ki2_ref[0, pl.ds(start, KEY_TILE), :].astype(BF16)
        sc = jnp.zeros((KEY_TILE, tq), F32)
        for h in range(IDX_HEADS):
            sc = sc + jnp.maximum(_dot_t(kt, qim_ref[h]), 0.0) * w[h:h + 1, :]
        adm = (start + key_row) < k_lim
        lowest = jnp.where(adm, sc, -jnp.inf)
        sc_ref[t] = lowest
        rmax8 = jnp.maximum(rmax8, _fold(lowest, jnp.maximum))
        rmin8 = jnp.minimum(rmin8, _fold(jnp.where(adm, sc, jnp.inf), jnp.minimum))
        return rmin8, rmax8

    rmin8, rmax8 = _pair_loop(
        n_tiles, score_tile,
        (jnp.full((SUBLANES, tq), jnp.inf, F32), jnp.full((SUBLANES, tq), -jnp.inf, F32)))
    rmin = jnp.min(rmin8, axis=0, keepdims=True)
    rmax = jnp.max(rmax8, axis=0, keepdims=True)

    def count_ge(thr):
        def body(t, c8):
            return c8 + _fold(jnp.where(sc_ref[t] >= thr, 1.0, 0.0), jnp.add)
        c8 = lax.fori_loop(0, n_tiles, body, jnp.zeros((SUBLANES, tq), F32))
        return jnp.sum(c8, axis=0, keepdims=True)

    def n_unsettled(c_lo):
        return jnp.sum(jnp.where(c_lo > TOPK, 1.0, 0.0))

    def search(carry, last_step):
        def cond(c):
            return jnp.logical_and(c[0] < last_step, c[1] > 0.0)

        def step(c):
            s, _, lo, hi, c_lo, c_hi = c
            for _ in range(STEP_GROUP):
                mid = lo + (hi - lo) * 0.5
                c_mid = count_ge(mid)
                unsettled = c_lo > TOPK
                enough = c_mid >= TOPK
                lo, c_lo = (jnp.where(unsettled, jnp.where(enough, mid, lo), lo),
                            jnp.where(unsettled, jnp.where(enough, c_mid, c_lo), c_lo))
                hi, c_hi = (jnp.where(unsettled, jnp.where(enough, hi, mid), hi),
                            jnp.where(unsettled, jnp.where(enough, c_hi, c_mid), c_hi))
            return s + STEP_GROUP, n_unsettled(c_lo), lo, hi, c_lo, c_hi

        return lax.while_loop(cond, step, carry)

    hi0 = rmax + jnp.maximum(jnp.abs(rmax), 1e-30) * (2.0 ** -10)
    c_lo0 = jnp.where(lane < n_real, k_lim, 0).astype(F32)
    steps, n_open, lo, hi, c_lo, c_hi = search(
        (jnp.int32(0), n_unsettled(c_lo0), rmin, hi0, c_lo0, jnp.zeros((1, tq), F32)),
        FIRST_STEPS)

    def n_untied():
        def body(t, carry):
            vmin8, vmax8 = carry
            x = sc_ref[t]
            vmax8 = jnp.maximum(vmax8, _fold(
                jnp.where(x >= lo, jnp.where(x < hi, x, -jnp.inf), -jnp.inf), jnp.maximum))
            vmin8 = jnp.minimum(vmin8, _fold(
                jnp.where(x >= lo, jnp.where(x < hi, x, jnp.inf), jnp.inf), jnp.minimum))
            return vmin8, vmax8
        vmin8, vmax8 = lax.fori_loop(
            0, n_tiles, body,
            (jnp.full((SUBLANES, tq), jnp.inf, F32), jnp.full((SUBLANES, tq), -jnp.inf, F32)))
        vmin = jnp.min(vmin8, axis=0, keepdims=True)
        vmax = jnp.max(vmax8, axis=0, keepdims=True)
        return jnp.sum(jnp.where(c_lo > TOPK, jnp.where(vmax == vmin, 0.0, 1.0), 0.0))

    n_open = lax.cond(n_open > 0.0, n_untied, lambda: jnp.float32(0.0))
    _, _, lo, hi, c_lo, c_hi = search((steps, n_open, lo, hi, c_lo, c_hi), SEARCH_STEPS)

    need = TOPK - c_hi

    def mask_tile(t, run, _):
        x = sc_ref[t]
        inr = jnp.where(x >= lo, jnp.where(x < hi, 1.0, 0.0), 0.0)
        rank = _dot(tri_ref[...], inr.astype(BF16)) + run
        mb_ref[t] = jnp.where(
            x >= hi, 0.0, jnp.where(inr > 0.0, jnp.where(rank <= need, 0.0, NEG), NEG))
        return rank[KEY_TILE - 1:KEY_TILE, :]

    def mask_tile_settled(t, run, _):
        mb_ref[t] = jnp.where(sc_ref[t] >= lo, 0.0, NEG)
        return run

    lax.cond(n_unsettled(c_lo) > 0.0,
             lambda: _pair_loop(n_tiles, mask_tile, jnp.zeros((1, tq), F32)),
             lambda: _pair_loop(n_tiles, mask_tile_settled, jnp.zeros((1, tq), F32)))

    selectors = _value_selectors()

    def attend(t, m_run, slot):
        return _attend_tile(kb_ref, vb_ref, t, lambda h: mb_ref[t], qbm_ref, acc_ref, s_ref, p_ref,
                            m_run, slot, selectors)

    _pair_loop(n_tiles, attend, _attend_init(acc_ref, tq))
    _attend_finish(acc_ref, out_ref, o_ref)


def _sparse(qi, wit, ki2, qb, kb, vb, tri, *, tq, n_real, q_pos0, n_keys, k_batch0=0):
    bn, tq_total, _ = qb.shape
    tk_total = kb.shape[1]
    assert tk_total % KEY_TILE == 0 and n_keys <= tk_total and tq % LANES == 0
    max_tiles = tk_total // KEY_TILE
    blocks = tq_total // tq
    kern = functools.partial(_sparse_kernel, tq=tq, n_real=n_real, q_pos0=q_pos0, n_keys=n_keys)
    qmap = lambda b, j: (b, j, 0)
    kmap = lambda b, j: (k_batch0 + b, 0, 0)
    return pl.pallas_call(
        kern,
        out_shape=jax.ShapeDtypeStruct((bn, tq_total, ATT_WIDTH), BF16),
        grid=(bn, blocks),
        in_specs=[pl.BlockSpec((1, tq, IDX_HEADS * IDX_DIM), qmap),
                  pl.BlockSpec((LANES, tq), lambda b, j: (0, b * blocks + j)),
                  pl.BlockSpec((1, tk_total, LANES), kmap),
                  pl.BlockSpec((1, tq, ATT_WIDTH), qmap),
                  pl.BlockSpec((1, tk_total, ATT_WIDTH), kmap),
                  pl.BlockSpec((1, tk_total, ATT_WIDTH), kmap),
                  _const_spec((KEY_TILE, KEY_TILE))],
        out_specs=pl.BlockSpec((1, tq, ATT_WIDTH), qmap),
        scratch_shapes=[pltpu.VMEM((max_tiles, KEY_TILE, tq), F32),
                        pltpu.VMEM((max_tiles, KEY_TILE, tq), F32),
                        pltpu.VMEM((IDX_HEADS, tq, LANES), BF16),
                        pltpu.VMEM((N_HEADS, tq, LANES), BF16),
                        *_attend_scratch(tq)],
        compiler_params=_params(("parallel", "arbitrary")),
        name="sparse",
    )(qi, wit, ki2, qb, kb, vb, tri)


def _pool_kernel(u_ref, prev_ref, w_ref, scale_ref, o_ref, f_ref, a_ref, b_ref, *, t, pos0):
    n = t + POOL_PAD
    u = u_ref[0]
    f_ref[0:16, :] = jnp.zeros((16, C_WIDTH), F32)
    f_ref[16:32, :] = prev_ref[0]
    f_ref[pl.ds(POOL_PAD, t), :] = u
    s2 = f_ref[pl.ds(8, n - 8), :] + f_ref[pl.ds(7, n - 8), :]
    a_ref[pl.ds(8, n - 8), :] = s2
    s4 = a_ref[pl.ds(16, n - 16), :] + a_ref[pl.ds(14, n - 16), :]
    b_ref[pl.ds(16, n - 16), :] = s4
    s8 = b_ref[pl.ds(24, n - 24), :] + b_ref[pl.ds(20, n - 24), :]
    a_ref[pl.ds(24, n - 24), :] = s8
    s16 = a_ref[pl.ds(32, t), :] + a_ref[pl.ds(24, t), :]
    lane = lax.broadcasted_iota(jnp.int32, (t, C_WIDTH), 1)
    g0, g1, g2 = lane < POOL_GROUP, lane < 2 * POOL_GROUP, lane < 3 * POOL_GROUP
    total = jnp.where(g0, s2[24:], jnp.where(g1, s4[16:], jnp.where(g2, s8[8:], s16)))
    win = jnp.where(g0, POOL_WINDOWS[0],
                    jnp.where(g1, POOL_WINDOWS[1],
                              jnp.where(g2, POOL_WINDOWS[2], POOL_WINDOWS[3])))
    pos = pos0 + lax.broadcasted_iota(jnp.int32, (t, C_WIDTH), 0)
    cnt = jnp.minimum(pos + 1, win).astype(F32)
    pooled = total / cnt - u
    o_ref[0] = (_dot(pooled.astype(BF16), w_ref[...]) * scale_ref[...]).astype(BF16)


def _pool(u, prev16, w_blk, scale, *, pos0):
    bn, t, _ = u.shape
    kern = functools.partial(_pool_kernel, t=t, pos0=pos0)
    bmap = lambda b: (b, 0, 0)
    n = t + POOL_PAD
    return pl.pallas_call(
        kern,
        out_shape=jax.ShapeDtypeStruct((bn, t, C_WIDTH), BF16),
        grid=(bn,),
        in_specs=[pl.BlockSpec((1, t, C_WIDTH), bmap),
                  pl.BlockSpec((1, POOL_MAX, C_WIDTH), bmap),
                  _const_spec((C_WIDTH, C_WIDTH)),
                  _const_spec((1, C_WIDTH))],
        out_specs=pl.BlockSpec((1, t, C_WIDTH), bmap),
        scratch_shapes=[pltpu.VMEM((n, C_WIDTH), F32)] * 3,
        compiler_params=_params(("parallel",)),
        name="pool",
    )(u, prev16, w_blk, scale)


def _pool_weight(w_pool):
    w = jnp.zeros((C_WIDTH, C_WIDTH), w_pool.dtype)
    for g in range(len(POOL_WINDOWS)):
        sl = slice(g * POOL_GROUP, (g + 1) * POOL_GROUP)
        w = w.at[sl, sl].set(w_pool[g])
    return w.astype(BF16)


def _outmlp_kernel(ya_ref, yb_ref, yc_ref, x_ref, woa_ref, wob_ref, woc_ref,
                   g1_ref, g2_ref, g3_ref, w1_ref, w2_ref, o_ref):
    y = _dot(ya_ref[...], woa_ref[...]) + _dot(yb_ref[...], wob_ref[...])
    y = y + _dot(yc_ref[...], woc_ref[...])
    x1 = x_ref[...] + _rms(y, g1_ref[...])
    h = _rms(x1, g2_ref[...]).astype(BF16)
    m = jnp.zeros_like(x1)
    for f in range(D_FF // D_MODEL):
        sl = slice(f * D_MODEL, (f + 1) * D_MODEL)
        a = jnp.maximum(_dot(h, w1_ref[:, sl]), 0.0)
        m = m + _dot((a * a).astype(BF16), w2_ref[sl, :])
    o_ref[...] = x1 + _rms(m, g3_ref[...])


def _outmlp(ya, yb, yc, x, woa, wob, woc, g1, g2, g3, w1, w2):
    n = x.shape[0]
    assert n % ROW_TILE == 0
    row = lambda i: (i, 0)
    return pl.pallas_call(
        _outmlp_kernel,
        out_shape=jax.ShapeDtypeStruct((n, D_MODEL), F32),
        grid=(n // ROW_TILE,),
        in_specs=[pl.BlockSpec((ROW_TILE, ATT_WIDTH), row),
                  pl.BlockSpec((ROW_TILE, ATT_WIDTH), row),
                  pl.BlockSpec((ROW_TILE, C_WIDTH), row),
                  pl.BlockSpec((ROW_TILE, D_MODEL), row),
                  _const_spec((ATT_WIDTH, D_MODEL)),
                  _const_spec((ATT_WIDTH, D_MODEL)),
                  _const_spec((C_WIDTH, D_MODEL)),
                  _const_spec((1, D_MODEL)),
                  _const_spec((1, D_MODEL)),
                  _const_spec((1, D_MODEL)),
                  _const_spec((D_MODEL, D_FF)),
                  _const_spec((D_FF, D_MODEL))],
        out_specs=pl.BlockSpec((ROW_TILE, D_MODEL), row),
        compiler_params=_params(("parallel",)),
        name="outmlp",
    )(ya, yb, yc, x, woa, wob, woc, g1, g2, g3, w1, w2)


def _pad_axis(x, axis, size):
    pads = [(0, 0)] * x.ndim
    pads[axis] = (0, size - x.shape[axis])
    return jnp.pad(x, pads)


def _layer(x, weights, tri, stacked, *, layer, depth, seq, band_tab, band_shift, q_pos0,
           past=None):
    (g_pre_mix, w_proj, w_proj_tail, w_proj_t, w_pool_blk, pool_scale, woa, wob, woc, g_post_mix,
     g_pre_mlp, w1, w2, g_post_mlp) = weights
    bn = x.shape[0] // seq
    (qa, qb, qi, u, wit), stacked = _proj(x, g_pre_mix, w_proj, w_proj_tail, w_proj_t, stacked,
                                          layer=layer, depth=depth)
    per_batch = lambda t: t.reshape(bn, seq, t.shape[-1])
    qa, qb, qi, u = map(per_batch, (qa, qb, qi, u))
    ka_all, va_all, kb_all, vb_all, ki2_all = (t.reshape(depth * bn, seq, t.shape[-1])
                                               for t in stacked)

    if past is None:
        k_batch0 = layer * bn
        prev16 = jnp.zeros((bn, POOL_MAX, C_WIDTH), F32)
        n_keys = seq
        tq, qa_q, qi_q, qb_q, wit_q = KEY_TILE, qa, qi, qb, wit
    else:
        c_a_k, c_a_v, c_b_k, c_b_v, c_b_kidx, c_pool = past
        k_batch0 = 0
        ka, va, kb, vb, ki2 = (t[layer].reshape(bn, seq, t.shape[-1]) for t in stacked)
        flat = lambda t: t.reshape(bn, t.shape[1], ATT_WIDTH)
        band_rows = 3 * KEY_TILE
        ka_all = _pad_axis(jnp.concatenate([flat(c_a_k), ka], axis=1), 1, band_rows)
        va_all = _pad_axis(jnp.concatenate([flat(c_a_v), va], axis=1), 1, band_rows)
        n_keys = c_b_k.shape[1] + seq
        key_rows = -(-n_keys // KEY_TILE) * KEY_TILE
        kb_all = _pad_axis(jnp.concatenate([flat(c_b_k), kb], axis=1), 1, key_rows)
        vb_all = _pad_axis(jnp.concatenate([flat(c_b_v), vb], axis=1), 1, key_rows)
        c_ki2 = jnp.concatenate([c_b_kidx, c_b_kidx], axis=-1)
        ki2_all = _pad_axis(jnp.concatenate([c_ki2, ki2], axis=1), 1, key_rows)
        prev16 = jnp.pad(c_pool, ((0, 0), (1, 0), (0, 0)))
        tq = LANES
        qa_q, qi_q, qb_q = (_pad_axis(t, 1, LANES) for t in (qa, qi, qb))
        wit_q = _pad_axis(wit.reshape(LANES, bn, seq), 2, LANES).reshape(LANES, bn * LANES)

    ya = _band(qa_q, ka_all, va_all, band_tab, tq=tq, shift=band_shift,
               k_batch0=k_batch0)[:, :seq]
    yb = _sparse(qi_q, wit_q, ki2_all, qb_q, kb_all, vb_all, tri, tq=tq, n_real=min(seq, tq),
                 q_pos0=q_pos0, n_keys=n_keys, k_batch0=k_batch0)[:, :seq]
    yc = _pool(u, prev16, w_pool_blk, pool_scale, pos0=q_pos0)
    flat2 = lambda t: t.reshape(bn * seq, t.shape[-1])
    x = _outmlp(flat2(ya), flat2(yb), flat2(yc), x, woa, wob, woc,
                g_post_mix, g_pre_mlp, g_post_mlp, w1, w2)
    return x, stacked, u


def _state_leaves(stacked, depth, bn, seq, n_keep):
    ka, va, kb, vb, ki2 = (t.reshape(depth, bn, seq, t.shape[-1]) for t in stacked)
    heads = lambda t: t.reshape(depth, bn, t.shape[2], N_HEADS, HEAD_DIM)
    return (heads(ka[:, :, seq - n_keep:]), heads(va[:, :, seq - n_keep:]), heads(kb), heads(vb),
            ki2[..., :IDX_DIM])


def kernel(x_prompt, x_sample, cache_a_k, cache_a_v, cache_b_k, cache_b_v, cache_b_kidx, state_pool, g_pre_mix, w_in, rel_bias, w_pool, pool_scale, w_out, g_post_mix, g_pre_mlp, w_ff1, w_ff2, g_post_mlp):
    batch, seq, _ = x_prompt.shape
    dec_batch, dec_seq, _ = x_sample.shape
    depth = w_in.shape[0]
    past_len = cache_b_k.shape[2]
    n_a = cache_a_k.shape[2]
    assert seq % KEY_TILE == 0 and (batch * seq) % ROW_TILE == 0
    assert (dec_batch * dec_seq) % ROW_TILE == 0 and n_a + dec_seq <= 3 * KEY_TILE
    assert dec_seq <= LANES

    tri = (jnp.arange(KEY_TILE)[:, None] >= jnp.arange(KEY_TILE)[None, :]).astype(BF16)
    xp = x_prompt.reshape(batch * seq, D_MODEL)
    xs = x_sample.reshape(dec_batch * dec_seq, D_MODEL)
    p_stacked = s_stacked = None
    p_pools, s_pools = [], []
    row = lambda t: t.reshape(1, -1)
    for l in range(depth):
        weights = (row(g_pre_mix[l]), *_proj_weight(w_in[l]), _pool_weight(w_pool[l]),
                   row(pool_scale[l]),
                   w_out[l, :ATT_WIDTH].astype(BF16),
                   w_out[l, ATT_WIDTH:2 * ATT_WIDTH].astype(BF16),
                   w_out[l, 2 * ATT_WIDTH:].astype(BF16),
                   row(g_post_mix[l]), row(g_pre_mlp[l]),
                   w_ff1[l].astype(BF16), w_ff2[l].astype(BF16), row(g_post_mlp[l]))
        xp, p_stacked, u = _layer(
            xp, weights, tri, p_stacked, layer=l, depth=depth, seq=seq,
            band_tab=_band_table(rel_bias[l], 2 * KEY_TILE, 0, KEY_TILE, 3 * KEY_TILE),
            band_shift=2, q_pos0=0)
        p_pools.append(u[:, seq - (POOL_MAX - 1):])
        xs, s_stacked, u = _layer(
            xs, weights, tri, s_stacked, layer=l, depth=depth, seq=dec_seq,
            band_tab=_band_table(rel_bias[l], past_len, past_len - n_a, LANES, n_a + dec_seq),
            band_shift=0, q_pos0=past_len,
            past=(cache_a_k[l], cache_a_v[l], cache_b_k[l], cache_b_v[l], cache_b_kidx[l],
                  state_pool[l]))
        s_pools.append(jnp.concatenate([state_pool[l], u], axis=1)[:, dec_seq:])
    return ((xp.reshape(batch, seq, D_MODEL), xs.reshape(dec_batch, dec_seq, D_MODEL))
            + _state_leaves(p_stacked, depth, batch, seq, min(A_WINDOW, seq))
            + (jnp.stack(p_pools, axis=0),)
            + _state_leaves(s_stacked, depth, dec_batch, dec_seq, dec_seq)
            + (jnp.stack(s_pools, axis=0),))
```

```python
import functools

import jax
import jax.numpy as jnp
from jax import lax
from jax.experimental import pallas as pl
from jax.experimental.pallas import tpu as pltpu

D_MODEL = 1024
CHUNK = 64
CHUNK_SHIFT = 6
HEAD_DIM = 64
N_HEADS = 6
ATT_WIDTH = N_HEADS * HEAD_DIM
N_PAIRS = N_HEADS // 2
C_WIDTH = 256
A_LEFT_CHUNKS = 8
A_WINDOW = A_LEFT_CHUNKS * CHUNK
REL_CLIP = 128
IDX_HEADS = 8
IDX_DIM = 64
TOPK = 256
POOL_WINDOWS = (2, 4, 8, 16)
POOL_GROUP = 64
POOL_MAX = 16
POOL_PAD = 32
D_FF = 4 * D_MODEL
RMS_EPS = 1e-6
IDX_SCALE = IDX_HEADS ** -0.5 * IDX_DIM ** -0.5
ATT_SCALE = HEAD_DIM ** -0.5

LANES = 128
SUBLANES = 8
KEY_TILE = 256
KEY_TILE_SHIFT = 8
ROW_TILE = 512
NEG = -1e30
LOG2E = 1.4426950408889634
VALUE_ROWS = HEAD_DIM + 16
STEP_GROUP = 4
FIRST_STEPS = 20
SEARCH_STEPS = 40
VMEM_LIMIT = 48 * 1024 * 1024

_OFF_KI, _OFF_WI, _OFF_U, _OFF_END = 2816, 2880, 2888, 3144

F32 = jnp.float32
BF16 = jnp.bfloat16


def _const_spec(shape):
    zeros = (0,) * len(shape)
    return pl.BlockSpec(shape, lambda *_: zeros, pipeline_mode=pl.Buffered(1))


def _params(semantics):
    return pltpu.CompilerParams(dimension_semantics=semantics, vmem_limit_bytes=VMEM_LIMIT)


def _rms(x, g):
    ms = jnp.mean(x * x, axis=-1, keepdims=True)
    return x * lax.rsqrt(ms + RMS_EPS) * g


def _dot(a, b):
    return jnp.dot(a, b, preferred_element_type=F32)


def _dot_t(a, b):
    return lax.dot_general(a, b, (((1,), (1,)), ((), ())), preferred_element_type=F32)


def _half_masks(rows):
    lane = lax.broadcasted_iota(jnp.int32, (rows, LANES), 1)
    first = lane < HEAD_DIM
    m0 = jnp.where(first, 1.0, 0.0).astype(BF16)
    m1 = jnp.where(first, 0.0, 1.0).astype(BF16)
    return first, m0, m1


def _fold(x, op):
    parts = [x[i:i + SUBLANES] for i in range(0, x.shape[0], SUBLANES)]
    lanes = min(4, len(parts))
    acc = parts[:lanes]
    for i, part in enumerate(parts[lanes:]):
        acc[i % lanes] = op(acc[i % lanes], part)
    while len(acc) > 1:
        acc = [op(acc[i], acc[i + 1]) for i in range(0, len(acc), 2)]
    return acc[0]


def _pair_loop(n, body, init):
    def triple(i, carry):
        return body(3 * i + 2, body(3 * i + 1, body(3 * i, carry, 0), 1), 0)
    n3 = lax.div(n, 3)
    carry = lax.fori_loop(0, n3, triple, init)
    rest = n - 3 * n3
    carry = lax.cond(rest >= 1, lambda c: body(3 * n3, c, 1), lambda c: c, carry)
    return lax.cond(rest >= 2, lambda c: body(3 * n3 + 1, c, 0), lambda c: c, carry)


def _proj_kernel(x_ref, g_ref, w_ref, wtail_ref, wt_ref, qa_ref, qb_ref, qi_ref, u_ref, wit_ref,
                 ka_ref, va_ref, kb_ref, vb_ref, ki2_ref):
    def put(ref, val):
        ref[0] = val
        if ref.shape[0] > 1:
            ref[1:] = jnp.zeros((ref.shape[0] - 1,) + val.shape, val.dtype)

    h = _rms(x_ref[...], g_ref[...]).astype(BF16)
    z = _dot(h, w_ref[:, 0:768])
    qa_ref[...] = (z[:, 0:384] * (ATT_SCALE * LOG2E)).astype(BF16)
    put(ka_ref, z[:, 384:768])
    z = _dot(h, w_ref[:, 768:1536])
    put(va_ref, z[:, 0:384])
    qb_ref[...] = (z[:, 384:768] * (ATT_SCALE * LOG2E)).astype(BF16)
    z = _dot(h, w_ref[:, 1536:2304])
    put(kb_ref, z[:, 0:384])
    put(vb_ref, z[:, 384:768])
    qi_ref[...] = _dot(h, w_ref[:, 2304:2816]).astype(BF16)
    z = _dot(h, wtail_ref[...])
    put(ki2_ref, z[:, 0:128])
    u_ref[...] = z[:, 128:384]
    wit_ref[...] = _dot_t(wt_ref[...], h)


N_PROJ_IN = 5
N_PROJ_PLAIN = 5
STACKED_WIDTHS = (ATT_WIDTH,) * 4 + (LANES,)


def _proj(x, g, w, wtail, wt, stacked, *, layer, depth):
    n = x.shape[0]
    assert n % ROW_TILE == 0
    row = lambda i: (i, 0)
    plain = ((ATT_WIDTH, BF16), (ATT_WIDTH, BF16), (IDX_HEADS * IDX_DIM, BF16), (C_WIDTH, F32))
    out_shape = [jax.ShapeDtypeStruct((n, wd), dt) for wd, dt in plain]
    out_specs = [pl.BlockSpec((ROW_TILE, wd), row) for wd, _ in plain]
    out_shape += [jax.ShapeDtypeStruct((LANES, n), F32)]
    out_specs += [pl.BlockSpec((LANES, ROW_TILE), lambda i: (0, i))]
    assert len(out_shape) == N_PROJ_PLAIN
    out_shape += [jax.ShapeDtypeStruct((depth, n, wd), F32) for wd in STACKED_WIDTHS]
    in_specs = [pl.BlockSpec((ROW_TILE, D_MODEL), row),
                _const_spec((1, D_MODEL)),
                _const_spec((D_MODEL, _OFF_KI)),
                _const_spec((D_MODEL, 2 * IDX_DIM + C_WIDTH)),
                _const_spec((LANES, D_MODEL))]
    if stacked is None:
        assert layer == 0
        operands, aliases, kern = (), {}, _proj_kernel
        out_specs += [pl.BlockSpec((depth, ROW_TILE, wd), lambda i: (0, i, 0))
                      for wd in STACKED_WIDTHS]
    else:
        operands = tuple(stacked)
        in_specs += [pl.BlockSpec(memory_space=pl.ANY)] * len(operands)
        aliases = {N_PROJ_IN + k: N_PROJ_PLAIN + k for k in range(len(operands))}
        out_specs += [pl.BlockSpec((1, ROW_TILE, wd), lambda i: (layer, i, 0))
                      for wd in STACKED_WIDTHS]

        def kern(*refs):
            _proj_kernel(*refs[:N_PROJ_IN], *refs[N_PROJ_IN + len(operands):])

    outs = pl.pallas_call(
        kern,
        out_shape=tuple(out_shape),
        grid=(n // ROW_TILE,),
        in_specs=in_specs,
        out_specs=tuple(out_specs),
        input_output_aliases=aliases,
        compiler_params=_params(("parallel",)),
        name="proj",
    )(x, g, w, wtail, wt, *operands)
    return outs[:N_PROJ_PLAIN], outs[N_PROJ_PLAIN:]


def _proj_weight(w_in):
    w_bf = w_in.astype(BF16)
    ki = w_bf[:, _OFF_KI:_OFF_WI]
    wtail = jnp.concatenate([ki, ki, w_bf[:, _OFF_U:_OFF_END]], axis=1)
    wi_t = jnp.pad(w_bf[:, _OFF_WI:_OFF_U].T, ((0, LANES - IDX_HEADS), (0, 0)))
    return w_bf[:, :_OFF_KI], wtail, wi_t


def _mask_heads(q_ref, qm_ref, heads, tq):
    _, m0, m1 = _half_masks(tq)
    for h in range(heads):
        pair = slice(LANES * (h // 2), LANES * (h // 2 + 1))
        qm_ref[h] = q_ref[0, :, pair] * (m0 if h % 2 == 0 else m1)


def _attend_scratch(tq):
    return [pltpu.VMEM((N_HEADS, VALUE_ROWS, tq), F32),
            pltpu.VMEM((ATT_WIDTH, tq), F32),
            pltpu.VMEM((2, N_HEADS, KEY_TILE, tq), F32),
            pltpu.VMEM((2, N_HEADS, KEY_TILE, tq), BF16)]


def _attend_init(acc_ref, tq):
    for h in range(N_HEADS):
        acc_ref[h] = jnp.zeros((VALUE_ROWS, tq), F32)
    return tuple(jnp.full((1, tq), NEG, F32) for _ in range(N_HEADS))


def _value_selectors():
    row = lax.broadcasted_iota(jnp.int32, (VALUE_ROWS, LANES), 0)
    lane = lax.broadcasted_iota(jnp.int32, (VALUE_ROWS, LANES), 1)
    pick = lambda off: jnp.where(row < HEAD_DIM, jnp.where(lane == row + off, 1.0, 0.0),
                                 0.0).astype(BF16)
    ones_rows = jnp.where(
        lax.broadcasted_iota(jnp.int32, (VALUE_ROWS, KEY_TILE), 0) >= HEAD_DIM, 1.0, 0.0)
    return pick(0), pick(HEAD_DIM), ones_rows


def _attend_tile(k_ref, v_ref, tile, bias_of_head, qm_ref, acc_ref, s_ref, p_ref, m_run, slot,
                 selectors):
    start = pl.multiple_of(tile * KEY_TILE, KEY_TILE)
    sel0, sel1, ones_rows = selectors
    m_new = []
    for pair in range(N_PAIRS):
        kt = k_ref[0, pl.ds(start, KEY_TILE), LANES * pair:LANES * (pair + 1)].astype(BF16)
        for h in (2 * pair, 2 * pair + 1):
            s = _dot_t(kt, qm_ref[h]) + bias_of_head(h)
            s_ref[slot, h] = s
            col_max = jnp.max(_fold(s, jnp.maximum), axis=0, keepdims=True)
            m_new.append(jnp.maximum(m_run[h], col_max))
    for h in range(N_HEADS):
        p_ref[slot, h] = jnp.exp2(s_ref[slot, h] - m_new[h]).astype(BF16)
        acc_ref[h] = jnp.exp2(m_run[h] - m_new[h]) * acc_ref[h]
    for pair in range(N_PAIRS):
        vp = v_ref[0, pl.ds(start, KEY_TILE), LANES * pair:LANES * (pair + 1)].astype(BF16)
        for h, sel in ((2 * pair, sel0), (2 * pair + 1, sel1)):
            vt = (_dot_t(sel, vp) + ones_rows).astype(BF16)
            acc_ref[h] = acc_ref[h] + _dot(vt, p_ref[slot, h])
    return tuple(m_new)


def _attend_finish(acc_ref, out_ref, o_ref):
    for h in range(N_HEADS):
        acc = acc_ref[h]
        out_ref[HEAD_DIM * h:HEAD_DIM * (h + 1), :] = acc[:HEAD_DIM] / acc[HEAD_DIM:HEAD_DIM + 1]
    o_ref[0] = out_ref[...].T.astype(BF16)


def _band_kernel(q_ref, k_ref, v_ref, tab_ref, o_ref, qm_ref, acc_ref, out_ref, s_ref, p_ref,
                 *, tq, shift):
    j = pl.program_id(1)
    _mask_heads(q_ref, qm_ref, N_HEADS, tq)
    selectors = _value_selectors()
    m_run = _attend_init(acc_ref, tq)
    for t in range(3):
        blk = j + (t - shift)
        if t >= shift:
            bias = lambda h, t=t: tab_ref[h, t]
        else:
            bias = lambda h, t=t, blk=blk: jnp.where(blk >= 0, tab_ref[h, t], NEG)
        m_run = _attend_tile(k_ref, v_ref, jnp.maximum(blk, 0), bias, qm_ref, acc_ref, s_ref,
                             p_ref, m_run, t % 2, selectors)
    _attend_finish(acc_ref, out_ref, o_ref)


def _band(q, k, v, tab, *, tq, shift, k_batch0=0):
    bn, tq_total, _ = q.shape
    tk_total = k.shape[1]
    kern = functools.partial(_band_kernel, tq=tq, shift=shift)
    qmap = lambda b, j: (b, j, 0)
    kmap = lambda b, j: (k_batch0 + b, 0, 0)
    return pl.pallas_call(
        kern,
        out_shape=jax.ShapeDtypeStruct((bn, tq_total, ATT_WIDTH), BF16),
        grid=(bn, tq_total // tq),
        in_specs=[pl.BlockSpec((1, tq, ATT_WIDTH), qmap),
                  pl.BlockSpec((1, tk_total, ATT_WIDTH), kmap),
                  pl.BlockSpec((1, tk_total, ATT_WIDTH), kmap),
                  _const_spec(tab.shape)],
        out_specs=pl.BlockSpec((1, tq, ATT_WIDTH), qmap),
        scratch_shapes=[pltpu.VMEM((N_HEADS, tq, LANES), BF16),
                        *_attend_scratch(tq)],
        compiler_params=_params(("parallel", "arbitrary")),
        name="band",
    )(q, k, v, tab)


def _band_table(rel_bias, q_pos0, k_pos0, tq, n_real):
    tk = 3 * KEY_TILE
    span = tq + tk - 1
    diff = (q_pos0 - k_pos0) - (tk - 1) + jnp.arange(span)
    vec = jnp.take(rel_bias.astype(F32), jnp.clip(diff, -REL_CLIP, REL_CLIP) + REL_CLIP, axis=1)
    h = vec.shape[0]
    skew = jnp.tile(vec, (1, tq + 1))[:, :tq * (span + 1)].reshape(h, tq, span + 1)
    bias = jnp.flip(skew[:, :, :tk], axis=-1)
    qc = (q_pos0 + jnp.arange(tq)) // CHUNK
    kc = (k_pos0 + jnp.arange(tk)) // CHUNK
    valid = ((kc[None, :] <= qc[:, None]) & (kc[None, :] >= qc[:, None] - A_LEFT_CHUNKS)
             & (jnp.arange(tk) < n_real)[None, :])
    tab = jnp.where(valid[None], bias * LOG2E, NEG)
    return tab.reshape(h, tq, 3, KEY_TILE).transpose(0, 2, 3, 1)


def _sparse_kernel(qi_ref, wit_ref, ki2_ref, qb_ref, kb_ref, vb_ref, tri_ref, o_ref,
                   sc_ref, mb_ref, qim_ref, qbm_ref, acc_ref, out_ref, s_ref, p_ref,
                   *, tq, n_real, q_pos0, n_keys):
    j = pl.program_id(1)
    q_first = q_pos0 + j * n_real
    last_chunk = lax.shift_right_logical(q_first + (n_real - 1), CHUNK_SHIFT)
    k_end = jnp.minimum((last_chunk + 1) * CHUNK, n_keys)
    n_tiles = lax.shift_right_logical(k_end + (KEY_TILE - 1), KEY_TILE_SHIFT)

    lane = lax.broadcasted_iota(jnp.int32, (1, tq), 1)
    q_pos = q_first + lane
    k_lim = jnp.minimum((lax.shift_right_logical(q_pos, CHUNK_SHIFT) + 1) * CHUNK, n_keys)
    key_row = lax.broadcasted_iota(jnp.int32, (KEY_TILE, tq), 0)

    _mask_heads(qi_ref, qim_ref, IDX_HEADS, tq)
    _mask_heads(qb_ref, qbm_ref, N_HEADS, tq)
    w = wit_ref[0:SUBLANES, :] * IDX_SCALE

    def score_tile(t, carry, _):
        rmin8, rmax8 = carry
        start = pl.multiple_of(t * KEY_TILE, KEY_TILE)
        kt = ki2_ref[0, pl.ds(start, KEY_TILE), :].astype(BF16)
        sc = jnp.zeros((KEY_TILE, tq), F32)
        for h in range(IDX_HEADS):
            sc = sc + jnp.maximum(_dot_t(kt, qim_ref[h]), 0.0) * w[h:h + 1, :]
        adm = (start + key_row) < k_lim
        lowest = jnp.where(adm, sc, -jnp.inf)
        sc_ref[t] = lowest
        rmax8 = jnp.maximum(rmax8, _fold(lowest, jnp.maximum))
        rmin8 = jnp.minimum(rmin8, _fold(jnp.where(adm, sc, jnp.inf), jnp.minimum))
        return rmin8, rmax8

    rmin8, rmax8 = _pair_loop(
        n_tiles, score_tile,
        (jnp.full((SUBLANES, tq), jnp.inf, F32), jnp.full((SUBLANES, tq), -jnp.inf, F32)))
    rmin = jnp.min(rmin8, axis=0, keepdims=True)
    rmax = jnp.max(rmax8, axis=0, keepdims=True)

    def count_ge(thr):
        def body(t, c8):
            return c8 + _fold(jnp.where(sc_ref[t] >= thr, 1.0, 0.0), jnp.add)
        c8 = lax.fori_loop(0, n_tiles, body, jnp.zeros((SUBLANES, tq), F32))
        return jnp.sum(c8, axis=0, keepdims=True)

    def n_unsettled(c_lo):
        return jnp.sum(jnp.where(c_lo > TOPK, 1.0, 0.0))

    def search(carry, last_step):
        def cond(c):
            return jnp.logical_and(c[0] < last_step, c[1] > 0.0)

        def step(c):
            s, _, lo, hi, c_lo, c_hi = c
            for _ in range(STEP_GROUP):
                mid = lo + (hi - lo) * 0.5
                c_mid = count_ge(mid)
                unsettled = c_lo > TOPK
                enough = c_mid >= TOPK
                lo, c_lo = (jnp.where(unsettled, jnp.where(enough, mid, lo), lo),
                            jnp.where(unsettled, jnp.where(enough, c_mid, c_lo), c_lo))
                hi, c_hi = (jnp.where(unsettled, jnp.where(enough, hi, mid), hi),
                            jnp.where(unsettled, jnp.where(enough, c_hi, c_mid), c_hi))
            return s + STEP_GROUP, n_unsettled(c_lo), lo, hi, c_lo, c_hi

        return lax.while_loop(cond, step, carry)

    hi0 = rmax + jnp.maximum(jnp.abs(rmax), 1e-30) * (2.0 ** -10)
    c_lo0 = jnp.where(lane < n_real, k_lim, 0).astype(F32)
    steps, n_open, lo, hi, c_lo, c_hi = search(
        (jnp.int32(0), n_unsettled(c_lo0), rmin, hi0, c_lo0, jnp.zeros((1, tq), F32)),
        FIRST_STEPS)

    def n_untied():
        def body(t, carry):
            vmin8, vmax8 = carry
            x = sc_ref[t]
            vmax8 = jnp.maximum(vmax8, _fold(
                jnp.where(x >= lo, jnp.where(x < hi, x, -jnp.inf), -jnp.inf), jnp.maximum))
            vmin8 = jnp.minimum(vmin8, _fold(
                jnp.where(x >= lo, jnp.where(x < hi, x, jnp.inf), jnp.inf), jnp.minimum))
            return vmin8, vmax8
        vmin8, vmax8 = lax.fori_loop(
            0, n_tiles, body,
            (jnp.full((SUBLANES, tq), jnp.inf, F32), jnp.full((SUBLANES, tq), -jnp.inf, F32)))
        vmin = jnp.min(vmin8, axis=0, keepdims=True)
        vmax = jnp.max(vmax8, axis=0, keepdims=True)
        return jnp.sum(jnp.where(c_lo > TOPK, jnp.where(vmax == vmin, 0.0, 1.0), 0.0))

    n_open = lax.cond(n_open > 0.0, n_untied, lambda: jnp.float32(0.0))
    _, _, lo, hi, c_lo, c_hi = search((steps, n_open, lo, hi, c_lo, c_hi), SEARCH_STEPS)

    need = TOPK - c_hi

    def mask_tile(t, run, _):
        x = sc_ref[t]
        inr = jnp.where(x >= lo, jnp.where(x < hi, 1.0, 0.0), 0.0)
        rank = _dot(tri_ref[...], inr.astype(BF16)) + run
        mb_ref[t] = jnp.where(
            x >= hi, 0.0, jnp.where(inr > 0.0, jnp.where(rank <= need, 0.0, NEG), NEG))
        return rank[KEY_TILE - 1:KEY_TILE, :]

    def mask_tile_settled(t, run, _):
        mb_ref[t] = jnp.where(sc_ref[t] >= lo, 0.0, NEG)
        return run

    lax.cond(n_unsettled(c_lo) > 0.0,
             lambda: _pair_loop(n_tiles, mask_tile, jnp.zeros((1, tq), F32)),
             lambda: _pair_loop(n_tiles, mask_tile_settled, jnp.zeros((1, tq), F32)))

    selectors = _value_selectors()

    def attend(t, m_run, slot):
        return _attend_tile(kb_ref, vb_ref, t, lambda h: mb_ref[t], qbm_ref, acc_ref, s_ref, p_ref,
                            m_run, slot, selectors)

    _pair_loop(n_tiles, attend, _attend_init(acc_ref, tq))
    _attend_finish(acc_ref, out_ref, o_ref)


def _sparse(qi, wit, ki2, qb, kb, vb, tri, *, tq, n_real, q_pos0, n_keys, k_batch0=0):
    bn, tq_total, _ = qb.shape
    tk_total = kb.shape[1]
    assert tk_total % KEY_TILE == 0 and n_keys <= tk_total and tq % LANES == 0
    max_tiles = tk_total // KEY_TILE
    blocks = tq_total // tq
    kern = functools.partial(_sparse_kernel, tq=tq, n_real=n_real, q_pos0=q_pos0, n_keys=n_keys)
    qmap = lambda b, j: (b, j, 0)
    kmap = lambda b, j: (k_batch0 + b, 0, 0)
    return pl.pallas_call(
        kern,
        out_shape=jax.ShapeDtypeStruct((bn, tq_total, ATT_WIDTH), BF16),
        grid=(bn, blocks),
        in_specs=[pl.BlockSpec((1, tq, IDX_HEADS * IDX_DIM), qmap),
                  pl.BlockSpec((LANES, tq), lambda b, j: (0, b * blocks + j)),
                  pl.BlockSpec((1, tk_total, LANES), kmap),
                  pl.BlockSpec((1, tq, ATT_WIDTH), qmap),
                  pl.BlockSpec((1, tk_total, ATT_WIDTH), kmap),
                  pl.BlockSpec((1, tk_total, ATT_WIDTH), kmap),
                  _const_spec((KEY_TILE, KEY_TILE))],
        out_specs=pl.BlockSpec((1, tq, ATT_WIDTH), qmap),
        scratch_shapes=[pltpu.VMEM((max_tiles, KEY_TILE, tq), F32),
                        pltpu.VMEM((max_tiles, KEY_TILE, tq), F32),
                        pltpu.VMEM((IDX_HEADS, tq, LANES), BF16),
                        pltpu.VMEM((N_HEADS, tq, LANES), BF16),
                        *_attend_scratch(tq)],
        compiler_params=_params(("parallel", "arbitrary")),
        name="sparse",
    )(qi, wit, ki2, qb, kb, vb, tri)


def _pool_kernel(u_ref, prev_ref, w_ref, scale_ref, o_ref, f_ref, a_ref, b_ref, *, t, pos0):
    n = t + POOL_PAD
    u = u_ref[0]
    f_ref[0:16, :] = jnp.zeros((16, C_WIDTH), F32)
    f_ref[16:32, :] = prev_ref[0]
    f_ref[pl.ds(POOL_PAD, t), :] = u
    s2 = f_ref[pl.ds(8, n - 8), :] + f_ref[pl.ds(7, n - 8), :]
    a_ref[pl.ds(8, n - 8), :] = s2
    s4 = a_ref[pl.ds(16, n - 16), :] + a_ref[pl.ds(14, n - 16), :]
    b_ref[pl.ds(16, n - 16), :] = s4
    s8 = b_ref[pl.ds(24, n - 24), :] + b_ref[pl.ds(20, n - 24), :]
    a_ref[pl.ds(24, n - 24), :] = s8
    s16 = a_ref[pl.ds(32, t), :] + a_ref[pl.ds(24, t), :]
    lane = lax.broadcasted_iota(jnp.int32, (t, C_WIDTH), 1)
    g0, g1, g2 = lane < POOL_GROUP, lane < 2 * POOL_GROUP, lane < 3 * POOL_GROUP
    total = jnp.where(g0, s2[24:], jnp.where(g1, s4[16:], jnp.where(g2, s8[8:], s16)))
    win = jnp.where(g0, POOL_WINDOWS[0],
                    jnp.where(g1, POOL_WINDOWS[1],
                              jnp.where(g2, POOL_WINDOWS[2], POOL_WINDOWS[3])))
    pos = pos0 + lax.broadcasted_iota(jnp.int32, (t, C_WIDTH), 0)
    cnt = jnp.minimum(pos + 1, win).astype(F32)
    pooled = total / cnt - u
    o_ref[0] = (_dot(pooled.astype(BF16), w_ref[...]) * scale_ref[...]).astype(BF16)


def _pool(u, prev16, w_blk, scale, *, pos0):
    bn, t, _ = u.shape
    kern = functools.partial(_pool_kernel, t=t, pos0=pos0)
    bmap = lambda b: (b, 0, 0)
    n = t + POOL_PAD
    return pl.pallas_call(
        kern,
        out_shape=jax.ShapeDtypeStruct((bn, t, C_WIDTH), BF16),
        grid=(bn,),
        in_specs=[pl.BlockSpec((1, t, C_WIDTH), bmap),
                  pl.BlockSpec((1, POOL_MAX, C_WIDTH), bmap),
                  _const_spec((C_WIDTH, C_WIDTH)),
                  _const_spec((1, C_WIDTH))],
        out_specs=pl.BlockSpec((1, t, C_WIDTH), bmap),
        scratch_shapes=[pltpu.VMEM((n, C_WIDTH), F32)] * 3,
        compiler_params=_params(("parallel",)),
        name="pool",
    )(u, prev16, w_blk, scale)


def _pool_weight(w_pool):
    w = jnp.zeros((C_WIDTH, C_WIDTH), w_pool.dtype)
    for g in range(len(POOL_WINDOWS)):
        sl = slice(g * POOL_GROUP, (g + 1) * POOL_GROUP)
        w = w.at[sl, sl].set(w_pool[g])
    return w.astype(BF16)


def _outmlp_kernel(ya_ref, yb_ref, yc_ref, x_ref, woa_ref, wob_ref, woc_ref,
                   g1_ref, g2_ref, g3_ref, w1_ref, w2_ref, o_ref):
    y = _dot(ya_ref[...], woa_ref[...]) + _dot(yb_ref[...], wob_ref[...])
    y = y + _dot(yc_ref[...], woc_ref[...])
    x1 = x_ref[...] + _rms(y, g1_ref[...])
    h = _rms(x1, g2_ref[...]).astype(BF16)
    m = jnp.zeros_like(x1)
    for f in range(D_FF // D_MODEL):
        sl = slice(f * D_MODEL, (f + 1) * D_MODEL)
        a = jnp.maximum(_dot(h, w1_ref[:, sl]), 0.0)
        m = m + _dot((a * a).astype(BF16), w2_ref[sl, :])
    o_ref[...] = x1 + _rms(m, g3_ref[...])


def _outmlp(ya, yb, yc, x, woa, wob, woc, g1, g2, g3, w1, w2):
    n = x.shape[0]
    assert n % ROW_TILE == 0
    row = lambda i: (i, 0)
    return pl.pallas_call(
        _outmlp_kernel,
        out_shape=jax.ShapeDtypeStruct((n, D_MODEL), F32),
        grid=(n // ROW_TILE,),
        in_specs=[pl.BlockSpec((ROW_TILE, ATT_WIDTH), row),
                  pl.BlockSpec((ROW_TILE, ATT_WIDTH), row),
                  pl.BlockSpec((ROW_TILE, C_WIDTH), row),
                  pl.BlockSpec((ROW_TILE, D_MODEL), row),
                  _const_spec((ATT_WIDTH, D_MODEL)),
                  _const_spec((ATT_WIDTH, D_MODEL)),
                  _const_spec((C_WIDTH, D_MODEL)),
                  _const_spec((1, D_MODEL)),
                  _const_spec((1, D_MODEL)),
                  _const_spec((1, D_MODEL)),
                  _const_spec((D_MODEL, D_FF)),
                  _const_spec((D_FF, D_MODEL))],
        out_specs=pl.BlockSpec((ROW_TILE, D_MODEL), row),
        compiler_params=_params(("parallel",)),
        name="outmlp",
    )(ya, yb, yc, x, woa, wob, woc, g1, g2, g3, w1, w2)


def _pad_axis(x, axis, size):
    pads = [(0, 0)] * x.ndim
    pads[axis] = (0, size - x.shape[axis])
    return jnp.pad(x, pads)


def _layer(x, weights, tri, stacked, *, layer, depth, seq, band_tab, band_shift, q_pos0,
           past=None):
    (g_pre_mix, w_proj, w_proj_tail, w_proj_t, w_pool_blk, pool_scale, woa, wob, woc, g_post_mix,
     g_pre_mlp, w1, w2, g_post_mlp) = weights
    bn = x.shape[0] // seq
    (qa, qb, qi, u, wit), stacked = _proj(x, g_pre_mix, w_proj, w_proj_tail, w_proj_t, stacked,
                                          layer=layer, depth=depth)
    per_batch = lambda t: t.reshape(bn, seq, t.shape[-1])
    qa, qb, qi, u = map(per_batch, (qa, qb, qi, u))
    ka_all, va_all, kb_all, vb_all, ki2_all = (t.reshape(depth * bn, seq, t.shape[-1])
                                               for t in stacked)

    if past is None:
        k_batch0 = layer * bn
        prev16 = jnp.zeros((bn, POOL_MAX, C_WIDTH), F32)
        n_keys = seq
        tq, qa_q, qi_q, qb_q, wit_q = KEY_TILE, qa, qi, qb, wit
    else:
        c_a_k, c_a_v, c_b_k, c_b_v, c_b_kidx, c_pool = past
        k_batch0 = 0
        ka, va, kb, vb, ki2 = (t[layer].reshape(bn, seq, t.shape[-1]) for t in stacked)
        flat = lambda t: t.reshape(bn, t.shape[1], ATT_WIDTH)
        band_rows = 3 * KEY_TILE
        ka_all = _pad_axis(jnp.concatenate([flat(c_a_k), ka], axis=1), 1, band_rows)
        va_all = _pad_axis(jnp.concatenate([flat(c_a_v), va], axis=1), 1, band_rows)
        n_keys = c_b_k.shape[1] + seq
        key_rows = -(-n_keys // KEY_TILE) * KEY_TILE
        kb_all = _pad_axis(jnp.concatenate([flat(c_b_k), kb], axis=1), 1, key_rows)
        vb_all = _pad_axis(jnp.concatenate([flat(c_b_v), vb], axis=1), 1, key_rows)
        c_ki2 = jnp.concatenate([c_b_kidx, c_b_kidx], axis=-1)
        ki2_all = _pad_axis(jnp.concatenate([c_ki2, ki2], axis=1), 1, key_rows)
        prev16 = jnp.pad(c_pool, ((0, 0), (1, 0), (0, 0)))
        tq = LANES
        qa_q, qi_q, qb_q = (_pad_axis(t, 1, LANES) for t in (qa, qi, qb))
        wit_q = _pad_axis(wit.reshape(LANES, bn, seq), 2, LANES).reshape(LANES, bn * LANES)

    ya = _band(qa_q, ka_all, va_all, band_tab, tq=tq, shift=band_shift,
               k_batch0=k_batch0)[:, :seq]
    yb = _sparse(qi_q, wit_q, ki2_all, qb_q, kb_all, vb_all, tri, tq=tq, n_real=min(seq, tq),
                 q_pos0=q_pos0, n_keys=n_keys, k_batch0=k_batch0)[:, :seq]
    yc = _pool(u, prev16, w_pool_blk, pool_scale, pos0=q_pos0)
    flat2 = lambda t: t.reshape(bn * seq, t.shape[-1])
    x = _outmlp(flat2(ya), flat2(yb), flat2(yc), x, woa, wob, woc,
                g_post_mix, g_pre_mlp, g_post_mlp, w1, w2)
    return x, stacked, u


def _state_leaves(stacked, depth, bn, seq, n_keep):
    ka, va, kb, vb, ki2 = (t.reshape(depth, bn, seq, t.shape[-1]) for t in stacked)
    heads = lambda t: t.reshape(depth, bn, t.shape[2], N_HEADS, HEAD_DIM)
    return (heads(ka[:, :, seq - n_keep:]), heads(va[:, :, seq - n_keep:]), heads(kb), heads(vb),
            ki2[..., :IDX_DIM])


def kernel(x_prompt, x_sample, cache_a_k, cache_a_v, cache_b_k, cache_b_v, cache_b_kidx, state_pool, g_pre_mix, w_in, rel_bias, w_pool, pool_scale, w_out, g_post_mix, g_pre_mlp, w_ff1, w_ff2, g_post_mlp):
    batch, seq, _ = x_prompt.shape
    dec_batch, dec_seq, _ = x_sample.shape
    depth = w_in.shape[0]
    past_len = cache_b_k.shape[2]
    n_a = cache_a_k.shape[2]
    assert seq % KEY_TILE == 0 and (batch * seq) % ROW_TILE == 0
    assert (dec_batch * dec_seq) % ROW_TILE == 0 and n_a + dec_seq <= 3 * KEY_TILE
    assert dec_seq <= LANES

    tri = (jnp.arange(KEY_TILE)[:, None] >= jnp.arange(KEY_TILE)[None, :]).astype(BF16)
    xp = x_prompt.reshape(batch * seq, D_MODEL)
    xs = x_sample.reshape(dec_batch * dec_seq, D_MODEL)
    p_stacked = s_stacked = None
    p_pools, s_pools = [], []
    row = lambda t: t.reshape(1, -1)
    for l in range(depth):
        weights = (row(g_pre_mix[l]), *_proj_weight(w_in[l]), _pool_weight(w_pool[l]),
                   row(pool_scale[l]),
                   w_out[l, :ATT_WIDTH].astype(BF16),
                   w_out[l, ATT_WIDTH:2 * ATT_WIDTH].astype(BF16),
                   w_out[l, 2 * ATT_WIDTH:].astype(BF16),
                   row(g_post_mix[l]), row(g_pre_mlp[l]),
                   w_ff1[l].astype(BF16), w_ff2[l].astype(BF16), row(g_post_mlp[l]))
        xp, p_stacked, u = _layer(
            xp, weights, tri, p_stacked, layer=l, depth=depth, seq=seq,
            band_tab=_band_table(rel_bias[l], 2 * KEY_TILE, 0, KEY_TILE, 3 * KEY_TILE),
            band_shift=2, q_pos0=0)
        p_pools.append(u[:, seq - (POOL_MAX - 1):])
        xs, s_stacked, u = _layer(
            xs, weights, tri, s_stacked, layer=l, depth=depth, seq=dec_seq,
            band_tab=_band_table(rel_bias[l], past_len, past_len - n_a, LANES, n_a + dec_seq),
            band_shift=0, q_pos0=past_len,
            past=(cache_a_k[l], cache_a_v[l], cache_b_k[l], cache_b_v[l], cache_b_kidx[l],
                  state_pool[l]))
        s_pools.append(jnp.concatenate([state_pool[l], u], axis=1)[:, dec_seq:])
    return ((xp.reshape(batch, seq, D_MODEL), xs.reshape(dec_batch, dec_seq, D_MODEL))
            + _state_leaves(p_stacked, depth, batch, seq, min(A_WINDOW, seq))
            + (jnp.stack(p_pools, axis=0),)
            + _state_leaves(s_stacked, depth, dec_batch, dec_seq, dec_seq)
            + (jnp.stack(s_pools, axis=0),))
```
